```python
import math
import jax, jax.numpy as jnp
from jax import lax
import numpy as np

D_MODEL = 1024
BATCH = 2
SEQ = 16384
DEPTH = 4

CTX_LEN = 256
GRID_W = 64
D_FF = 2816
RG_WIDTH = 512
RG_HEADS = 8
RG_HEAD_DIM = RG_WIDTH // RG_HEADS
RG_C = 8.0
CONV_W = 4
GLA_HEADS = 4
GLA_DK = 64
GLA_DV = 128
GLA_KW = GLA_HEADS * GLA_DK
GLA_VW = GLA_HEADS * GLA_DV
GLA_RANK = 16
GLA_GATE_NORM = 16.0
GLA_CHUNK = 64
MIX_WIDTH = RG_WIDTH + GLA_VW
IN_WIDTH = 2 * RG_WIDTH + 2 * GLA_KW + 2 * GLA_VW + 2 * GLA_RANK
N_MOD = 9
EPS = 1e-6

kernel_name = "hybrid_rglru_gla_macaron_dit"


def rmsnorm(x, g):
    xf = x.astype(jnp.float32)
    y = xf * lax.rsqrt(jnp.mean(xf * xf, axis=-1, keepdims=True) + EPS)
    return (y * g.astype(jnp.float32)).astype(x.dtype)


def ada_in(h, m, g, j):
    return rmsnorm(h, g) * (1 + m[:, 3 * j + 1][:, None]) + m[:, 3 * j][:, None]


def ada_gate(m, j):
    return m[:, 3 * j + 2][:, None]


def swiglu(u, w1, w3, w2):
    return (jax.nn.silu(u @ w1) * (u @ w3)) @ w2


def split_proj(p):
    outs = []
    off = 0
    for w in (RG_WIDTH, RG_WIDTH, GLA_KW, GLA_KW, GLA_VW, GLA_VW, 2 * GLA_RANK):
        outs.append(p[..., off:off + w])
        off += w
    return outs


def short_conv(x, w, b):
    t = x.shape[1]
    xp = jnp.pad(x, ((0, 0), (2, 1), (0, 0)))
    y = xp[:, 0:t] * w[0]
    for k in range(1, CONV_W):
        y = y + xp[:, k:k + t] * w[k]
    return y + b


def _combine(e1, e2):
    a1, b1 = e1
    a2, b2 = e2
    return a1 * a2, a2 * b1 + b2


def linear_scan(a, b, h0, reverse):
    if reverse:
        a, b = a[:, ::-1], b[:, ::-1]
    b = b.at[:, 0].add(a[:, 0] * h0)
    _, h = lax.associative_scan(_combine, (a, b), axis=1)
    return h[:, ::-1] if reverse else h


def rg_lru_dir(xc, h0, lam, wa, ba, wi, bi, reverse):
    bsz, t, _ = xc.shape
    xh = xc.reshape(bsz, t, RG_HEADS, RG_HEAD_DIM)
    r = jax.nn.sigmoid(jnp.einsum('bthi,hij->bthj', xh, wa).reshape(bsz, t, RG_WIDTH) + ba)
    i = jax.nn.sigmoid(jnp.einsum('bthi,hij->bthj', xh, wi).reshape(bsz, t, RG_WIDTH) + bi)
    log_a = -RG_C * r * jax.nn.softplus(-lam)
    a = jnp.exp(log_a)
    inp = jnp.sqrt(-jnp.expm1(2 * log_a)) * (i * xc)
    return linear_scan(a, inp, h0, reverse)


def to_heads(x, n_heads):
    bsz, t, _ = x.shape
    return x.reshape(bsz, t, n_heads, -1).transpose(0, 2, 1, 3)


def col_major(x, rows):
    bsz, h, t, d = x.shape
    return jnp.swapaxes(x.reshape(bsz, h, rows, GRID_W, d), 2, 3).reshape(bsz, h, t, d)


def row_major(x, rows):
    bsz, h, t, d = x.shape
    return jnp.swapaxes(x.reshape(bsz, h, GRID_W, rows, d), 2, 3).reshape(bsz, h, t, d)


def gla_chunked(q, k, v, g, s0):
    bsz, h, t, _ = q.shape
    dv = v.shape[-1]
    n = t // GLA_CHUNK

    def chunks(z):
        return jnp.moveaxis(z.reshape(bsz, h, n, GLA_CHUNK, z.shape[-1]), 2, 0)

    mask = jnp.tril(jnp.ones((GLA_CHUNK, GLA_CHUNK), dtype=bool))[:, :, None]

    def step(s, inp):
        qc, kc, vc, gc = inp
        b = jnp.cumsum(gc, axis=2)
        o_inter = jnp.einsum('bhtk,bhkv->bhtv', qc * jnp.exp(b), s)
        diff = b[:, :, :, None, :] - b[:, :, None, :, :]
        decay = jnp.where(mask, jnp.exp(jnp.minimum(diff, 0.0)), 0.0)
        att = jnp.einsum('bhtk,bhsk,bhtsk->bhts', qc, kc, decay)
        o = o_inter + jnp.einsum('bhts,bhsv->bhtv', att, vc)
        b_last = b[:, :, -1:, :]
        s_new = (jnp.exp(b_last[:, :, 0, :])[..., None] * s
                 + jnp.einsum('bhsk,bhsv->bhkv', kc * jnp.exp(b_last - b), vc))
        return s_new, o

    s_fin, o = lax.scan(step, s0, (chunks(q), chunks(k), chunks(v), chunks(g)))
    return jnp.moveaxis(o, 0, 2).reshape(bsz, h, t, dv), s_fin


def gla_log_gate(lr, wup, bup, d):
    z = lr[..., d * GLA_RANK:(d + 1) * GLA_RANK] @ wup + bup
    return jax.nn.log_sigmoid(z.astype(jnp.float32)) / GLA_GATE_NORM


def gla_out(o, og, gnorm):
    bsz, h, t, dv = o.shape
    o = rmsnorm(o.transpose(0, 2, 1, 3), gnorm.reshape(h, dv)).reshape(bsz, t, h * dv)
    return o.astype(og.dtype) * jax.nn.silu(og)


def mixer(u_c, u_l, w_in, conv_w, conv_b, lam, wa, ba, wi, bi, wup, bup, gnorm, w_out, need_ctx):
    bsz = u_l.shape[0]
    rows = u_l.shape[1] // GRID_W
    xr_c, gr_c, q_c, k_c, v_c, og_c, lr_c = split_proj(u_c @ w_in)
    xr_l, gr_l, q_l, k_l, v_l, og_l, lr_l = split_proj(u_l @ w_in)

    xc_c = short_conv(xr_c, conv_w, conv_b)
    xc_l = short_conv(xr_l, conv_w, conv_b)
    zero_h = jnp.zeros((bsz, RG_WIDTH), xc_c.dtype)
    h_c_dirs, h_l_dirs = [], []
    for d, rev in ((0, False), (1, True)):
        h_c = rg_lru_dir(xc_c, zero_h, lam[d], wa[d], ba[d], wi[d], bi[d], rev)
        h_end = h_c[:, 0] if rev else h_c[:, -1]
        h_l = rg_lru_dir(xc_l, h_end, lam[d], wa[d], ba[d], wi[d], bi[d], rev)
        h_c_dirs.append(h_c)
        h_l_dirs.append(h_l)
    rg_l = jax.nn.gelu(gr_l) * (h_l_dirs[0] + h_l_dirs[1])

    qs = GLA_DK ** -0.5
    hc = (to_heads(q_c * qs, GLA_HEADS), to_heads(k_c, GLA_HEADS), to_heads(v_c, GLA_HEADS))
    hl = tuple(col_major(to_heads(z, GLA_HEADS), rows) for z in (q_l * qs, k_l, v_l))
    s0 = jnp.zeros((bsz, GLA_HEADS, GLA_DK, GLA_DV), jnp.float32)
    o_c_dirs, o_l_dirs = [], []
    for d, rev in ((0, False), (1, True)):
        g_c = to_heads(gla_log_gate(lr_c, wup[d], bup[d], d), GLA_HEADS)
        g_l = col_major(to_heads(gla_log_gate(lr_l, wup[d], bup[d], d), GLA_HEADS), rows)
        ac = hc + (g_c,)
        al = hl + (g_l,)
        if rev:
            ac = tuple(z[:, :, ::-1] for z in ac)
            al = tuple(z[:, :, ::-1] for z in al)
        o_c, s_c = gla_chunked(*ac, s0)
        o_l, _ = gla_chunked(*al, s_c)
        if rev:
            o_c, o_l = o_c[:, :, ::-1], o_l[:, :, ::-1]
        o_c_dirs.append(o_c)
        o_l_dirs.append(o_l)
    gla_l = gla_out(row_major(o_l_dirs[0] + o_l_dirs[1], rows), og_l, gnorm)
    y_l = (jnp.concatenate([rg_l, gla_l.astype(rg_l.dtype)], axis=-1) @ w_out).astype(u_l.dtype)

    y_c = None
    if need_ctx:
        rg_c = jax.nn.gelu(gr_c) * (h_c_dirs[0] + h_c_dirs[1])
        gla_c = gla_out(o_c_dirs[0] + o_c_dirs[1], og_c, gnorm)
        y_c = (jnp.concatenate([rg_c, gla_c.astype(rg_c.dtype)], axis=-1) @ w_out).astype(u_c.dtype)
    return y_c, y_l


def setup_inputs(seed: int = 0) -> dict:
    key = jax.random.key(seed)
    ks = jax.random.split(key, 32)

    def nrm(k, shape, scale):
        return jax.random.normal(k, shape, jnp.float32) * scale

    a0 = jax.random.uniform(ks[13], (DEPTH, 2, RG_WIDTH), jnp.float32, 0.9, 0.999)
    s = a0 ** (1.0 / RG_C)
    rg_lam = jnp.log(s) - jnp.log1p(-s)
    return {
        "x": nrm(ks[0], (BATCH, SEQ, D_MODEL), 1.0),
        "c": nrm(ks[1], (BATCH, D_MODEL), 1.0),
        "ctx": nrm(ks[2], (BATCH, CTX_LEN, D_MODEL), 1.0),
        "c_ctx": nrm(ks[3], (D_MODEL,), 1.0),
        "w_mod": nrm(ks[4], (DEPTH, D_MODEL, N_MOD * D_MODEL), 0.5 * D_MODEL ** -0.5),
        "b_mod": nrm(ks[5], (DEPTH, N_MOD * D_MODEL), 0.02),
        "norm_g": 1.0 + nrm(ks[6], (DEPTH, 3, D_MODEL), 0.02),
        "ffn_w1": nrm(ks[7], (DEPTH, 2, D_MODEL, D_FF), D_MODEL ** -0.5),
        "ffn_w3": nrm(ks[8], (DEPTH, 2, D_MODEL, D_FF), D_MODEL ** -0.5),
        "ffn_w2": nrm(ks[9], (DEPTH, 2, D_FF, D_MODEL), D_FF ** -0.5),
        "w_in": nrm(ks[10], (DEPTH, D_MODEL, IN_WIDTH), D_MODEL ** -0.5),
        "conv_w": nrm(ks[11], (DEPTH, CONV_W, RG_WIDTH), CONV_W ** -0.5),
        "conv_b": nrm(ks[12], (DEPTH, RG_WIDTH), 0.02),
        "rg_lam": rg_lam,
        "rg_wa": nrm(ks[14], (DEPTH, 2, RG_HEADS, RG_HEAD_DIM, RG_HEAD_DIM), RG_HEAD_DIM ** -0.5),
        "rg_ba": nrm(ks[15], (DEPTH, 2, RG_WIDTH), 0.02),
        "rg_wi": nrm(ks[16], (DEPTH, 2, RG_HEADS, RG_HEAD_DIM, RG_HEAD_DIM), RG_HEAD_DIM ** -0.5),
        "rg_bi": nrm(ks[17], (DEPTH, 2, RG_WIDTH), 0.02),
        "gla_wup": nrm(ks[18], (DEPTH, 2, GLA_RANK, GLA_KW), GLA_RANK ** -0.5),
        "gla_bup": nrm(ks[19], (DEPTH, 2, GLA_KW), 0.1),
        "gla_norm_g": 1.0 + nrm(ks[20], (DEPTH, GLA_VW), 0.02),
        "w_out": nrm(ks[21], (DEPTH, MIX_WIDTH, D_MODEL), MIX_WIDTH ** -0.5),
        "final_g": 1.0 + nrm(ks[22], (D_MODEL,), 0.02),
    }


def reference(x, c, ctx, c_ctx, w_mod, b_mod, norm_g, ffn_w1, ffn_w3, ffn_w2, w_in,
              conv_w, conv_b, rg_lam, rg_wa, rg_ba, rg_wi, rg_bi, gla_wup, gla_bup,
              gla_norm_g, w_out, final_g):
    bsz = x.shape[0]
    h_l = x
    h_c = ctx
    sc = jax.nn.silu(c)
    sc_ctx = jax.nn.silu(c_ctx)[None]
    for l in range(DEPTH):
        last = l == DEPTH - 1
        m_l = (sc @ w_mod[l] + b_mod[l]).reshape(bsz, N_MOD, D_MODEL)
        m_c = (sc_ctx @ w_mod[l] + b_mod[l]).reshape(1, N_MOD, D_MODEL)

        h_l = h_l + 0.5 * ada_gate(m_l, 0) * swiglu(ada_in(h_l, m_l, norm_g[l, 0], 0),
                                                     ffn_w1[l, 0], ffn_w3[l, 0], ffn_w2[l, 0])
        h_c = h_c + 0.5 * ada_gate(m_c, 0) * swiglu(ada_in(h_c, m_c, norm_g[l, 0], 0),
                                                     ffn_w1[l, 0], ffn_w3[l, 0], ffn_w2[l, 0])

        u_l = ada_in(h_l, m_l, norm_g[l, 1], 1)
        u_c = ada_in(h_c, m_c, norm_g[l, 1], 1)
        y_c, y_l = mixer(u_c, u_l, w_in[l], conv_w[l], conv_b[l], rg_lam[l], rg_wa[l], rg_ba[l],
                         rg_wi[l], rg_bi[l], gla_wup[l], gla_bup[l], gla_norm_g[l], w_out[l],
                         not last)
        h_l = h_l + ada_gate(m_l, 1) * y_l

        h_l = h_l + 0.5 * ada_gate(m_l, 2) * swiglu(ada_in(h_l, m_l, norm_g[l, 2], 2),
                                                     ffn_w1[l, 1], ffn_w3[l, 1], ffn_w2[l, 1])
        if not last:
            h_c = h_c + ada_gate(m_c, 1) * y_c
            h_c = h_c + 0.5 * ada_gate(m_c, 2) * swiglu(ada_in(h_c, m_c, norm_g[l, 2], 2),
                                                         ffn_w1[l, 1], ffn_w3[l, 1], ffn_w2[l, 1])
    return rmsnorm(h_l, final_g)
```

```python
import functools

import jax
import jax.numpy as jnp
from jax import lax
from jax.experimental import pallas as pl
from jax.experimental.pallas import tpu as pltpu

F32 = jnp.float32
BF16 = jnp.bfloat16

LANES = 128
SUBLANES = 8
TOKEN_BLOCK = 256
VMEM_LIMIT = 56 * 1024 * 1024

N_MOD = 9
EPS = 1e-6
GRID_W = 64
RG_WIDTH = 512
RG_HEADS = 8
RG_C = 8.0
GLA_HEADS = 4
GLA_DK = 64
GLA_DV = 128
GLA_KW = GLA_HEADS * GLA_DK
GLA_VW = GLA_HEADS * GLA_DV
GLA_RANK = 16
GLA_GATE_NORM = 16.0
GLA_CHUNK = 64


def _params(sem):
    return pltpu.CompilerParams(dimension_semantics=sem, vmem_limit_bytes=VMEM_LIMIT)


def _const_spec(shape):
    nd = len(shape)
    return pl.BlockSpec(shape, lambda *_: (0,) * nd, pipeline_mode=pl.Buffered(1))


def _sigmoid(x):
    return 1.0 / (1.0 + jnp.exp(-x))


def _softplus(x):
    return jnp.maximum(x, 0.0) + jnp.log1p(jnp.exp(-jnp.abs(x)))


def _ada_norm(x, m, g, j):
    ms = jnp.mean(x * x, axis=-1, keepdims=True)
    y = x * lax.rsqrt(ms + EPS) * g
    return y * (1.0 + m[3 * j + 1:3 * j + 2]) + m[3 * j:3 * j + 1]


def _mod_kernel(c_ref, w_ref, b_ref, o_ref):
    c = c_ref[...]
    sc = c * _sigmoid(c)
    o_ref[...] = jnp.dot(sc, w_ref[...], preferred_element_type=F32,
                         precision=lax.Precision.HIGHEST) + b_ref[...]


def _mod_call(cvec, w_mod, b_mod):
    depth, d, n = w_mod.shape
    rows = cvec.shape[0]
    tn = n // 8
    return pl.pallas_call(
        _mod_kernel,
        grid=(depth, n // tn),
        in_specs=[
            pl.BlockSpec((rows, d), lambda l, i: (0, 0)),
            pl.BlockSpec((None, d, tn), lambda l, i: (l, 0, i)),
            pl.BlockSpec((None, 1, tn), lambda l, i: (l, 0, i)),
        ],
        out_specs=pl.BlockSpec((None, rows, tn), lambda l, i: (l, 0, i)),
        out_shape=jax.ShapeDtypeStruct((depth, rows, n), F32),
        compiler_params=_params(("parallel", "parallel")),
        name="mod",
    )(cvec, w_mod, b_mod.reshape(depth, 1, n))


def _mod_index(n_ctx_blocks):
    def index(b, i):
        return (jnp.where(i < n_ctx_blocks, 0, b + 1), 0, 0)
    return index


def _ffn_kernel(x_ref, m_ref, g_ref, w1_ref, w3_ref, w2_ref, fg_ref, o_ref, *, j, final):
    x = x_ref[...]
    m = m_ref[...]
    u = _ada_norm(x, m, g_ref[...], j).astype(BF16)
    h1 = jnp.dot(u, w1_ref[...], preferred_element_type=F32)
    h3 = jnp.dot(u, w3_ref[...], preferred_element_type=F32)
    act = (h1 * _sigmoid(h1) * h3).astype(BF16)
    y = jnp.dot(act, w2_ref[...], preferred_element_type=F32)
    out = x + (0.5 * m[3 * j + 2:3 * j + 3]) * y
    if final:
        ms = jnp.mean(out * out, axis=-1, keepdims=True)
        out = out * lax.rsqrt(ms + EPS) * fg_ref[...]
    o_ref[...] = out


def _ffn_call(h, mods_l, g_row, w1, w3, w2, final_g, *, j, n_ctx_blocks, latents_only, final):
    bsz, s, d = h.shape
    tm = TOKEN_BLOCK
    dff = w1.shape[1]
    skip = n_ctx_blocks if latents_only else 0
    nblk = s // tm - skip
    mod_index = _mod_index(n_ctx_blocks)
    return pl.pallas_call(
        functools.partial(_ffn_kernel, j=j, final=final),
        grid=(bsz, nblk),
        in_specs=[
            pl.BlockSpec((None, tm, d), lambda b, i: (b, i + skip, 0)),
            pl.BlockSpec((None, N_MOD, d), lambda b, i: mod_index(b, i + skip)),
            _const_spec((1, d)),
            _const_spec((d, dff)),
            _const_spec((d, dff)),
            _const_spec((dff, d)),
            _const_spec((1, d)),
        ],
        out_specs=pl.BlockSpec((None, tm, d), lambda b, i: (b, i, 0)),
        out_shape=jax.ShapeDtypeStruct((bsz, nblk * tm, d), F32),
        compiler_params=_params(("parallel", "parallel")),
        name="ffn",
    )(h, mods_l, g_row, w1, w3, w2, final_g)


def _proj_kernel(x_ref, m_ref, g_ref, w_ref, wup_ref, bup_ref,
                 xr_ref, gr_ref, q_ref, k_ref, v_ref, og_ref, lg_ref):
    u = _ada_norm(x_ref[...], m_ref[...], g_ref[...], 1).astype(BF16)
    p = jnp.dot(u, w_ref[...], preferred_element_type=F32)
    o = 0
    xr_ref[...] = p[:, o:o + RG_WIDTH]; o += RG_WIDTH
    gr_ref[...] = p[:, o:o + RG_WIDTH]; o += RG_WIDTH
    q_ref[...] = p[:, o:o + GLA_KW] * (GLA_DK ** -0.5); o += GLA_KW
    k_ref[...] = p[:, o:o + GLA_KW]; o += GLA_KW
    v_ref[...] = p[:, o:o + GLA_VW]; o += GLA_VW
    og_ref[...] = p[:, o:o + GLA_VW]; o += GLA_VW
    lr = p[:, o:o + 2 * GLA_RANK].astype(BF16)
    z = jnp.dot(lr, wup_ref[...], preferred_element_type=F32) + bup_ref[...]
    lg_ref[...] = (-_softplus(-z)) / GLA_GATE_NORM


def _proj_call(h, mods_l, g_row, w_in, wup_bd, bup_row, *, n_ctx_blocks):
    bsz, s, d = h.shape
    tm = TOKEN_BLOCK
    widths = (RG_WIDTH, RG_WIDTH, GLA_KW, GLA_KW, GLA_VW, GLA_VW, 2 * GLA_KW)
    return pl.pallas_call(
        _proj_kernel,
        grid=(bsz, s // tm),
        in_specs=[
            pl.BlockSpec((None, tm, d), lambda b, i: (b, i, 0)),
            pl.BlockSpec((None, N_MOD, d), _mod_index(n_ctx_blocks)),
            _const_spec((1, d)),
            _const_spec(w_in.shape),
            _const_spec(wup_bd.shape),
            _const_spec(bup_row.shape),
        ],
        out_specs=[pl.BlockSpec((None, tm, w), lambda b, i: (b, i, 0)) for w in widths],
        out_shape=[jax.ShapeDtypeStruct((bsz, s, w), F32) for w in widths],
        compiler_params=_params(("parallel", "parallel")),
        name="proj",
    )(h, mods_l, g_row, w_in, wup_bd, bup_row)


def _rg_block(j, nblk, n_ctx_blocks, reverse):
    if not reverse:
        return j
    return jnp.where(j < n_ctx_blocks, n_ctx_blocks - 1 - j, nblk - 1 - (j - n_ctx_blocks))


def _rg_kernel(x_ref, xp_ref, xn_ref, cw_ref, cb_ref, lam_ref, wg_ref, ba_ref, bi_ref,
               o_ref, carry_ref, *, reverse, nblk, n_ctx_blocks):
    tm = TOKEN_BLOCK
    j = pl.program_id(1)
    blk = _rg_block(j, nblk, n_ctx_blocks, reverse)

    @pl.when(j == 0)
    def _():
        carry_ref[...] = jnp.zeros_like(carry_ref)

    seg_first = (blk == 0) | (blk == n_ctx_blocks)
    seg_last = (blk == n_ctx_blocks - 1) | (blk == nblk - 1)
    x = x_ref[...]
    xp = jnp.where(seg_first, 0.0, xp_ref[...])
    xn = jnp.where(seg_last, 0.0, xn_ref[...])
    row8 = lax.broadcasted_iota(jnp.int32, (SUBLANES, 1), 0)

    def shift_down(k):
        sh = pltpu.roll(x, k, axis=0)
        first = jnp.where(row8 < k, pltpu.roll(xp, k, axis=0), sh[0:SUBLANES])
        return jnp.concatenate([first, sh[SUBLANES:]], axis=0)

    def shift_up1():
        sh = pltpu.roll(x, tm - 1, axis=0)
        last = jnp.where(row8 == SUBLANES - 1, pltpu.roll(xn, SUBLANES - 1, axis=0),
                         sh[tm - SUBLANES:])
        return jnp.concatenate([sh[:tm - SUBLANES], last], axis=0)

    cw = cw_ref[...]
    xc = shift_down(2) * cw[0:1] + shift_down(1) * cw[1:2]
    xc = xc + x * cw[2:3]
    xc = xc + shift_up1() * cw[3:4]
    xc = xc + cb_ref[...]

    sp = _softplus(-lam_ref[...])
    row = lax.broadcasted_iota(jnp.int32, (tm, 1), 0)
    for g in range(RG_WIDTH // LANES):
        sl = slice(g * LANES, (g + 1) * LANES)
        xg = xc[:, sl]
        pre = jnp.dot(xg.astype(BF16), wg_ref[g], preferred_element_type=F32)
        r = _sigmoid(pre[:, :LANES] + ba_ref[:, sl])
        i = _sigmoid(pre[:, LANES:] + bi_ref[:, sl])
        log_a = (-RG_C) * r * sp[:, sl]
        a = jnp.exp(log_a)
        bv = jnp.sqrt(1.0 - a * a) * (i * xg)
        s = 1
        while s < tm:
            if not reverse:
                a_sh = pltpu.roll(a, s, axis=0)
                b_sh = pltpu.roll(bv, s, axis=0)
                valid = row >= s
            else:
                a_sh = pltpu.roll(a, tm - s, axis=0)
                b_sh = pltpu.roll(bv, tm - s, axis=0)
                valid = row < tm - s
            bv = jnp.where(valid, a * b_sh + bv, bv)
            a = jnp.where(valid, a * a_sh, a)
            s *= 2
        h = a * carry_ref[:, sl] + bv
        o_ref[:, sl] = h
        carry_ref[:, sl] = h[0:1] if reverse else h[tm - 1:tm]


def _rg_call(xr, conv_w, conv_b, lam, wg, ba, bi, *, reverse, n_ctx_blocks):
    bsz, s, w = xr.shape
    tm = TOKEN_BLOCK
    nblk = s // tm
    per = tm // SUBLANES
    n8 = s // SUBLANES
    blk = functools.partial(_rg_block, nblk=nblk, n_ctx_blocks=n_ctx_blocks, reverse=reverse)
    return pl.pallas_call(
        functools.partial(_rg_kernel, reverse=reverse, nblk=nblk, n_ctx_blocks=n_ctx_blocks),
        grid=(bsz, nblk),
        in_specs=[
            pl.BlockSpec((None, tm, w), lambda b, j: (b, blk(j), 0)),
            pl.BlockSpec((None, SUBLANES, w), lambda b, j: (b, jnp.maximum(blk(j) * per - 1, 0), 0)),
            pl.BlockSpec((None, SUBLANES, w), lambda b, j: (b, jnp.minimum((blk(j) + 1) * per, n8 - 1), 0)),
            _const_spec(conv_w.shape),
            _const_spec((1, w)),
            _const_spec((1, w)),
            _const_spec(wg.shape),
            _const_spec((1, w)),
            _const_spec((1, w)),
        ],
        out_specs=pl.BlockSpec((None, tm, w), lambda b, j: (b, blk(j), 0)),
        out_shape=jax.ShapeDtypeStruct((bsz, s, w), F32),
        scratch_shapes=[pltpu.VMEM((1, w), F32)],
        compiler_params=_params(("arbitrary", "arbitrary")),
        name="rg_bwd" if reverse else "rg_fwd",
    )(xr, xr, xr, conv_w, conv_b, lam, wg, ba, bi)


def _level_ref(b, m, r):
    c, w = b.shape
    if m >= SUBLANES:
        b3 = b.reshape(c // m, m, w)
        return jnp.broadcast_to(b3[:, r:r + 1, :], (c // m, m, w)).reshape(c, w)
    bt = b.reshape(c // SUBLANES, SUBLANES, w)
    sub = lax.broadcasted_iota(jnp.int32, (1, SUBLANES, 1), 1)
    out = None
    for blk in range(SUBLANES // m):
        src = blk * m + r
        piece = jnp.broadcast_to(bt[:, src:src + 1, :], bt.shape)
        out = piece if out is None else jnp.where(sub >= blk * m, piece, out)
    return out.reshape(c, w)


def _dot_nt(a, b):
    return lax.dot_general(a, b, (((1,), (1,)), ((), ())), preferred_element_type=F32)


def _dot_tn(a, b):
    return lax.dot_general(a, b, (((0,), (0,)), ((), ())), preferred_element_type=F32)


def _gla_chunk(q, k, v, g, st, consts, reverse):
    c = GLA_CHUNK
    tri, key_head_masks, val_head_masks, state_mask, level_masks, diag_mask, row = consts
    g1 = g.astype(BF16)
    g2 = (g - g1.astype(F32)).astype(BF16)
    g3 = (g - g1.astype(F32) - g2.astype(F32)).astype(BF16)
    b = (jnp.dot(tri, g1, preferred_element_type=F32)
         + jnp.dot(tri, g2, preferred_element_type=F32)
         + jnp.dot(tri, g3, preferred_element_type=F32))
    b_last = b[0:1] if reverse else b[c - 1:c]

    def key_blockdiag(x):
        return jnp.concatenate([jnp.where(mk, x, 0.0) for mk in key_head_masks], axis=0).astype(BF16)

    att = jnp.where(diag_mask, _dot_nt(q.astype(BF16), key_blockdiag(k)), 0.0)
    m = c
    for lvl_mask in level_masks:
        half = m // 2
        ref = _level_ref(b, m, half if reverse else half - 1)
        in_late = (row % m) >= half
        q_rows = ~in_late if reverse else in_late
        qd = jnp.where(q_rows, jnp.exp(jnp.minimum(b - ref, 0.0)), 0.0) * q
        kd = jnp.where(q_rows, 0.0, jnp.exp(jnp.minimum(ref - b, 0.0))) * k
        a_l = _dot_nt(qd.astype(BF16), key_blockdiag(kd))
        att = jnp.where(lvl_mask, a_l, att)
        m = half
    v_bd = jnp.concatenate([jnp.where(mv, v, 0.0) for mv in val_head_masks], axis=0).astype(BF16)
    o = jnp.dot(att.astype(BF16), v_bd, preferred_element_type=F32)
    o = o + _dot_nt((q * jnp.exp(b)).astype(BF16), st.astype(BF16))
    k_hat = k * jnp.exp(b_last - b)
    upd = _dot_tn(v.astype(BF16), k_hat.astype(BF16))
    st_new = st * jnp.exp(b_last) + jnp.where(state_mask, upd, 0.0)
    return o, st_new


def _gla_consts(reverse):
    c, kw, vw = GLA_CHUNK, GLA_KW, GLA_VW
    row = lax.broadcasted_iota(jnp.int32, (c, 1), 0)
    r2 = lax.broadcasted_iota(jnp.int32, (c, c), 0)
    c2 = lax.broadcasted_iota(jnp.int32, (c, c), 1)
    tri = jnp.where((c2 >= r2) if reverse else (c2 <= r2), 1.0, 0.0).astype(BF16)
    klane = lax.broadcasted_iota(jnp.int32, (1, kw), 1)
    vlane = lax.broadcasted_iota(jnp.int32, (1, vw), 1)
    key_head_masks = [(klane // GLA_DK) == h for h in range(GLA_HEADS)]
    val_head_masks = [(vlane // GLA_DV) == h for h in range(GLA_HEADS)]
    srow = lax.broadcasted_iota(jnp.int32, (vw, 1), 0)
    state_mask = (srow // GLA_DV) == (klane // GLA_DK)
    t = lax.broadcasted_iota(jnp.int32, (c, kw), 0)
    s = lax.broadcasted_iota(jnp.int32, (c, kw), 1) % c
    level_masks = []
    m = c
    while m >= 2:
        half = m // 2
        same = (t // m) == (s // m)
        t_late = (t % m) >= half
        s_late = (s % m) >= half
        if reverse:
            level_masks.append(same & (~t_late) & s_late)
        else:
            level_masks.append(same & t_late & (~s_late))
        m = half
    diag_mask = t == s
    return tri, key_head_masks, val_head_masks, state_mask, level_masks, diag_mask, row


def _gla_kernel(qf_ref, kf_ref, vf_ref, gf_ref, qb_ref, kb_ref, vb_ref, gb_ref,
                of_ref, ob_ref, st_ref):
    @pl.when(pl.program_id(0) == 0)
    def _():
        st_ref[...] = jnp.zeros_like(st_ref)

    bsz = qf_ref.shape[0]
    streams = ((0, qf_ref, kf_ref, vf_ref, gf_ref, of_ref), (1, qb_ref, kb_ref, vb_ref, gb_ref, ob_ref))
    for d, q_ref, k_ref, v_ref, g_ref, o_ref in streams:
        consts = _gla_consts(reverse=bool(d))
        for bi in range(bsz):
            o, st_new = _gla_chunk(q_ref[bi], k_ref[bi], v_ref[bi], g_ref[bi], st_ref[d, bi],
                                   consts, reverse=bool(d))
            o_ref[bi] = o
            st_ref[d, bi] = st_new


def _gla_call(q, k, v, lg, *, n_ctx_chunks):
    bsz, s, kw = q.shape
    vw = v.shape[2]
    c = GLA_CHUNK
    n = s // c

    def fwd(i):
        return (0, i, 0)

    def bwd_chunk(i):
        return jnp.where(i < n_ctx_chunks, n_ctx_chunks - 1 - i, n - 1 - (i - n_ctx_chunks))

    def bwd(i):
        return (0, bwd_chunk(i), 0)

    return pl.pallas_call(
        _gla_kernel,
        grid=(n,),
        in_specs=[
            pl.BlockSpec((bsz, c, kw), fwd),
            pl.BlockSpec((bsz, c, kw), fwd),
            pl.BlockSpec((bsz, c, vw), fwd),
            pl.BlockSpec((bsz, c, kw), fwd),
            pl.BlockSpec((bsz, c, kw), bwd),
            pl.BlockSpec((bsz, c, kw), bwd),
            pl.BlockSpec((bsz, c, vw), bwd),
            pl.BlockSpec((bsz, c, kw), lambda i: (0, bwd_chunk(i), 1)),
        ],
        out_specs=[pl.BlockSpec((bsz, c, vw), fwd), pl.BlockSpec((bsz, c, vw), bwd)],
        out_shape=[jax.ShapeDtypeStruct((bsz, s, vw), F32)] * 2,
        scratch_shapes=[pltpu.VMEM((2, bsz, vw, kw), F32)],
        compiler_params=_params(("arbitrary",)),
        name="gla",
    )(q, k, v, lg, q, k, v, lg)


def _out_kernel(h_ref, m_ref, gr_ref, hf_ref, hb_ref, o_ref, og_ref, gn_ref, w_ref, out_ref):
    gr = gr_ref[...]
    gelu = 0.5 * gr * (1.0 + jnp.tanh(0.7978845608028654 * (gr + 0.044715 * (gr * gr * gr))))
    rg = gelu * (hf_ref[...] + hb_ref[...])
    o = o_ref[...]
    og = og_ref[...]
    gn = gn_ref[...]
    parts = []
    for hd in range(GLA_HEADS):
        sl = slice(hd * GLA_DV, (hd + 1) * GLA_DV)
        oh = o[:, sl]
        ms = jnp.mean(oh * oh, axis=-1, keepdims=True)
        parts.append(oh * lax.rsqrt(ms + EPS) * gn[:, sl])
    gla = jnp.concatenate(parts, axis=-1) * (og * _sigmoid(og))
    mix = jnp.concatenate([rg, gla], axis=-1).astype(BF16)
    y = jnp.dot(mix, w_ref[...], preferred_element_type=F32)
    out_ref[...] = h_ref[...] + m_ref[5:6] * y


def _out_call(h, mods_l, gr, hf, hb, o, og, gnorm_row, w_out, *, n_ctx_blocks):
    bsz, s, d = h.shape
    tm = TOKEN_BLOCK

    def tok(w):
        return pl.BlockSpec((None, tm, w), lambda b, i: (b, i, 0))

    return pl.pallas_call(
        _out_kernel,
        grid=(bsz, s // tm),
        in_specs=[
            tok(d),
            pl.BlockSpec((None, N_MOD, d), _mod_index(n_ctx_blocks)),
            tok(RG_WIDTH), tok(RG_WIDTH), tok(RG_WIDTH), tok(GLA_VW), tok(GLA_VW),
            _const_spec(gnorm_row.shape),
            _const_spec(w_out.shape),
        ],
        out_specs=tok(d),
        out_shape=jax.ShapeDtypeStruct((bsz, s, d), F32),
        compiler_params=_params(("parallel", "parallel")),
        name="mix_out",
    )(h, mods_l, gr, hf, hb, o, og, gnorm_row, w_out)


def _to_scan_order(a, n_ctx):
    bsz, s, w = a.shape
    rows = (s - n_ctx) // GRID_W
    lat = a[:, n_ctx:].reshape(bsz, rows, GRID_W, w).swapaxes(1, 2).reshape(bsz, s - n_ctx, w)
    return jnp.concatenate([a[:, :n_ctx], lat], axis=1)


def _from_scan_order(a, n_ctx):
    bsz, s, w = a.shape
    rows = (s - n_ctx) // GRID_W
    lat = a[:, n_ctx:].reshape(bsz, GRID_W, rows, w).swapaxes(1, 2).reshape(bsz, s - n_ctx, w)
    return jnp.concatenate([a[:, :n_ctx], lat], axis=1)


def _pair_blockdiag(w):
    hd = w.shape[-1]
    per = LANES // hd
    wg = w.reshape(RG_HEADS // per, per, hd, hd)
    eye = jnp.eye(per, dtype=w.dtype)
    return jnp.einsum('gpij,pq->gpiqj', wg, eye).reshape(RG_HEADS // per, LANES, LANES)


def kernel(x, c, ctx, c_ctx, w_mod, b_mod, norm_g, ffn_w1, ffn_w3, ffn_w2, w_in, conv_w, conv_b,
           rg_lam, rg_wa, rg_ba, rg_wi, rg_bi, gla_wup, gla_bup, gla_norm_g, w_out, final_g):
    bsz, t, d = x.shape
    n_ctx = ctx.shape[1]
    depth = w_mod.shape[0]
    assert n_ctx % TOKEN_BLOCK == 0 and t % TOKEN_BLOCK == 0 and t % GRID_W == 0
    n_ctx_blocks = n_ctx // TOKEN_BLOCK
    n_ctx_chunks = n_ctx // GLA_CHUNK

    rows = -(-(bsz + 1) // SUBLANES) * SUBLANES
    cvec = jnp.zeros((rows, d), F32).at[0].set(c_ctx).at[1:bsz + 1].set(c)
    mods = _mod_call(cvec, w_mod, b_mod).reshape(depth, rows, N_MOD, d)

    h = jnp.concatenate([ctx, x], axis=1)
    fg = final_g.reshape(1, d)
    for l in range(depth):
        last = l == depth - 1
        m_l = mods[l]
        w1 = ffn_w1[l].astype(BF16)
        w3 = ffn_w3[l].astype(BF16)
        w2 = ffn_w2[l].astype(BF16)
        h = _ffn_call(h, m_l, norm_g[l, 0].reshape(1, d), w1[0], w3[0], w2[0], fg,
                      j=0, n_ctx_blocks=n_ctx_blocks, latents_only=False, final=False)

        zero = jnp.zeros((GLA_RANK, GLA_KW), F32)
        wup_bd = jnp.concatenate([jnp.concatenate([gla_wup[l, 0], zero], axis=1),
                                  jnp.concatenate([zero, gla_wup[l, 1]], axis=1)], axis=0).astype(BF16)
        bup_row = gla_bup[l].reshape(1, 2 * GLA_KW)
        xr, gr, q, k, v, og, lg = _proj_call(h, m_l, norm_g[l, 1].reshape(1, d), w_in[l].astype(BF16),
                                             wup_bd, bup_row, n_ctx_blocks=n_ctx_blocks)

        scans = []
        for dr in range(2):
            wg = jnp.concatenate([_pair_blockdiag(rg_wa[l, dr]), _pair_blockdiag(rg_wi[l, dr])],
                                 axis=-1).astype(BF16)
            scans.append(_rg_call(xr, conv_w[l], conv_b[l].reshape(1, -1), rg_lam[l, dr].reshape(1, -1), wg,
                                  rg_ba[l, dr].reshape(1, -1), rg_bi[l, dr].reshape(1, -1),
                                  reverse=bool(dr), n_ctx_blocks=n_ctx_blocks))

        o_f, o_b = _gla_call(_to_scan_order(q, n_ctx), _to_scan_order(k, n_ctx), _to_scan_order(v, n_ctx),
                             _to_scan_order(lg, n_ctx), n_ctx_chunks=n_ctx_chunks)
        o = _from_scan_order(o_f + o_b, n_ctx)

        h = _out_call(h, m_l, gr, scans[0], scans[1], o, og, gla_norm_g[l].reshape(1, -1),
                      w_out[l].astype(BF16), n_ctx_blocks=n_ctx_blocks)
        h = _ffn_call(h, m_l, norm_g[l, 2].reshape(1, d), w1[1], w3[1], w2[1], fg,
                      j=2, n_ctx_blocks=n_ctx_blocks, latents_only=last, final=last)
    return h
```

```python
import functools

import jax
import jax.numpy as jnp
from jax import lax
from jax.experimental import pallas as pl
from jax.experimental.pallas import tpu as pltpu

F32 = jnp.float32
BF16 = jnp.bfloat16

LANES = 128
SUBLANES = 8
FFN_TOKENS = 512
TOKEN_BLOCK = 512
GLA_CHUNKS_PER_STEP = 4
VMEM_LIMIT = 56 * 1024 * 1024

N_MOD = 9
EPS = 1e-6
GRID_W = 64
RG_WIDTH = 512
RG_HEADS = 8
RG_C = 8.0
GLA_HEADS = 4
GLA_DK = 64
GLA_DV = 128
GLA_KW = GLA_HEADS * GLA_DK
GLA_VW = GLA_HEADS * GLA_DV
GLA_RANK = 16
GLA_GATE_NORM = 16.0
GLA_CHUNK = 64
GLA_PAIRS = GLA_KW // LANES
PAIR_VW = GLA_VW // GLA_PAIRS


def _params(sem):
    return pltpu.CompilerParams(dimension_semantics=sem, vmem_limit_bytes=VMEM_LIMIT)


def _fixed_spec(shape, index):
    return pl.BlockSpec(shape, lambda *_: index, pipeline_mode=pl.Buffered(1))


def _full_spec(arr):
    return _fixed_spec(arr.shape, (0,) * arr.ndim)


def _sigmoid(x):
    return 1.0 / (1.0 + jnp.exp(-x))


def _softplus(x):
    return jnp.maximum(x, 0.0) + jnp.log1p(jnp.exp(-jnp.abs(x)))


def _ada_norm(x, m, g, j):
    ms = jnp.mean(x * x, axis=-1, keepdims=True)
    y = x * lax.rsqrt(ms + EPS) * g
    return y * (1.0 + m[3 * j + 1:3 * j + 2]) + m[3 * j:3 * j + 1]


def _token_block(n, target):
    return min(n, target)


def _mod_kernel(c_ref, w_ref, b_ref, o_ref):
    c = c_ref[...]
    sc = c * _sigmoid(c)
    o_ref[...] = jnp.dot(sc, w_ref[...], preferred_element_type=F32,
                         precision=lax.Precision.HIGHEST) + b_ref[...]


def _mod_call(cvec, w_mod, b_mod):
    depth, d, n = w_mod.shape
    rows = cvec.shape[0]
    tn = n // 8
    return pl.pallas_call(
        _mod_kernel,
        grid=(depth, n // tn),
        in_specs=[
            pl.BlockSpec((rows, d), lambda l, i: (0, 0)),
            pl.BlockSpec((None, d, tn), lambda l, i: (l, 0, i)),
            pl.BlockSpec((None, 1, tn), lambda l, i: (l, 0, i)),
        ],
        out_specs=pl.BlockSpec((None, rows, tn), lambda l, i: (l, 0, i)),
        out_shape=jax.ShapeDtypeStruct((depth, rows, n), F32),
        compiler_params=_params(("parallel", "parallel")),
        name="mod",
    )(cvec, w_mod, b_mod.reshape(depth, 1, n))


def _mod_spec(mods, l, is_ctx):
    d = mods.shape[-1]
    if is_ctx:
        return pl.BlockSpec((None, None, N_MOD, d), lambda b, i: (l, 0, 0, 0))
    return pl.BlockSpec((None, None, N_MOD, d), lambda b, i: (l, b + 1, 0, 0))


def _tok_spec(tm, w):
    return pl.BlockSpec((None, tm, w), lambda b, i: (b, i, 0))


def _ffn_kernel(x_ref, m_ref, g_ref, w1_ref, w3_ref, w2_ref, fg_ref, o_ref, *, j, final, ff_chunks):
    x = x_ref[...]
    m = m_ref[...]
    u = _ada_norm(x, m, g_ref[...], j).astype(BF16)
    y = None
    for lo, hi in ff_chunks:
        h1 = jnp.dot(u, w1_ref[:, lo:hi], preferred_element_type=F32)
        h3 = jnp.dot(u, w3_ref[:, lo:hi], preferred_element_type=F32)
        act = (h1 * _sigmoid(h1) * h3).astype(BF16)
        part = jnp.dot(act, w2_ref[lo:hi, :], preferred_element_type=F32)
        y = part if y is None else y + part
    out = x + (0.5 * m[3 * j + 2:3 * j + 3]) * y
    if final:
        ms = jnp.mean(out * out, axis=-1, keepdims=True)
        out = out * lax.rsqrt(ms + EPS) * fg_ref[...]
    o_ref[...] = out


def _ffn_call(h, mods, norm_g4, w1, w3, w2, final_g, *, l, half, is_ctx, final):
    bsz, n, d = h.shape
    dff = w1.shape[-1]
    tm = _token_block(n, FFN_TOKENS)
    j = 2 * half
    step = 1024
    ff_chunks = tuple((lo, min(lo + step, dff)) for lo in range(0, dff, step))
    return pl.pallas_call(
        functools.partial(_ffn_kernel, j=j, final=final, ff_chunks=ff_chunks),
        grid=(bsz, n // tm),
        in_specs=[
            _tok_spec(tm, d),
            _mod_spec(mods, l, is_ctx),
            _fixed_spec((None, None, 1, d), (l, j, 0, 0)),
            _fixed_spec((None, None, d, dff), (l, half, 0, 0)),
            _fixed_spec((None, None, d, dff), (l, half, 0, 0)),
            _fixed_spec((None, None, dff, d), (l, half, 0, 0)),
            _full_spec(final_g),
        ],
        out_specs=_tok_spec(tm, d),
        out_shape=jax.ShapeDtypeStruct((bsz, n, d), F32),
        compiler_params=_params(("parallel", "parallel")),
        name="ffn",
    )(h, mods, norm_g4, w1, w3, w2, final_g)


def _store_interleaved(ref, val):
    nb, seg_len, _ = ref.shape
    w = val.shape[1]
    for bb in range(nb):
        for s in range(SUBLANES):
            r0 = (bb * SUBLANES + s) * seg_len
            ref[bb, :, s * w:(s + 1) * w] = val[r0:r0 + seg_len]


def _load_interleaved(ref):
    nb, _, sw = ref.shape
    w = sw // SUBLANES
    return jnp.concatenate([ref[bb, :, s * w:(s + 1) * w]
                            for bb in range(nb) for s in range(SUBLANES)], axis=0)


def _interleaved_spec(nb, seg_len, w, nblk):
    return pl.BlockSpec((nb, seg_len, SUBLANES * w), lambda b, i: (b * nblk + i, 0, 0))


def _proj_kernel(x_ref, m_ref, g_ref, w_ref, wup_ref, bup_ref,
                 xr_ref, gr_ref, q_ref, k_ref, v_ref, og_ref, lg_ref):
    u = _ada_norm(x_ref[...], m_ref[...], g_ref[...], 1).astype(BF16)
    p = jnp.dot(u, w_ref[...], preferred_element_type=F32)
    o = 0
    _store_interleaved(xr_ref, p[:, o:o + RG_WIDTH]); o += RG_WIDTH
    gr_ref[...] = p[:, o:o + RG_WIDTH]; o += RG_WIDTH
    q_ref[...] = p[:, o:o + GLA_KW] * (GLA_DK ** -0.5); o += GLA_KW
    k_ref[...] = p[:, o:o + GLA_KW]; o += GLA_KW
    v_ref[...] = p[:, o:o + GLA_VW].astype(BF16); o += GLA_VW
    og_ref[...] = p[:, o:o + GLA_VW]; o += GLA_VW
    lr = p[:, o:o + 2 * GLA_RANK].astype(BF16)
    z = jnp.dot(lr, wup_ref[...], preferred_element_type=F32) + bup_ref[...]
    lg_ref[...] = (-_softplus(-z)) / GLA_GATE_NORM


def _proj_call(h, mods, norm_g4, w_in, wup_bd, bup, *, l, is_ctx, seqs_per_block):
    bsz, n, d = h.shape
    tm = _token_block(n, TOKEN_BLOCK)
    nblk = n // tm
    seg_len = tm // (seqs_per_block * SUBLANES)
    outs = ((RG_WIDTH, F32), (GLA_KW, F32), (GLA_KW, F32), (GLA_VW, BF16), (GLA_VW, F32), (2 * GLA_KW, F32))
    xr_shape = (bsz * nblk * seqs_per_block, seg_len, SUBLANES * RG_WIDTH)
    return pl.pallas_call(
        _proj_kernel,
        grid=(bsz, nblk),
        in_specs=[
            _tok_spec(tm, d),
            _mod_spec(mods, l, is_ctx),
            _fixed_spec((None, None, 1, d), (l, 1, 0, 0)),
            _fixed_spec((None,) + w_in.shape[1:], (l, 0, 0)),
            _fixed_spec((None,) + wup_bd.shape[1:], (l, 0, 0)),
            _fixed_spec((None, 1, bup.shape[-1]), (l, 0, 0)),
        ],
        out_specs=[_interleaved_spec(seqs_per_block, seg_len, RG_WIDTH, nblk)]
                  + [_tok_spec(tm, w) for w, _ in outs],
        out_shape=[jax.ShapeDtypeStruct(xr_shape, F32)]
                  + [jax.ShapeDtypeStruct((bsz, n, w), dt) for w, dt in outs],
        compiler_params=_params(("parallel", "parallel")),
        name="proj",
    )(h, mods, norm_g4, w_in, wup_bd, bup)


def _rg_kernel(x_ref, xp_ref, xn_ref, c0_ref, cw_ref, cb_ref, lam_ref, wg_ref, ba_ref, bi_ref,
               o_ref, cout_ref, carry_ref, *, reverse, nblk):
    seg_len = x_ref.shape[0]
    j = pl.program_id(1)
    blk = (nblk - 1 - j) if reverse else j

    @pl.when(j == 0)
    def _():
        carry_ref[...] = c0_ref[...]

    x = x_ref[...]
    last = SUBLANES - 1
    prev2 = jnp.where(blk == 0, 0.0, xp_ref[last - 1, last:, :])
    prev1 = jnp.where(blk == 0, 0.0, xp_ref[last, last:, :])
    next1 = jnp.where(blk == nblk - 1, 0.0, xn_ref[0, 0:1, :])
    row8 = lax.broadcasted_iota(jnp.int32, (SUBLANES, 1), 0)

    def from_prev_segment(tile, halo):
        return jnp.where(row8 == 0, halo, pltpu.roll(tile, 1, axis=0))

    def from_next_segment(tile, halo):
        return jnp.where(row8 == last, halo, pltpu.roll(tile, last, axis=0))

    xm2 = from_prev_segment(x[seg_len - 2], prev2)
    xm1 = from_prev_segment(x[seg_len - 1], prev1)
    xp1 = from_next_segment(x[0], next1)
    xe = jnp.concatenate([xm2[None], xm1[None], x, xp1[None]], axis=0)
    cw = cw_ref[...]
    xc = xe[0:seg_len] * cw[0:1] + xe[1:seg_len + 1] * cw[1:2]
    xc = xc + xe[2:seg_len + 2] * cw[2:3]
    xc = xc + xe[3:seg_len + 3] * cw[3:4]
    xc = (xc + cb_ref[...]).reshape(seg_len * SUBLANES, RG_WIDTH)

    sp = _softplus(-lam_ref[...])
    order = range(seg_len - 1, -1, -1) if reverse else range(seg_len)
    for g in range(RG_WIDTH // LANES):
        sl = slice(g * LANES, (g + 1) * LANES)
        xg = xc[:, sl]
        pre = jnp.dot(xg.astype(BF16), wg_ref[g], preferred_element_type=F32)
        r = _sigmoid(pre[:, :LANES] + ba_ref[:, sl])
        i = _sigmoid(pre[:, LANES:] + bi_ref[:, sl])
        log_a = (-RG_C) * r * sp[:, sl]
        a = jnp.exp(log_a)
        bv = jnp.sqrt(1.0 - a * a) * (i * xg)
        hs, ps = [None] * seg_len, [None] * seg_len
        h_run = p_run = None
        for t in order:
            rows = slice(t * SUBLANES, (t + 1) * SUBLANES)
            if h_run is None:
                h_run, p_run = bv[rows], a[rows]
            else:
                h_run = a[rows] * h_run + bv[rows]
                p_run = a[rows] * p_run
            hs[t], ps[t] = h_run, p_run
        pc, hc = p_run, h_run
        s = 1
        while s < SUBLANES:
            shift = SUBLANES - s if reverse else s
            valid = (row8 < SUBLANES - s) if reverse else (row8 >= s)
            hc = jnp.where(valid, pc * pltpu.roll(hc, shift, axis=0) + hc, hc)
            pc = jnp.where(valid, pc * pltpu.roll(pc, shift, axis=0), pc)
            s *= 2
        carry = carry_ref[:, sl]
        seg_end = pc * carry + hc
        if reverse:
            seg_in = jnp.where(row8 == last, carry, pltpu.roll(seg_end, last, axis=0))
            carry_ref[:, sl] = seg_end[0:1]
        else:
            seg_in = jnp.where(row8 == 0, carry, pltpu.roll(seg_end, 1, axis=0))
            carry_ref[:, sl] = seg_end[last:]
        for t in range(seg_len):
            o_ref[t, :, sl] = hs[t] + ps[t] * seg_in

    @pl.when(j == nblk - 1)
    def _():
        cout_ref[...] = carry_ref[...]


def _rg_call(xr_il, carry0, conv_w, conv_b, lam, wg, ba, bi, *, l, d, reverse):
    bsz = carry0.shape[0]
    nb_total, seg_len, sw = xr_il.shape
    w = sw // SUBLANES
    nblk = nb_total // bsz
    tiles = seg_len // SUBLANES
    x4 = xr_il.reshape(nb_total, seg_len, SUBLANES, w)
    halo = xr_il.reshape(nb_total * tiles, SUBLANES, SUBLANES, w)

    def blk(b, j):
        return b * nblk + ((nblk - 1 - j) if reverse else j)

    vec = _fixed_spec((None, None, 1, w), (l, d, 0, 0))
    scan, carry = pl.pallas_call(
        functools.partial(_rg_kernel, reverse=reverse, nblk=nblk),
        grid=(bsz, nblk),
        in_specs=[
            pl.BlockSpec((None, seg_len, SUBLANES, w), lambda b, j: (blk(b, j), 0, 0, 0)),
            pl.BlockSpec((None, SUBLANES, SUBLANES, w),
                         lambda b, j: (jnp.maximum(blk(b, j) * tiles - 1, 0), 0, 0, 0)),
            pl.BlockSpec((None, SUBLANES, SUBLANES, w),
                         lambda b, j: (jnp.minimum((blk(b, j) + 1) * tiles, nb_total * tiles - 1), 0, 0, 0)),
            pl.BlockSpec((None, 1, w), lambda b, j: (b, 0, 0)),
            _fixed_spec((None,) + conv_w.shape[1:], (l, 0, 0)),
            _fixed_spec((None, 1, w), (l, 0, 0)),
            vec,
            _fixed_spec((None, None) + wg.shape[2:], (l, d, 0, 0, 0)),
            vec,
            vec,
        ],
        out_specs=[pl.BlockSpec((None, seg_len, SUBLANES, w), lambda b, j: (blk(b, j), 0, 0, 0)),
                   pl.BlockSpec((None, 1, w), lambda b, j: (b, 0, 0))],
        out_shape=[jax.ShapeDtypeStruct(x4.shape, F32), jax.ShapeDtypeStruct((bsz, 1, w), F32)],
        scratch_shapes=[pltpu.VMEM((1, w), F32)],
        compiler_params=_params(("arbitrary", "arbitrary")),
        name="rg_bwd" if reverse else "rg_fwd",
    )(x4, halo, halo, carry0, conv_w, conv_b, lam, wg, ba, bi)
    return scan.reshape(xr_il.shape), carry


def _level_ref(b, m, r):
    c, w = b.shape
    if m >= SUBLANES:
        b3 = b.reshape(c // m, m, w)
        return jnp.broadcast_to(b3[:, r:r + 1, :], (c // m, m, w)).reshape(c, w)
    bt = b.reshape(c // SUBLANES, SUBLANES, w)
    sub = lax.broadcasted_iota(jnp.int32, (1, SUBLANES, 1), 1)
    out = None
    for blk in range(SUBLANES // m):
        src = blk * m + r
        piece = jnp.broadcast_to(bt[:, src:src + 1, :], bt.shape)
        out = piece if out is None else jnp.where(sub >= blk * m, piece, out)
    return out.reshape(c, w)


def _dot_nt(a, b):
    return lax.dot_general(a, b, (((1,), (1,)), ((), ())), preferred_element_type=F32)


def _dot_tn(a, b):
    return lax.dot_general(a, b, (((0,), (0,)), ((), ())), preferred_element_type=F32)


def _gla_consts(reverse):
    c = GLA_CHUNK
    row = lax.broadcasted_iota(jnp.int32, (c, 1), 0)
    r2 = lax.broadcasted_iota(jnp.int32, (c, c), 0)
    c2 = lax.broadcasted_iota(jnp.int32, (c, c), 1)
    tri = jnp.where((c2 >= r2) if reverse else (c2 <= r2), 1.0, 0.0).astype(BF16)
    klane = lax.broadcasted_iota(jnp.int32, (1, LANES), 1)
    vlane = lax.broadcasted_iota(jnp.int32, (1, PAIR_VW), 1)
    key_first = klane < GLA_DK
    val_first = vlane < GLA_DV
    srow = lax.broadcasted_iota(jnp.int32, (PAIR_VW, 1), 0)
    state_mask = (srow < GLA_DV) == key_first
    t = lax.broadcasted_iota(jnp.int32, (c, LANES), 0)
    s = lax.broadcasted_iota(jnp.int32, (c, LANES), 1) % c
    levels = []
    m = c
    while m >= 2:
        half = m // 2
        late = (row % m) >= half
        q_rows = ~late if reverse else late
        same = (t // m) == (s // m)
        t_late = (t % m) >= half
        s_late = (s % m) >= half
        mask = same & (~t_late) & s_late if reverse else same & t_late & (~s_late)
        levels.append((m, q_rows, mask))
        m = half
    return dict(tri=tri, key_first=key_first, val_first=val_first, state_mask=state_mask,
                levels=levels, diag=(t == s))


def _pair_blockdiag_rows(x, first_mask):
    zero = jnp.zeros_like(x)
    return jnp.concatenate([jnp.where(first_mask, x, zero), jnp.where(first_mask, zero, x)], axis=0)


def _gla_chunks(streams):
    c = GLA_CHUNK
    pair_k = [slice(p * LANES, (p + 1) * LANES) for p in range(GLA_PAIRS)]
    pair_v = [slice(p * PAIR_VW, (p + 1) * PAIR_VW) for p in range(GLA_PAIRS)]
    for s in streams:
        g, tri = s["g"], s["consts"]["tri"]
        g1 = g.astype(BF16)
        g2 = (g - g1.astype(F32)).astype(BF16)
        g3 = (g - g1.astype(F32) - g2.astype(F32)).astype(BF16)
        s["b"] = (jnp.dot(tri, g1, preferred_element_type=F32)
                  + jnp.dot(tri, g2, preferred_element_type=F32)
                  + jnp.dot(tri, g3, preferred_element_type=F32))
    for s in streams:
        kf = s["consts"]["key_first"]
        qb, kb = s["q"].astype(BF16), s["k"].astype(BF16)
        s["att"] = [jnp.where(s["consts"]["diag"],
                              _dot_nt(qb[:, kl], _pair_blockdiag_rows(kb[:, kl], kf)), 0.0) for kl in pair_k]
    for lvl in range(len(streams[0]["consts"]["levels"])):
        for s in streams:
            m, q_rows, mask = s["consts"]["levels"][lvl]
            half = m // 2
            ref = _level_ref(s["b"], m, half if s["reverse"] else half - 1)
            x = (jnp.exp(-jnp.abs(s["b"] - ref)) * jnp.where(q_rows, s["q"], s["k"])).astype(BF16)
            kf = s["consts"]["key_first"]
            s["att"] = [jnp.where(mask, _dot_nt(x[:, kl], _pair_blockdiag_rows(x[:, kl], kf)), att)
                        for kl, att in zip(pair_k, s["att"])]
    for s in streams:
        b = s["b"]
        b_last = b[0:1] if s["reverse"] else b[c - 1:c]
        s["q_in"] = (s["q"] * jnp.exp(b)).astype(BF16)
        s["k_hat"] = (s["k"] * jnp.exp(b_last - b)).astype(BF16)
        s["decay"] = jnp.exp(b_last)
    for s in streams:
        vf = s["consts"]["val_first"]
        for p, (kl, vl) in enumerate(zip(pair_k, pair_v)):
            vp = s["v"][:, vl]
            st = s["st_ref"][p]
            o = jnp.dot(s["att"][p].astype(BF16), _pair_blockdiag_rows(vp, vf), preferred_element_type=F32)
            o = o + _dot_nt(s["q_in"][:, kl], st.astype(BF16))
            s["o_ref"][:, vl] = o
            upd = _dot_tn(vp, s["k_hat"][:, kl])
            s["st_ref"][p] = st * s["decay"][:, kl] + jnp.where(s["consts"]["state_mask"], upd, 0.0)


def _gla_kernel(qf_ref, kf_ref, vf_ref, gf_ref, qb_ref, kb_ref, vb_ref, gb_ref, s0_ref,
                of_ref, ob_ref, sout_ref, st_ref, *, chunks):
    @pl.when(pl.program_id(0) == 0)
    def _():
        st_ref[...] = s0_ref[...]

    bsz = qf_ref.shape[0]
    dirs = ((0, qf_ref, kf_ref, vf_ref, gf_ref, of_ref), (1, qb_ref, kb_ref, vb_ref, gb_ref, ob_ref))
    consts = [_gla_consts(reverse=False), _gla_consts(reverse=True)]
    for ci in range(chunks):
        streams = []
        for d, q_ref, k_ref, v_ref, g_ref, o_ref in dirs:
            rows = pl.ds(((chunks - 1 - ci) if d else ci) * GLA_CHUNK, GLA_CHUNK)
            for bi in range(bsz):
                streams.append(dict(q=q_ref[bi, rows, :], k=k_ref[bi, rows, :], v=v_ref[bi, rows, :],
                                    g=g_ref[bi, rows, :], st_ref=st_ref.at[d, bi], o_ref=o_ref.at[bi, rows],
                                    consts=consts[d], reverse=bool(d)))
        _gla_chunks(streams)

    @pl.when(pl.program_id(0) == pl.num_programs(0) - 1)
    def _():
        sout_ref[...] = st_ref[...]


def _gla_call(q, k, v, lg, state0, *, col_major):
    bsz, n, kw = q.shape
    vw = v.shape[2]
    seq_rows = n // GRID_W if col_major else n
    chunks = min(GLA_CHUNKS_PER_STEP, seq_rows // GLA_CHUNK)
    br = chunks * GLA_CHUNK
    rblocks = seq_rows // br
    nstep = n // br
    if col_major:
        def view(a):
            return a.reshape(bsz, seq_rows, GRID_W * a.shape[2])

        def pos(i, wblocks=1, off=0):
            return (0, i % rblocks, (i // rblocks) * wblocks + off)
    else:
        def view(a):
            return a

        def pos(i, wblocks=1, off=0):
            return (0, i, off)

    def fwd(i):
        return pos(i)

    def bwd(i):
        return pos(nstep - 1 - i)

    o_f, o_b, s_out = pl.pallas_call(
        functools.partial(_gla_kernel, chunks=chunks),
        grid=(nstep,),
        in_specs=[
            pl.BlockSpec((bsz, br, kw), fwd),
            pl.BlockSpec((bsz, br, kw), fwd),
            pl.BlockSpec((bsz, br, vw), fwd),
            pl.BlockSpec((bsz, br, kw), lambda i: pos(i, 2, 0)),
            pl.BlockSpec((bsz, br, kw), bwd),
            pl.BlockSpec((bsz, br, kw), bwd),
            pl.BlockSpec((bsz, br, vw), bwd),
            pl.BlockSpec((bsz, br, kw), lambda i: pos(nstep - 1 - i, 2, 1)),
            _full_spec(state0),
        ],
        out_specs=[pl.BlockSpec((bsz, br, vw), fwd), pl.BlockSpec((bsz, br, vw), bwd),
                   pl.BlockSpec(state0.shape, lambda i: (0,) * state0.ndim)],
        out_shape=[jax.ShapeDtypeStruct(view(v).shape, F32)] * 2
                  + [jax.ShapeDtypeStruct(state0.shape, F32)],
        scratch_shapes=[pltpu.VMEM(state0.shape, F32)],
        compiler_params=_params(("arbitrary",)),
        name="gla_lat" if col_major else "gla_ctx",
    )(view(q), view(k), view(v), view(lg), view(q), view(k), view(v), view(lg), state0)
    return o_f.reshape(bsz, n, vw), o_b.reshape(bsz, n, vw), s_out


def _out_kernel(h_ref, m_ref, gr_ref, hf_ref, hb_ref, of_ref, ob_ref, og_ref, gn_ref, w_ref, out_ref):
    gr = gr_ref[...]
    gelu = 0.5 * gr * (1.0 + jnp.tanh(0.7978845608028654 * (gr + 0.044715 * (gr * gr * gr))))
    rg = gelu * (_load_interleaved(hf_ref) + _load_interleaved(hb_ref))
    o = of_ref[...] + ob_ref[...]
    og = og_ref[...]
    gn = gn_ref[...]
    parts = []
    for hd in range(GLA_HEADS):
        sl = slice(hd * GLA_DV, (hd + 1) * GLA_DV)
        oh = o[:, sl]
        ms = jnp.mean(oh * oh, axis=-1, keepdims=True)
        parts.append(oh * lax.rsqrt(ms + EPS) * gn[:, sl])
    gla = jnp.concatenate(parts, axis=-1) * (og * _sigmoid(og))
    mix = jnp.concatenate([rg, gla], axis=-1).astype(BF16)
    y = jnp.dot(mix, w_ref[...], preferred_element_type=F32)
    out_ref[...] = h_ref[...] + m_ref[5:6] * y


def _out_call(h, mods, gr, hf, hb, o_f, o_b, og, gnorm, w_out, *, l, is_ctx, seqs_per_block):
    bsz, n, d = h.shape
    tm = _token_block(n, TOKEN_BLOCK)
    nblk = n // tm
    seg_len = tm // (seqs_per_block * SUBLANES)
    scan_spec = _interleaved_spec(seqs_per_block, seg_len, RG_WIDTH, nblk)
    return pl.pallas_call(
        _out_kernel,
        grid=(bsz, nblk),
        in_specs=[
            _tok_spec(tm, d),
            _mod_spec(mods, l, is_ctx),
            _tok_spec(tm, RG_WIDTH), scan_spec, scan_spec,
            _tok_spec(tm, GLA_VW), _tok_spec(tm, GLA_VW), _tok_spec(tm, GLA_VW),
            _fixed_spec((None, 1, GLA_VW), (l, 0, 0)),
            _fixed_spec((None,) + w_out.shape[1:], (l, 0, 0)),
        ],
        out_specs=_tok_spec(tm, d),
        out_shape=jax.ShapeDtypeStruct((bsz, n, d), F32),
        compiler_params=_params(("parallel", "parallel")),
        name="mix_out",
    )(h, mods, gr, hf, hb, o_f, o_b, og, gnorm, w_out)


def _pair_blockdiag(w):
    hd = w.shape[-1]
    per = LANES // hd
    lead = w.shape[:-3]
    wg = w.reshape(lead + (RG_HEADS // per, per, hd, hd))
    eye = jnp.eye(per, dtype=w.dtype)
    return jnp.einsum('...gpij,pq->...gpiqj', wg, eye).reshape(lead + (RG_HEADS // per, LANES, LANES))


def kernel(x, c, ctx, c_ctx, w_mod, b_mod, norm_g, ffn_w1, ffn_w3, ffn_w2, w_in, conv_w, conv_b,
           rg_lam, rg_wa, rg_ba, rg_wi, rg_bi, gla_wup, gla_bup, gla_norm_g, w_out, final_g):
    bsz, t, d = x.shape
    n_ctx = ctx.shape[1]
    depth = w_mod.shape[0]
    assert t % TOKEN_BLOCK == 0 and t % FFN_TOKENS == 0 and t % (GRID_W * GLA_CHUNK) == 0
    assert n_ctx % (SUBLANES * SUBLANES) == 0 and n_ctx % GLA_CHUNK == 0 and bsz * n_ctx <= TOKEN_BLOCK

    rows = -(-(bsz + 1) // SUBLANES) * SUBLANES
    cvec = jnp.zeros((rows, d), F32).at[0].set(c_ctx).at[1:bsz + 1].set(c)
    mods = _mod_call(cvec, w_mod, b_mod).reshape(depth, rows, N_MOD, d)

    w1, w3, w2 = ffn_w1.astype(BF16), ffn_w3.astype(BF16), ffn_w2.astype(BF16)
    w_in_b, w_out_b = w_in.astype(BF16), w_out.astype(BF16)
    norm_g4 = norm_g.reshape(depth, 3, 1, d)
    zero = jnp.zeros((depth, GLA_RANK, GLA_KW), F32)
    wup_bd = jnp.concatenate([jnp.concatenate([gla_wup[:, 0], zero], axis=2),
                              jnp.concatenate([zero, gla_wup[:, 1]], axis=2)], axis=1).astype(BF16)
    bup = gla_bup.reshape(depth, 1, 2 * GLA_KW)
    wg = jnp.concatenate([_pair_blockdiag(rg_wa), _pair_blockdiag(rg_wi)], axis=-1).astype(BF16)
    conv_b3 = conv_b.reshape(depth, 1, RG_WIDTH)
    lam4 = rg_lam.reshape(depth, 2, 1, RG_WIDTH)
    ba4 = rg_ba.reshape(depth, 2, 1, RG_WIDTH)
    bi4 = rg_bi.reshape(depth, 2, 1, RG_WIDTH)
    gnorm = gla_norm_g.reshape(depth, 1, GLA_VW)
    fg = final_g.reshape(1, d)

    h_c = ctx.reshape(1, bsz * n_ctx, d)
    h_l = x
    for l in range(depth):
        last = l == depth - 1
        ffn = functools.partial(_ffn_call, mods=mods, norm_g4=norm_g4, w1=w1, w3=w3, w2=w2, final_g=fg, l=l)
        h_l = ffn(h_l, half=0, is_ctx=False, final=False)
        h_c = ffn(h_c, half=0, is_ctx=True, final=False)

        proj = functools.partial(_proj_call, mods=mods, norm_g4=norm_g4, w_in=w_in_b, wup_bd=wup_bd,
                                 bup=bup, l=l)
        xr_l, gr_l, q_l, k_l, v_l, og_l, lg_l = proj(h_l, is_ctx=False, seqs_per_block=1)
        xr_c, gr_c, q_c, k_c, v_c, og_c, lg_c = proj(h_c, is_ctx=True, seqs_per_block=bsz)

        scans_l, scans_c = [], []
        for dr in range(2):
            rg = functools.partial(_rg_call, conv_w=conv_w, conv_b=conv_b3, lam=lam4, wg=wg, ba=ba4,
                                   bi=bi4, l=l, d=dr, reverse=bool(dr))
            s_c, carry = rg(xr_c, jnp.zeros((bsz, 1, RG_WIDTH), F32))
            s_l, _ = rg(xr_l, carry)
            scans_c.append(s_c)
            scans_l.append(s_l)

        per_seq = lambda a: a.reshape(bsz, n_ctx, a.shape[-1])
        state0 = jnp.zeros((2, bsz, GLA_PAIRS, PAIR_VW, LANES), F32)
        oc_f, oc_b, state = _gla_call(per_seq(q_c), per_seq(k_c), per_seq(v_c), per_seq(lg_c), state0,
                                      col_major=False)
        ol_f, ol_b, _ = _gla_call(q_l, k_l, v_l, lg_l, state, col_major=True)

        out = functools.partial(_out_call, mods=mods, gnorm=gnorm, w_out=w_out_b, l=l)
        h_l = out(h_l, gr=gr_l, hf=scans_l[0], hb=scans_l[1], o_f=ol_f, o_b=ol_b, og=og_l,
                  is_ctx=False, seqs_per_block=1)
        h_l = ffn(h_l, half=1, is_ctx=False, final=last)
        if not last:
            flat = lambda a: a.reshape(1, bsz * n_ctx, a.shape[-1])
            h_c = out(h_c, gr=gr_c, hf=scans_c[0], hb=scans_c[1], o_f=flat(oc_f), o_b=flat(oc_b), og=og_c,
                      is_ctx=True, seqs_per_block=bsz)
            h_c = ffn(h_c, half=1, is_ctx=True, final=False)
    return h_l
```

```python
import functools

import jax
import jax.numpy as jnp
from jax import lax
from jax.experimental import pallas as pl
from jax.experimental.pallas import tpu as pltpu

F32 = jnp.float32
BF16 = jnp.bfloat16

LANES = 128
SUBLANES = 8
FFN_TOKENS = 512
TOKEN_BLOCK = 512
GLA_CHUNKS_PER_STEP = 4
VMEM_LIMIT = 56 * 1024 * 1024

N_MOD = 9
EPS = 1e-6
GRID_W = 64
RG_WIDTH = 512
RG_HEADS = 8
RG_C = 8.0
GLA_HEADS = 4
GLA_DK = 64
GLA_DV = 128
GLA_KW = GLA_HEADS * GLA_DK
GLA_VW = GLA_HEADS * GLA_DV
GLA_RANK = 16
GLA_GATE_NORM = 16.0
GLA_CHUNK = 64
GLA_PAIRS = GLA_KW // LANES
PAIR_VW = GLA_VW // GLA_PAIRS


def _params(sem):
    return pltpu.CompilerParams(dimension_semantics=sem, vmem_limit_bytes=VMEM_LIMIT)


def _fixed_spec(shape, index):
    return pl.BlockSpec(shape, lambda *_: index, pipeline_mode=pl.Buffered(1))


def _full_spec(arr):
    return _fixed_spec(arr.shape, (0,) * arr.ndim)


def _sigmoid(x):
    return 1.0 / (1.0 + jnp.exp(-x))


def _softplus(x):
    return jnp.maximum(x, 0.0) + jnp.log1p(jnp.exp(-jnp.abs(x)))


def _ada_norm(x, m, g, j):
    ms = jnp.mean(x * x, axis=-1, keepdims=True)
    y = x * lax.rsqrt(ms + EPS) * g
    return y * (1.0 + m[3 * j + 1:3 * j + 2]) + m[3 * j:3 * j + 1]


def _token_block(n, target):
    return min(n, target)


def _mod_kernel(c_ref, w_ref, b_ref, o_ref):
    c = c_ref[...]
    sc = c * _sigmoid(c)
    o_ref[...] = jnp.dot(sc, w_ref[...], preferred_element_type=F32,
                         precision=lax.Precision.HIGHEST) + b_ref[...]


def _mod_call(cvec, w_mod, b_mod):
    depth, d, n = w_mod.shape
    rows = cvec.shape[0]
    tn = n // 8
    return pl.pallas_call(
        _mod_kernel,
        grid=(depth, n // tn),
        in_specs=[
            pl.BlockSpec((rows, d), lambda l, i: (0, 0)),
            pl.BlockSpec((None, d, tn), lambda l, i: (l, 0, i)),
            pl.BlockSpec((None, 1, tn), lambda l, i: (l, 0, i)),
        ],
        out_specs=pl.BlockSpec((None, rows, tn), lambda l, i: (l, 0, i)),
        out_shape=jax.ShapeDtypeStruct((depth, rows, n), F32),
        compiler_params=_params(("parallel", "parallel")),
        name="mod",
    )(cvec, w_mod, b_mod.reshape(depth, 1, n))


def _mod_spec(mods, l, is_ctx):
    d = mods.shape[-1]
    if is_ctx:
        return pl.BlockSpec((None, None, N_MOD, d), lambda b, i: (l, 0, 0, 0))
    return pl.BlockSpec((None, None, N_MOD, d), lambda b, i: (l, b + 1, 0, 0))


def _tok_spec(tm, w):
    return pl.BlockSpec((None, tm, w), lambda b, i: (b, i, 0))


def _ffn_kernel(x_ref, m_ref, g_ref, w1_ref, w3_ref, w2_ref, fg_ref, o_ref, *, j, final, ff_chunks):
    x = x_ref[...]
    m = m_ref[...]
    u = _ada_norm(x, m, g_ref[...], j).astype(BF16)
    y = None
    for lo, hi in ff_chunks:
        h1 = jnp.dot(u, w1_ref[:, lo:hi], preferred_element_type=F32)
        h3 = jnp.dot(u, w3_ref[:, lo:hi], preferred_element_type=F32)
        act = (h1 * _sigmoid(h1) * h3).astype(BF16)
        part = jnp.dot(act, w2_ref[lo:hi, :], preferred_element_type=F32)
        y = part if y is None else y + part
    out = x + (0.5 * m[3 * j + 2:3 * j + 3]) * y
    if final:
        ms = jnp.mean(out * out, axis=-1, keepdims=True)
        out = out * lax.rsqrt(ms + EPS) * fg_ref[...]
    o_ref[...] = out


def _ffn_call(h, mods, norm_g4, w1, w3, w2, final_g, *, l, half, is_ctx, final):
    bsz, n, d = h.shape
    dff = w1.shape[-1]
    tm = _token_block(n, FFN_TOKENS)
    j = 2 * half
    step = 1024
    ff_chunks = tuple((lo, min(lo + step, dff)) for lo in range(0, dff, step))
    return pl.pallas_call(
        functools.partial(_ffn_kernel, j=j, final=final, ff_chunks=ff_chunks),
        grid=(bsz, n // tm),
        in_specs=[
            _tok_spec(tm, d),
            _mod_spec(mods, l, is_ctx),
            _fixed_spec((None, None, 1, d), (l, j, 0, 0)),
            _fixed_spec((None, None, d, dff), (l, half, 0, 0)),
            _fixed_spec((None, None, d, dff), (l, half, 0, 0)),
            _fixed_spec((None, None, dff, d), (l, half, 0, 0)),
            _full_spec(final_g),
        ],
        out_specs=_tok_spec(tm, d),
        out_shape=jax.ShapeDtypeStruct((bsz, n, d), F32),
        compiler_params=_params(("parallel", "parallel")),
        name="ffn",
    )(h, mods, norm_g4, w1, w3, w2, final_g)


def _store_interleaved(ref, val):
    nb, seg_len, _ = ref.shape
    w = val.shape[1]
    for bb in range(nb):
        for s in range(SUBLANES):
            r0 = (bb * SUBLANES + s) * seg_len
            ref[bb, :, s * w:(s + 1) * w] = val[r0:r0 + seg_len]


def _load_interleaved(ref):
    nb, _, sw = ref.shape
    w = sw // SUBLANES
    return jnp.concatenate([ref[bb, :, s * w:(s + 1) * w]
                            for bb in range(nb) for s in range(SUBLANES)], axis=0)


def _interleaved_spec(nb, seg_len, w, nblk):
    return pl.BlockSpec((nb, seg_len, SUBLANES * w), lambda b, i: (b * nblk + i, 0, 0))


def _proj_kernel(x_ref, m_ref, g_ref, w_ref, wup_ref, bup_ref,
                 xr_ref, gr_ref, q_ref, k_ref, v_ref, og_ref, lg_ref):
    u = _ada_norm(x_ref[...], m_ref[...], g_ref[...], 1).astype(BF16)
    p = jnp.dot(u, w_ref[...], preferred_element_type=F32)
    o = 0
    if len(xr_ref.shape) == 3:
        _store_interleaved(xr_ref, p[:, o:o + RG_WIDTH])
    else:
        xr_ref[...] = p[:, o:o + RG_WIDTH]
    o += RG_WIDTH
    gr_ref[...] = p[:, o:o + RG_WIDTH]; o += RG_WIDTH
    q_ref[...] = p[:, o:o + GLA_KW] * (GLA_DK ** -0.5); o += GLA_KW
    k_ref[...] = p[:, o:o + GLA_KW]; o += GLA_KW
    v_ref[...] = p[:, o:o + GLA_VW].astype(BF16); o += GLA_VW
    og_ref[...] = p[:, o:o + GLA_VW]; o += GLA_VW
    lr = p[:, o:o + 2 * GLA_RANK].astype(BF16)
    z = jnp.dot(lr, wup_ref[...], preferred_element_type=F32) + bup_ref[...]
    lg_ref[...] = (-_softplus(-z)) / GLA_GATE_NORM


def _proj_call(h, mods, norm_g4, w_in, wup_bd, bup, *, l, is_ctx, seqs_per_block):
    bsz, n, d = h.shape
    tm = _token_block(n, TOKEN_BLOCK)
    nblk = n // tm
    outs = ((RG_WIDTH, F32), (GLA_KW, F32), (GLA_KW, F32), (GLA_VW, BF16), (GLA_VW, F32), (2 * GLA_KW, F32))
    if seqs_per_block:
        seg_len = tm // (seqs_per_block * SUBLANES)
        xr_shape = (bsz * nblk * seqs_per_block, seg_len, SUBLANES * RG_WIDTH)
        xr_spec = _interleaved_spec(seqs_per_block, seg_len, RG_WIDTH, nblk)
    else:
        xr_shape = (bsz, n, RG_WIDTH)
        xr_spec = _tok_spec(tm, RG_WIDTH)
    return pl.pallas_call(
        _proj_kernel,
        grid=(bsz, nblk),
        in_specs=[
            _tok_spec(tm, d),
            _mod_spec(mods, l, is_ctx),
            _fixed_spec((None, None, 1, d), (l, 1, 0, 0)),
            _fixed_spec((None,) + w_in.shape[1:], (l, 0, 0)),
            _fixed_spec((None,) + wup_bd.shape[1:], (l, 0, 0)),
            _fixed_spec((None, 1, bup.shape[-1]), (l, 0, 0)),
        ],
        out_specs=[xr_spec] + [_tok_spec(tm, w) for w, _ in outs],
        out_shape=[jax.ShapeDtypeStruct(xr_shape, F32)]
                  + [jax.ShapeDtypeStruct((bsz, n, w), dt) for w, dt in outs],
        compiler_params=_params(("parallel", "parallel")),
        name="proj",
    )(h, mods, norm_g4, w_in, wup_bd, bup)


def _rg_kernel(x_ref, xp_ref, xn_ref, c0_ref, cw_ref, cb_ref, lam_ref, wg_ref, ba_ref, bi_ref,
               o_ref, cout_ref, carry_ref, *, reverse, nblk):
    seg_len = x_ref.shape[0]
    j = pl.program_id(1)
    blk = (nblk - 1 - j) if reverse else j

    @pl.when(j == 0)
    def _():
        carry_ref[...] = c0_ref[...]

    x = x_ref[...]
    last = SUBLANES - 1
    prev2 = jnp.where(blk == 0, 0.0, xp_ref[last - 1, last:, :])
    prev1 = jnp.where(blk == 0, 0.0, xp_ref[last, last:, :])
    next1 = jnp.where(blk == nblk - 1, 0.0, xn_ref[0, 0:1, :])
    row8 = lax.broadcasted_iota(jnp.int32, (SUBLANES, 1), 0)

    def from_prev_segment(tile, halo):
        return jnp.where(row8 == 0, halo, pltpu.roll(tile, 1, axis=0))

    def from_next_segment(tile, halo):
        return jnp.where(row8 == last, halo, pltpu.roll(tile, last, axis=0))

    xm2 = from_prev_segment(x[seg_len - 2], prev2)
    xm1 = from_prev_segment(x[seg_len - 1], prev1)
    xp1 = from_next_segment(x[0], next1)
    xe = jnp.concatenate([xm2[None], xm1[None], x, xp1[None]], axis=0)
    cw = cw_ref[...]
    xc = xe[0:seg_len] * cw[0:1] + xe[1:seg_len + 1] * cw[1:2]
    xc = xc + xe[2:seg_len + 2] * cw[2:3]
    xc = xc + xe[3:seg_len + 3] * cw[3:4]
    xc = (xc + cb_ref[...]).reshape(seg_len * SUBLANES, RG_WIDTH)

    sp = _softplus(-lam_ref[...])
    order = range(seg_len - 1, -1, -1) if reverse else range(seg_len)
    for g in range(RG_WIDTH // LANES):
        sl = slice(g * LANES, (g + 1) * LANES)
        xg = xc[:, sl]
        pre = jnp.dot(xg.astype(BF16), wg_ref[g], preferred_element_type=F32)
        r = _sigmoid(pre[:, :LANES] + ba_ref[:, sl])
        i = _sigmoid(pre[:, LANES:] + bi_ref[:, sl])
        log_a = (-RG_C) * r * sp[:, sl]
        a = jnp.exp(log_a)
        bv = jnp.sqrt(1.0 - a * a) * (i * xg)
        hs, ps = [None] * seg_len, [None] * seg_len
        h_run = p_run = None
        for t in order:
            rows = slice(t * SUBLANES, (t + 1) * SUBLANES)
            if h_run is None:
                h_run, p_run = bv[rows], a[rows]
            else:
                h_run = a[rows] * h_run + bv[rows]
                p_run = a[rows] * p_run
            hs[t], ps[t] = h_run, p_run
        pc, hc = p_run, h_run
        s = 1
        while s < SUBLANES:
            shift = SUBLANES - s if reverse else s
            valid = (row8 < SUBLANES - s) if reverse else (row8 >= s)
            hc = jnp.where(valid, pc * pltpu.roll(hc, shift, axis=0) + hc, hc)
            pc = jnp.where(valid, pc * pltpu.roll(pc, shift, axis=0), pc)
            s *= 2
        carry = carry_ref[:, sl]
        seg_end = pc * carry + hc
        if reverse:
            seg_in = jnp.where(row8 == last, carry, pltpu.roll(seg_end, last, axis=0))
            carry_ref[:, sl] = seg_end[0:1]
        else:
            seg_in = jnp.where(row8 == 0, carry, pltpu.roll(seg_end, 1, axis=0))
            carry_ref[:, sl] = seg_end[last:]
        for t in range(seg_len):
            o_ref[t, :, sl] = hs[t] + ps[t] * seg_in

    @pl.when(j == nblk - 1)
    def _():
        cout_ref[...] = carry_ref[...]


def _rg_call(x4, carry0, conv_w, conv_b, lam, wg, ba, bi, *, l, d, reverse):
    bsz, seg_len, nseg, w = x4.shape
    nblk = nseg // SUBLANES
    tiles = seg_len // SUBLANES

    def blk(j):
        return (nblk - 1 - j) if reverse else j

    vec = _fixed_spec((None, None, 1, w), (l, d, 0, 0))
    return pl.pallas_call(
        functools.partial(_rg_kernel, reverse=reverse, nblk=nblk),
        grid=(bsz, nblk),
        in_specs=[
            pl.BlockSpec((None, seg_len, SUBLANES, w), lambda b, j: (b, 0, blk(j), 0)),
            pl.BlockSpec((None, SUBLANES, SUBLANES, w),
                         lambda b, j: (b, tiles - 1, jnp.maximum(blk(j) - 1, 0), 0)),
            pl.BlockSpec((None, SUBLANES, SUBLANES, w),
                         lambda b, j: (b, 0, jnp.minimum(blk(j) + 1, nblk - 1), 0)),
            pl.BlockSpec((None, 1, w), lambda b, j: (b, 0, 0)),
            _fixed_spec((None,) + conv_w.shape[1:], (l, 0, 0)),
            _fixed_spec((None, 1, w), (l, 0, 0)),
            vec,
            _fixed_spec((None, None) + wg.shape[2:], (l, d, 0, 0, 0)),
            vec,
            vec,
        ],
        out_specs=[pl.BlockSpec((None, seg_len, SUBLANES, w), lambda b, j: (b, 0, blk(j), 0)),
                   pl.BlockSpec((None, 1, w), lambda b, j: (b, 0, 0))],
        out_shape=[jax.ShapeDtypeStruct(x4.shape, F32), jax.ShapeDtypeStruct((bsz, 1, w), F32)],
        scratch_shapes=[pltpu.VMEM((1, w), F32)],
        compiler_params=_params(("arbitrary", "arbitrary")),
        name="rg_bwd" if reverse else "rg_fwd",
    )(x4, x4, x4, carry0, conv_w, conv_b, lam, wg, ba, bi)


def _level_ref(b, m, r):
    c, w = b.shape
    if m >= SUBLANES:
        b3 = b.reshape(c // m, m, w)
        return jnp.broadcast_to(b3[:, r:r + 1, :], (c // m, m, w)).reshape(c, w)
    bt = b.reshape(c // SUBLANES, SUBLANES, w)
    sub = lax.broadcasted_iota(jnp.int32, (1, SUBLANES, 1), 1)
    out = None
    for blk in range(SUBLANES // m):
        src = blk * m + r
        piece = jnp.broadcast_to(bt[:, src:src + 1, :], bt.shape)
        out = piece if out is None else jnp.where(sub >= blk * m, piece, out)
    return out.reshape(c, w)


def _dot_nt(a, b):
    return lax.dot_general(a, b, (((1,), (1,)), ((), ())), preferred_element_type=F32)


def _dot_tn(a, b):
    return lax.dot_general(a, b, (((0,), (0,)), ((), ())), preferred_element_type=F32)


def _gla_consts(reverse):
    c = GLA_CHUNK
    row = lax.broadcasted_iota(jnp.int32, (c, 1), 0)
    r2 = lax.broadcasted_iota(jnp.int32, (c, c), 0)
    c2 = lax.broadcasted_iota(jnp.int32, (c, c), 1)
    tri = jnp.where((c2 >= r2) if reverse else (c2 <= r2), 1.0, 0.0).astype(BF16)
    klane = lax.broadcasted_iota(jnp.int32, (1, LANES), 1)
    vlane = lax.broadcasted_iota(jnp.int32, (1, PAIR_VW), 1)
    key_first = klane < GLA_DK
    val_first = vlane < GLA_DV
    srow = lax.broadcasted_iota(jnp.int32, (PAIR_VW, 1), 0)
    state_mask = (srow < GLA_DV) == key_first
    t = lax.broadcasted_iota(jnp.int32, (c, LANES), 0)
    s = lax.broadcasted_iota(jnp.int32, (c, LANES), 1) % c
    levels = []
    m = c
    while m >= 2:
        half = m // 2
        late = (row % m) >= half
        q_rows = ~late if reverse else late
        same = (t // m) == (s // m)
        t_late = (t % m) >= half
        s_late = (s % m) >= half
        mask = same & (~t_late) & s_late if reverse else same & t_late & (~s_late)
        levels.append((m, q_rows, mask))
        m = half
    return dict(tri=tri, key_first=key_first, val_first=val_first, state_mask=state_mask,
                levels=levels, diag=(t == s))


def _pair_blockdiag_rows(x, first_mask):
    zero = jnp.zeros_like(x)
    return jnp.concatenate([jnp.where(first_mask, x, zero), jnp.where(first_mask, zero, x)], axis=0)


def _gla_chunks(streams):
    c = GLA_CHUNK
    pair_k = [slice(p * LANES, (p + 1) * LANES) for p in range(GLA_PAIRS)]
    pair_v = [slice(p * PAIR_VW, (p + 1) * PAIR_VW) for p in range(GLA_PAIRS)]
    for s in streams:
        g, tri = s["g"], s["consts"]["tri"]
        g1 = g.astype(BF16)
        g2 = (g - g1.astype(F32)).astype(BF16)
        g3 = (g - g1.astype(F32) - g2.astype(F32)).astype(BF16)
        s["b"] = (jnp.dot(tri, g1, preferred_element_type=F32)
                  + jnp.dot(tri, g2, preferred_element_type=F32)
                  + jnp.dot(tri, g3, preferred_element_type=F32))
    for s in streams:
        kf = s["consts"]["key_first"]
        qb, kb = s["q"].astype(BF16), s["k"].astype(BF16)
        s["att"] = [jnp.where(s["consts"]["diag"],
                              _dot_nt(qb[:, kl], _pair_blockdiag_rows(kb[:, kl], kf)), 0.0) for kl in pair_k]
    for lvl in range(len(streams[0]["consts"]["levels"])):
        for s in streams:
            m, q_rows, mask = s["consts"]["levels"][lvl]
            half = m // 2
            ref = _level_ref(s["b"], m, half if s["reverse"] else half - 1)
            x = (jnp.exp(-jnp.abs(s["b"] - ref)) * jnp.where(q_rows, s["q"], s["k"])).astype(BF16)
            kf = s["consts"]["key_first"]
            s["att"] = [jnp.where(mask, _dot_nt(x[:, kl], _pair_blockdiag_rows(x[:, kl], kf)), att)
                        for kl, att in zip(pair_k, s["att"])]
    for s in streams:
        b = s["b"]
        b_last = b[0:1] if s["reverse"] else b[c - 1:c]
        s["q_in"] = (s["q"] * jnp.exp(b)).astype(BF16)
        s["k_hat"] = (s["k"] * jnp.exp(b_last - b)).astype(BF16)
        s["decay"] = jnp.exp(b_last)
    for s in streams:
        vf = s["consts"]["val_first"]
        for p, (kl, vl) in enumerate(zip(pair_k, pair_v)):
            vp = s["v"][:, vl]
            st = s["st_ref"][p]
            o = jnp.dot(s["att"][p].astype(BF16), _pair_blockdiag_rows(vp, vf), preferred_element_type=F32)
            o = o + _dot_nt(s["q_in"][:, kl], st.astype(BF16))
            s["o_ref"][:, vl] = o
            upd = _dot_tn(vp, s["k_hat"][:, kl])
            s["st_ref"][p] = st * s["decay"][:, kl] + jnp.where(s["consts"]["state_mask"], upd, 0.0)


def _gla_kernel(qf_ref, kf_ref, vf_ref, gf_ref, qb_ref, kb_ref, vb_ref, gb_ref, s0_ref,
                of_ref, ob_ref, sout_ref, st_ref, *, chunks):
    @pl.when(pl.program_id(0) == 0)
    def _():
        st_ref[...] = s0_ref[...]

    bsz = qf_ref.shape[0]
    dirs = ((0, qf_ref, kf_ref, vf_ref, gf_ref, of_ref), (1, qb_ref, kb_ref, vb_ref, gb_ref, ob_ref))
    consts = [_gla_consts(reverse=False), _gla_consts(reverse=True)]
    for ci in range(chunks):
        streams = []
        for d, q_ref, k_ref, v_ref, g_ref, o_ref in dirs:
            rows = pl.ds(((chunks - 1 - ci) if d else ci) * GLA_CHUNK, GLA_CHUNK)
            for bi in range(bsz):
                streams.append(dict(q=q_ref[bi, rows, :], k=k_ref[bi, rows, :], v=v_ref[bi, rows, :],
                                    g=g_ref[bi, rows, :], st_ref=st_ref.at[d, bi], o_ref=o_ref.at[bi, rows],
                                    consts=consts[d], reverse=bool(d)))
        _gla_chunks(streams)

    @pl.when(pl.program_id(0) == pl.num_programs(0) - 1)
    def _():
        sout_ref[...] = st_ref[...]


def _gla_call(q, k, v, lg, state0):
    bsz, n, kw = q.shape
    vw = v.shape[2]
    chunks = min(GLA_CHUNKS_PER_STEP, n // GLA_CHUNK)
    br = chunks * GLA_CHUNK
    nstep = n // br

    def fwd(i):
        return (0, i, 0)

    def bwd(i):
        return (0, nstep - 1 - i, 0)

    return pl.pallas_call(
        functools.partial(_gla_kernel, chunks=chunks),
        grid=(nstep,),
        in_specs=[
            pl.BlockSpec((bsz, br, kw), fwd),
            pl.BlockSpec((bsz, br, kw), fwd),
            pl.BlockSpec((bsz, br, vw), fwd),
            pl.BlockSpec((bsz, br, kw), fwd),
            pl.BlockSpec((bsz, br, kw), bwd),
            pl.BlockSpec((bsz, br, kw), bwd),
            pl.BlockSpec((bsz, br, vw), bwd),
            pl.BlockSpec((bsz, br, kw), lambda i: (0, nstep - 1 - i, 1)),
            _full_spec(state0),
        ],
        out_specs=[pl.BlockSpec((bsz, br, vw), fwd), pl.BlockSpec((bsz, br, vw), bwd),
                   pl.BlockSpec(state0.shape, lambda i: (0,) * state0.ndim)],
        out_shape=[jax.ShapeDtypeStruct((bsz, n, vw), F32)] * 2
                  + [jax.ShapeDtypeStruct(state0.shape, F32)],
        scratch_shapes=[pltpu.VMEM(state0.shape, F32)],
        compiler_params=_params(("arbitrary",)),
        name="gla",
    )(q, k, v, lg, q, k, v, lg, state0)


def _out_kernel(h_ref, m_ref, gr_ref, hf_ref, hb_ref, of_ref, ob_ref, og_ref, gn_ref, w_ref, out_ref):
    gr = gr_ref[...]
    gelu = 0.5 * gr * (1.0 + jnp.tanh(0.7978845608028654 * (gr + 0.044715 * (gr * gr * gr))))
    if len(hf_ref.shape) == 3:
        rg = gelu * (_load_interleaved(hf_ref) + _load_interleaved(hb_ref))
    else:
        rg = gelu * (hf_ref[...] + hb_ref[...])
    o = of_ref[...] + ob_ref[...]
    og = og_ref[...]
    gn = gn_ref[...]
    parts = []
    for hd in range(GLA_HEADS):
        sl = slice(hd * GLA_DV, (hd + 1) * GLA_DV)
        oh = o[:, sl]
        ms = jnp.mean(oh * oh, axis=-1, keepdims=True)
        parts.append(oh * lax.rsqrt(ms + EPS) * gn[:, sl])
    gla = jnp.concatenate(parts, axis=-1) * (og * _sigmoid(og))
    mix = jnp.concatenate([rg, gla], axis=-1).astype(BF16)
    y = jnp.dot(mix, w_ref[...], preferred_element_type=F32)
    out_ref[...] = h_ref[...] + m_ref[5:6] * y


def _out_call(h, mods, gr, hf, hb, o_f, o_b, og, gnorm, w_out, *, l, is_ctx, seqs_per_block):
    bsz, n, d = h.shape
    tm = _token_block(n, TOKEN_BLOCK)
    nblk = n // tm
    if seqs_per_block:
        scan_spec = _interleaved_spec(seqs_per_block, tm // (seqs_per_block * SUBLANES), RG_WIDTH, nblk)
    else:
        scan_spec = _tok_spec(tm, RG_WIDTH)
    return pl.pallas_call(
        _out_kernel,
        grid=(bsz, nblk),
        in_specs=[
            _tok_spec(tm, d),
            _mod_spec(mods, l, is_ctx),
            _tok_spec(tm, RG_WIDTH), scan_spec, scan_spec,
            _tok_spec(tm, GLA_VW), _tok_spec(tm, GLA_VW), _tok_spec(tm, GLA_VW),
            _fixed_spec((None, 1, GLA_VW), (l, 0, 0)),
            _fixed_spec((None,) + w_out.shape[1:], (l, 0, 0)),
        ],
        out_specs=_tok_spec(tm, d),
        out_shape=jax.ShapeDtypeStruct((bsz, n, d), F32),
        compiler_params=_params(("parallel", "parallel")),
        name="mix_out",
    )(h, mods, gr, hf, hb, o_f, o_b, og, gnorm, w_out)


def _pair_blockdiag(w):
    hd = w.shape[-1]
    per = LANES // hd
    lead = w.shape[:-3]
    wg = w.reshape(lead + (RG_HEADS // per, per, hd, hd))
    eye = jnp.eye(per, dtype=w.dtype)
    return jnp.einsum('...gpij,pq->...gpiqj', wg, eye).reshape(lead + (RG_HEADS // per, LANES, LANES))


def kernel(x, c, ctx, c_ctx, w_mod, b_mod, norm_g, ffn_w1, ffn_w3, ffn_w2, w_in, conv_w, conv_b,
           rg_lam, rg_wa, rg_ba, rg_wi, rg_bi, gla_wup, gla_bup, gla_norm_g, w_out, final_g):
    bsz, t, d = x.shape
    n_ctx = ctx.shape[1]
    depth = w_mod.shape[0]
    assert t % TOKEN_BLOCK == 0 and t % FFN_TOKENS == 0 and t % (GRID_W * SUBLANES) == 0
    assert n_ctx % (SUBLANES * SUBLANES) == 0 and n_ctx % GLA_CHUNK == 0 and bsz * n_ctx <= TOKEN_BLOCK

    rows = -(-(bsz + 1) // SUBLANES) * SUBLANES
    cvec = jnp.zeros((rows, d), F32).at[0].set(c_ctx).at[1:bsz + 1].set(c)
    mods = _mod_call(cvec, w_mod, b_mod).reshape(depth, rows, N_MOD, d)

    w1, w3, w2 = ffn_w1.astype(BF16), ffn_w3.astype(BF16), ffn_w2.astype(BF16)
    w_in_b, w_out_b = w_in.astype(BF16), w_out.astype(BF16)
    norm_g4 = norm_g.reshape(depth, 3, 1, d)
    zero = jnp.zeros((depth, GLA_RANK, GLA_KW), F32)
    wup_bd = jnp.concatenate([jnp.concatenate([gla_wup[:, 0], zero], axis=2),
                              jnp.concatenate([zero, gla_wup[:, 1]], axis=2)], axis=1).astype(BF16)
    bup = gla_bup.reshape(depth, 1, 2 * GLA_KW)
    wg = jnp.concatenate([_pair_blockdiag(rg_wa), _pair_blockdiag(rg_wi)], axis=-1).astype(BF16)
    conv_b3 = conv_b.reshape(depth, 1, RG_WIDTH)
    lam4 = rg_lam.reshape(depth, 2, 1, RG_WIDTH)
    ba4 = rg_ba.reshape(depth, 2, 1, RG_WIDTH)
    bi4 = rg_bi.reshape(depth, 2, 1, RG_WIDTH)
    gnorm = gla_norm_g.reshape(depth, 1, GLA_VW)
    fg = final_g.reshape(1, d)

    h_c = ctx.reshape(1, bsz * n_ctx, d)
    grid_rows = t // GRID_W
    h_l = x.reshape(bsz, grid_rows, GRID_W, d).swapaxes(1, 2).reshape(bsz, t, d)
    ctx_seg = n_ctx // SUBLANES
    for l in range(depth):
        last = l == depth - 1
        ffn = functools.partial(_ffn_call, mods=mods, norm_g4=norm_g4, w1=w1, w3=w3, w2=w2, final_g=fg, l=l)
        h_l = ffn(h_l, half=0, is_ctx=False, final=False)
        h_c = ffn(h_c, half=0, is_ctx=True, final=False)

        proj = functools.partial(_proj_call, mods=mods, norm_g4=norm_g4, w_in=w_in_b, wup_bd=wup_bd,
                                 bup=bup, l=l)
        xr_l, gr_l, q_l, k_l, v_l, og_l, lg_l = proj(h_l, is_ctx=False, seqs_per_block=0)
        xr_c, gr_c, q_c, k_c, v_c, og_c, lg_c = proj(h_c, is_ctx=True, seqs_per_block=bsz)

        scans_l, scans_c = [], []
        for dr in range(2):
            rg = functools.partial(_rg_call, conv_w=conv_w, conv_b=conv_b3, lam=lam4, wg=wg, ba=ba4,
                                   bi=bi4, l=l, d=dr, reverse=bool(dr))
            s_c, carry = rg(xr_c.reshape(bsz, ctx_seg, SUBLANES, RG_WIDTH), jnp.zeros((bsz, 1, RG_WIDTH), F32))
            s_l, _ = rg(xr_l.reshape(bsz, GRID_W, grid_rows, RG_WIDTH), carry)
            scans_c.append(s_c.reshape(xr_c.shape))
            scans_l.append(s_l.reshape(xr_l.shape))

        per_seq = lambda a: a.reshape(bsz, n_ctx, a.shape[-1])
        state0 = jnp.zeros((2, bsz, GLA_PAIRS, PAIR_VW, LANES), F32)
        oc_f, oc_b, state = _gla_call(per_seq(q_c), per_seq(k_c), per_seq(v_c), per_seq(lg_c), state0)
        ol_f, ol_b, _ = _gla_call(q_l, k_l, v_l, lg_l, state)

        out = functools.partial(_out_call, mods=mods, gnorm=gnorm, w_out=w_out_b, l=l)
        h_l = out(h_l, gr=gr_l, hf=scans_l[0], hb=scans_l[1], o_f=ol_f, o_b=ol_b, og=og_l,
                  is_ctx=False, seqs_per_block=0)
        h_l = ffn(h_l, half=1, is_ctx=False, final=last)
        if not last:
            flat = lambda a: a.reshape(1, bsz * n_ctx, a.shape[-1])
            h_c = out(h_c, gr=gr_c, hf=scans_c[0], hb=scans_c[1], o_f=flat(oc_f), o_b=flat(oc_b), og=og_c,
                      is_ctx=True, seqs_per_block=bsz)
            h_c = ffn(h_c, half=1, is_ctx=True, final=False)
    return h_l.reshape(bsz, GRID_W, grid_rows, d).swapaxes(1, 2).reshape(bsz, t, d)
```

```python
import functools
import math

import jax
import jax.numpy as jnp
from jax import lax
from jax.experimental import pallas as pl
from jax.experimental.pallas import tpu as pltpu

F32 = jnp.float32
BF16 = jnp.bfloat16

LANES = 128
SUBLANES = 8
FFN_TOKENS = 512
TOKEN_BLOCK = 512
SUB_ROWS = 256
FF_STEP = 1024
GLA_CHUNKS_PER_STEP = 4
VMEM_LIMIT = 56 * 1024 * 1024

N_MOD = 9
EPS = 1e-6
GRID_W = 64
RG_WIDTH = 512
RG_HEADS = 8
RG_C = 8.0
GLA_HEADS = 4
GLA_DK = 64
GLA_DV = 128
GLA_KW = GLA_HEADS * GLA_DK
GLA_VW = GLA_HEADS * GLA_DV
GLA_RANK = 16
GLA_GATE_NORM = 16.0
GLA_CHUNK = 64
GLA_PAIRS = GLA_KW // LANES
PAIR_VW = GLA_VW // GLA_PAIRS
LOG2E = math.log2(math.e)


def _params(sem):
    return pltpu.CompilerParams(dimension_semantics=sem, vmem_limit_bytes=VMEM_LIMIT)


def _fixed_spec(shape, index):
    return pl.BlockSpec(shape, lambda *_: index, pipeline_mode=pl.Buffered(1))


def _full_spec(arr):
    return _fixed_spec(arr.shape, (0,) * arr.ndim)


def _sigmoid(x):
    return 1.0 / (1.0 + jnp.exp(-x))


def _softplus(x):
    return jnp.maximum(x, 0.0) + jnp.log(1.0 + jnp.exp(-jnp.abs(x)))


def _ada_norm(x, m, g, j):
    ms = jnp.mean(x * x, axis=-1, keepdims=True)
    y = x * lax.rsqrt(ms + EPS) * g
    return y * (1.0 + m[3 * j + 1:3 * j + 2]) + m[3 * j:3 * j + 1]


def _token_block(n, target):
    return min(n, target)


def _row_blocks(n_rows):
    step = min(SUB_ROWS, n_rows)
    return [slice(r, r + step) for r in range(0, n_rows, step)]


def _staged(blocks, prologue, matmul, epilogue):
    operand = prologue(blocks[0])
    pending = None
    for i, rows in enumerate(blocks):
        acc = matmul(operand)
        if i + 1 < len(blocks):
            operand = prologue(blocks[i + 1])
        if pending is not None:
            epilogue(*pending)
        pending = (rows, acc)
    epilogue(*pending)


def _mod_kernel(c_ref, w_ref, b_ref, o_ref):
    c = c_ref[...]
    sc = (c * _sigmoid(c)).astype(BF16)
    o_ref[...] = jnp.dot(sc, w_ref[...].astype(BF16), preferred_element_type=F32) + b_ref[...]


def _mod_call(cvec, w_mod, b_mod):
    depth, d, n = w_mod.shape
    rows = cvec.shape[0]
    tn = n // 8
    return pl.pallas_call(
        _mod_kernel,
        grid=(depth, n // tn),
        in_specs=[
            pl.BlockSpec((rows, d), lambda l, i: (0, 0)),
            pl.BlockSpec((None, d, tn), lambda l, i: (l, 0, i)),
            pl.BlockSpec((None, 1, tn), lambda l, i: (l, 0, i)),
        ],
        out_specs=pl.BlockSpec((None, rows, tn), lambda l, i: (l, 0, i)),
        out_shape=jax.ShapeDtypeStruct((depth, rows, n), F32),
        compiler_params=_params(("parallel", "parallel")),
        name="mod",
    )(cvec, w_mod, b_mod.reshape(depth, 1, n))


def _mod_spec(mods, l, is_ctx):
    d = mods.shape[-1]
    if is_ctx:
        return pl.BlockSpec((None, None, N_MOD, d), lambda b, i: (l, 0, 0, 0))
    return pl.BlockSpec((None, None, N_MOD, d), lambda b, i: (l, b + 1, 0, 0))


def _tok_spec(tm, w):
    return pl.BlockSpec((None, tm, w), lambda b, i: (b, i, 0))


def _store_interleaved(ref, val):
    nb, seg_len, _ = ref.shape
    w = val.shape[1]
    for bb in range(nb):
        for s in range(SUBLANES):
            r0 = (bb * SUBLANES + s) * seg_len
            ref[bb, :, s * w:(s + 1) * w] = val[r0:r0 + seg_len]


def _load_interleaved(ref):
    nb, _, sw = ref.shape
    w = sw // SUBLANES
    return jnp.concatenate([ref[bb, :, s * w:(s + 1) * w]
                            for bb in range(nb) for s in range(SUBLANES)], axis=0)


def _interleaved_spec(nb, seg_len, w, nblk):
    return pl.BlockSpec((nb, seg_len, SUBLANES * w), lambda b, i: (b * nblk + i, 0, 0))


def _mixer_rows(rows, gr_ref, hf_ref, hb_ref, of_ref, ob_ref, og_ref, gn):
    gr = gr_ref[rows, :]
    gelu = 0.5 * gr * (1.0 + jnp.tanh(0.7978845608028654 * (gr + 0.044715 * (gr * gr * gr))))
    if len(hf_ref.shape) == 3:
        rg = gelu * (_load_interleaved(hf_ref) + _load_interleaved(hb_ref))
    else:
        rg = gelu * (hf_ref[rows, :] + hb_ref[rows, :])
    o = of_ref[rows, :] + ob_ref[rows, :]
    og = og_ref[rows, :]
    parts = []
    for hd in range(GLA_HEADS):
        sl = slice(hd * GLA_DV, (hd + 1) * GLA_DV)
        oh = o[:, sl]
        ms = jnp.mean(oh * oh, axis=-1, keepdims=True)
        parts.append(oh * lax.rsqrt(ms + EPS) * gn[:, sl])
    gla = jnp.concatenate(parts, axis=-1) * (og * _sigmoid(og))
    return jnp.concatenate([rg, gla], axis=-1).astype(BF16)


def _ffn_kernel(*refs, j, final, ff_chunks, with_mixer):
    if with_mixer:
        (x_ref, m_ref, g_ref, gr_ref, hf_ref, hb_ref, of_ref, ob_ref, og_ref, gn_ref, wo_ref,
         w1_ref, w3_ref, w2_ref, fg_ref, o_ref) = refs
        whole = len(hf_ref.shape) == 3
    else:
        x_ref, m_ref, g_ref, w1_ref, w3_ref, w2_ref, fg_ref, o_ref = refs
        whole = False
    m = m_ref[...]
    n_rows = x_ref.shape[0]

    def prologue(rows):
        x = x_ref[rows, :]
        if with_mixer:
            mix = _mixer_rows(rows, gr_ref, hf_ref, hb_ref, of_ref, ob_ref, og_ref, gn_ref[...])
            x = x + m[5:6] * jnp.dot(mix, wo_ref[...], preferred_element_type=F32)
        return x, _ada_norm(x, m, g_ref[...], j).astype(BF16)

    def matmul(operand):
        x, u = operand
        y = None
        for lo, hi in ff_chunks:
            h1 = jnp.dot(u, w1_ref[:, lo:hi], preferred_element_type=F32)
            h3 = jnp.dot(u, w3_ref[:, lo:hi], preferred_element_type=F32)
            act = (h1 * _sigmoid(h1) * h3).astype(BF16)
            part = jnp.dot(act, w2_ref[lo:hi, :], preferred_element_type=F32)
            y = part if y is None else y + part
        return x, y

    def epilogue(rows, acc):
        x, y = acc
        out = x + (0.5 * m[3 * j + 2:3 * j + 3]) * y
        if final:
            ms = jnp.mean(out * out, axis=-1, keepdims=True)
            out = out * lax.rsqrt(ms + EPS) * fg_ref[...]
        o_ref[rows, :] = out

    _staged([slice(0, n_rows)] if whole else _row_blocks(n_rows), prologue, matmul, epilogue)


def _ffn_call(h, mods, norm_g4, w1, w3, w2, final_g, *, l, half, is_ctx, final, mixer=None):
    bsz, n, d = h.shape
    dff = w1.shape[-1]
    tm = _token_block(n, FFN_TOKENS)
    nblk = n // tm
    j = 2 * half
    ff_chunks = tuple((lo, min(lo + FF_STEP, dff)) for lo in range(0, dff, FF_STEP))
    operands = [h, mods, norm_g4]
    in_specs = [_tok_spec(tm, d), _mod_spec(mods, l, is_ctx), _fixed_spec((None, None, 1, d), (l, j, 0, 0))]
    if mixer is not None:
        spb = mixer["seqs_per_block"]
        if spb:
            scan_spec = _interleaved_spec(spb, tm // (spb * SUBLANES), RG_WIDTH, nblk)
        else:
            scan_spec = _tok_spec(tm, RG_WIDTH)
        operands += [mixer[name] for name in ("gr", "hf", "hb", "o_f", "o_b", "og", "gnorm", "w_out")]
        in_specs += [_tok_spec(tm, RG_WIDTH), scan_spec, scan_spec,
                     _tok_spec(tm, GLA_VW), _tok_spec(tm, GLA_VW), _tok_spec(tm, GLA_VW),
                     _fixed_spec((None, 1, GLA_VW), (l, 0, 0)),
                     _fixed_spec((None,) + mixer["w_out"].shape[1:], (l, 0, 0))]
    operands += [w1, w3, w2, final_g]
    in_specs += [_fixed_spec((None, None, d, dff), (l, half, 0, 0)),
                 _fixed_spec((None, None, d, dff), (l, half, 0, 0)),
                 _fixed_spec((None, None, dff, d), (l, half, 0, 0)),
                 _full_spec(final_g)]
    return pl.pallas_call(
        functools.partial(_ffn_kernel, j=j, final=final, ff_chunks=ff_chunks, with_mixer=mixer is not None),
        grid=(bsz, nblk),
        in_specs=in_specs,
        out_specs=_tok_spec(tm, d),
        out_shape=jax.ShapeDtypeStruct((bsz, n, d), F32),
        compiler_params=_params(("parallel", "parallel")),
        name="mix_ffn" if mixer is not None else "ffn",
    )(*operands)


def _proj_kernel(x_ref, m_ref, g_ref, w_ref, wup_ref, bup_ref,
                 xr_ref, gr_ref, q_ref, k_ref, v_ref, og_ref, lg_ref):
    interleaved = len(xr_ref.shape) == 3
    m = m_ref[...]

    def prologue(rows):
        return _ada_norm(x_ref[rows, :], m, g_ref[...], 1).astype(BF16)

    def matmul(u):
        return jnp.dot(u, w_ref[...], preferred_element_type=F32)

    def epilogue(rows, p):
        o = 0
        if interleaved:
            _store_interleaved(xr_ref, p[:, o:o + RG_WIDTH])
        else:
            xr_ref[rows, :] = p[:, o:o + RG_WIDTH]
        o += RG_WIDTH
        gr_ref[rows, :] = p[:, o:o + RG_WIDTH]; o += RG_WIDTH
        q_ref[rows, :] = p[:, o:o + GLA_KW] * (GLA_DK ** -0.5); o += GLA_KW
        k_ref[rows, :] = p[:, o:o + GLA_KW]; o += GLA_KW
        v_ref[rows, :] = p[:, o:o + GLA_VW].astype(BF16); o += GLA_VW
        og_ref[rows, :] = p[:, o:o + GLA_VW]; o += GLA_VW
        lr = p[:, o:o + 2 * GLA_RANK].astype(BF16)
        z = jnp.dot(lr, wup_ref[...], preferred_element_type=F32) + bup_ref[...]
        lg_ref[rows, :] = _softplus(-z) * (-LOG2E / GLA_GATE_NORM)

    n_rows = x_ref.shape[0]
    _staged([slice(0, n_rows)] if interleaved else _row_blocks(n_rows), prologue, matmul, epilogue)


def _proj_call(h, mods, norm_g4, w_in, wup_bd, bup, *, l, is_ctx, seqs_per_block):
    bsz, n, d = h.shape
    tm = _token_block(n, TOKEN_BLOCK)
    nblk = n // tm
    outs = ((RG_WIDTH, F32), (GLA_KW, F32), (GLA_KW, F32), (GLA_VW, BF16), (GLA_VW, F32), (2 * GLA_KW, F32))
    if seqs_per_block:
        seg_len = tm // (seqs_per_block * SUBLANES)
        xr_shape = (bsz * nblk * seqs_per_block, seg_len, SUBLANES * RG_WIDTH)
        xr_spec = _interleaved_spec(seqs_per_block, seg_len, RG_WIDTH, nblk)
    else:
        xr_shape = (bsz, n, RG_WIDTH)
        xr_spec = _tok_spec(tm, RG_WIDTH)
    return pl.pallas_call(
        _proj_kernel,
        grid=(bsz, nblk),
        in_specs=[
            _tok_spec(tm, d),
            _mod_spec(mods, l, is_ctx),
            _fixed_spec((None, None, 1, d), (l, 1, 0, 0)),
            _fixed_spec((None,) + w_in.shape[1:], (l, 0, 0)),
            _fixed_spec((None,) + wup_bd.shape[1:], (l, 0, 0)),
            _fixed_spec((None, 1, bup.shape[-1]), (l, 0, 0)),
        ],
        out_specs=[xr_spec] + [_tok_spec(tm, w) for w, _ in outs],
        out_shape=[jax.ShapeDtypeStruct(xr_shape, F32)]
                  + [jax.ShapeDtypeStruct((bsz, n, w), dt) for w, dt in outs],
        compiler_params=_params(("parallel", "parallel")),
        name="proj",
    )(h, mods, norm_g4, w_in, wup_bd, bup)


def _rg_kernel(x_ref, xp_ref, xn_ref, c0_ref, cw_ref, cb_ref, lam_ref, wg_ref, ba_ref, bi_ref,
               o_ref, cout_ref, carry_ref, *, reverse, nblk):
    seg_len = x_ref.shape[0]
    j = pl.program_id(1)
    blk = (nblk - 1 - j) if reverse else j

    @pl.when(j == 0)
    def _():
        carry_ref[...] = c0_ref[...]

    x = x_ref[...]
    last = SUBLANES - 1
    prev2 = jnp.where(blk == 0, 0.0, xp_ref[last - 1, last:, :])
    prev1 = jnp.where(blk == 0, 0.0, xp_ref[last, last:, :])
    next1 = jnp.where(blk == nblk - 1, 0.0, xn_ref[0, 0:1, :])
    row8 = lax.broadcasted_iota(jnp.int32, (SUBLANES, 1), 0)

    def from_prev_segment(tile, halo):
        return jnp.where(row8 == 0, halo, pltpu.roll(tile, 1, axis=0))

    def from_next_segment(tile, halo):
        return jnp.where(row8 == last, halo, pltpu.roll(tile, last, axis=0))

    xm2 = from_prev_segment(x[seg_len - 2], prev2)
    xm1 = from_prev_segment(x[seg_len - 1], prev1)
    xp1 = from_next_segment(x[0], next1)
    xe = jnp.concatenate([xm2[None], xm1[None], x, xp1[None]], axis=0)
    cw = cw_ref[...]
    xc = xe[0:seg_len] * cw[0:1] + xe[1:seg_len + 1] * cw[1:2]
    xc = xc + xe[2:seg_len + 2] * cw[2:3]
    xc = xc + xe[3:seg_len + 3] * cw[3:4]
    xc = (xc + cb_ref[...]).reshape(seg_len * SUBLANES, RG_WIDTH)

    sp = _softplus(-lam_ref[...])
    order = range(seg_len - 1, -1, -1) if reverse else range(seg_len)
    for g in range(RG_WIDTH // LANES):
        sl = slice(g * LANES, (g + 1) * LANES)
        xg = xc[:, sl]
        pre = jnp.dot(xg.astype(BF16), wg_ref[g], preferred_element_type=F32)
        r = _sigmoid(pre[:, :LANES] + ba_ref[:, sl])
        i = _sigmoid(pre[:, LANES:] + bi_ref[:, sl])
        log_a = (-RG_C) * r * sp[:, sl]
        a = jnp.exp(log_a)
        bv = jnp.sqrt(1.0 - a * a) * (i * xg)
        hs, ps = [None] * seg_len, [None] * seg_len
        h_run = p_run = None
        for t in order:
            rows = slice(t * SUBLANES, (t + 1) * SUBLANES)
            if h_run is None:
                h_run, p_run = bv[rows], a[rows]
            else:
                h_run = a[rows] * h_run + bv[rows]
                p_run = a[rows] * p_run
            hs[t], ps[t] = h_run, p_run
        pc, hc = p_run, h_run
        s = 1
        while s < SUBLANES:
            shift = SUBLANES - s if reverse else s
            valid = (row8 < SUBLANES - s) if reverse else (row8 >= s)
            hc = jnp.where(valid, pc * pltpu.roll(hc, shift, axis=0) + hc, hc)
            pc = jnp.where(valid, pc * pltpu.roll(pc, shift, axis=0), pc)
            s *= 2
        carry = carry_ref[:, sl]
        seg_end = pc * carry + hc
        if reverse:
            seg_in = jnp.where(row8 == last, carry, pltpu.roll(seg_end, last, axis=0))
            carry_ref[:, sl] = seg_end[0:1]
        else:
            seg_in = jnp.where(row8 == 0, carry, pltpu.roll(seg_end, 1, axis=0))
            carry_ref[:, sl] = seg_end[last:]
        for t in range(seg_len):
            o_ref[t, :, sl] = hs[t] + ps[t] * seg_in

    @pl.when(j == nblk - 1)
    def _():
        cout_ref[...] = carry_ref[...]


def _rg_call(x4, carry0, conv_w, conv_b, lam, wg, ba, bi, *, l, d, reverse):
    bsz, seg_len, nseg, w = x4.shape
    nblk = nseg // SUBLANES
    tiles = seg_len // SUBLANES

    def blk(j):
        return (nblk - 1 - j) if reverse else j

    vec = _fixed_spec((None, None, 1, w), (l, d, 0, 0))
    return pl.pallas_call(
        functools.partial(_rg_kernel, reverse=reverse, nblk=nblk),
        grid=(bsz, nblk),
        in_specs=[
            pl.BlockSpec((None, seg_len, SUBLANES, w), lambda b, j: (b, 0, blk(j), 0)),
            pl.BlockSpec((None, SUBLANES, SUBLANES, w),
                         lambda b, j: (b, tiles - 1, jnp.maximum(blk(j) - 1, 0), 0)),
            pl.BlockSpec((None, SUBLANES, SUBLANES, w),
                         lambda b, j: (b, 0, jnp.minimum(blk(j) + 1, nblk - 1), 0)),
            pl.BlockSpec((None, 1, w), lambda b, j: (b, 0, 0)),
            _fixed_spec((None,) + conv_w.shape[1:], (l, 0, 0)),
            _fixed_spec((None, 1, w), (l, 0, 0)),
            vec,
            _fixed_spec((None, None) + wg.shape[2:], (l, d, 0, 0, 0)),
            vec,
            vec,
        ],
        out_specs=[pl.BlockSpec((None, seg_len, SUBLANES, w), lambda b, j: (b, 0, blk(j), 0)),
                   pl.BlockSpec((None, 1, w), lambda b, j: (b, 0, 0))],
        out_shape=[jax.ShapeDtypeStruct(x4.shape, F32), jax.ShapeDtypeStruct((bsz, 1, w), F32)],
        scratch_shapes=[pltpu.VMEM((1, w), F32)],
        compiler_params=_params(("arbitrary", "arbitrary")),
        name="rg_bwd" if reverse else "rg_fwd",
    )(x4, x4, x4, carry0, conv_w, conv_b, lam, wg, ba, bi)


def _level_ref(b, m, r):
    c, w = b.shape
    if m >= SUBLANES:
        b3 = b.reshape(c // m, m, w)
        return jnp.broadcast_to(b3[:, r:r + 1, :], (c // m, m, w)).reshape(c, w)
    bt = b.reshape(c // SUBLANES, SUBLANES, w)
    sub = lax.broadcasted_iota(jnp.int32, (1, SUBLANES, 1), 1)
    out = None
    for blk in range(SUBLANES // m):
        src = blk * m + r
        piece = jnp.broadcast_to(bt[:, src:src + 1, :], bt.shape)
        out = piece if out is None else jnp.where(sub >= blk * m, piece, out)
    return out.reshape(c, w)


def _halves_select(m, q_rows, q, k, reverse):
    half = m // 2
    if half % SUBLANES:
        return jnp.where(q_rows, q, k)
    first, second = (q, k) if reverse else (k, q)
    pieces = []
    for lo in range(0, q.shape[0], m):
        pieces += [first[lo:lo + half], second[lo + half:lo + m]]
    return jnp.concatenate(pieces, axis=0)


def _dot_nt(a, b):
    return lax.dot_general(a, b, (((1,), (1,)), ((), ())), preferred_element_type=F32)


def _dot_tn(a, b):
    return lax.dot_general(a, b, (((0,), (0,)), ((), ())), preferred_element_type=F32)


def _gla_consts(reverse):
    c = GLA_CHUNK
    row = lax.broadcasted_iota(jnp.int32, (c, 1), 0)
    r2 = lax.broadcasted_iota(jnp.int32, (c, c), 0)
    c2 = lax.broadcasted_iota(jnp.int32, (c, c), 1)
    tri = jnp.where((c2 >= r2) if reverse else (c2 <= r2), 1.0, 0.0).astype(BF16)
    klane = lax.broadcasted_iota(jnp.int32, (1, LANES), 1)
    vlane = lax.broadcasted_iota(jnp.int32, (1, PAIR_VW), 1)
    t = lax.broadcasted_iota(jnp.int32, (c, LANES), 0)
    s = lax.broadcasted_iota(jnp.int32, (c, LANES), 1) % c
    levels = []
    m = c
    while m >= 2:
        half = m // 2
        late = (row % m) >= half
        q_rows = ~late if reverse else late
        same = (t // m) == (s // m)
        t_late = (t % m) >= half
        s_late = (s % m) >= half
        mask = same & (~t_late) & s_late if reverse else same & t_late & (~s_late)
        levels.append((m, q_rows, mask))
        m = half
    key_first = klane < GLA_DK
    srow = lax.broadcasted_iota(jnp.int32, (PAIR_VW, 1), 0)
    state_mask = (srow < GLA_DV) == key_first
    return dict(tri=tri, key_first=key_first, val_first=vlane < GLA_DV, state_mask=state_mask,
                levels=levels, diag=(t == s))


def _pair_blockdiag_rows(x, first_mask):
    zero = jnp.zeros_like(x)
    return jnp.concatenate([jnp.where(first_mask, x, zero), jnp.where(first_mask, zero, x)], axis=0)


def _gla_stages(streams):
    c = GLA_CHUNK
    pair_k = [slice(p * LANES, (p + 1) * LANES) for p in range(GLA_PAIRS)]
    pair_v = [slice(p * PAIR_VW, (p + 1) * PAIR_VW) for p in range(GLA_PAIRS)]

    def cumulative_decay():
        for s in streams:
            g, tri = s["g"], s["consts"]["tri"]
            g1 = g.astype(BF16)
            g2 = (g - g1.astype(F32)).astype(BF16)
            g3 = (g - g1.astype(F32) - g2.astype(F32)).astype(BF16)
            s["b"] = (jnp.dot(tri, g1, preferred_element_type=F32)
                      + jnp.dot(tri, g2, preferred_element_type=F32)
                      + jnp.dot(tri, g3, preferred_element_type=F32))

    def diagonal():
        for s in streams:
            kf = s["consts"]["key_first"]
            qb, kb = s["q"].astype(BF16), s["k"].astype(BF16)
            s["att"] = [jnp.where(s["consts"]["diag"],
                                  _dot_nt(qb[:, kl], _pair_blockdiag_rows(kb[:, kl], kf)), 0.0) for kl in pair_k]

    def level(lvl):
        for s in streams:
            m, q_rows, mask = s["consts"]["levels"][lvl]
            half = m // 2
            if m == 2:
                x = jnp.where(q_rows, s["q"] * jnp.exp2(s["g"]), s["k"])
            else:
                ref = _level_ref(s["b"], m, half if s["reverse"] else half - 1)
                x = jnp.exp2(-jnp.abs(s["b"] - ref)) * _halves_select(m, q_rows, s["q"], s["k"], s["reverse"])
            x = x.astype(BF16)
            kf = s["consts"]["key_first"]
            s["att"] = [jnp.where(mask, _dot_nt(x[:, kl], _pair_blockdiag_rows(x[:, kl], kf)), att)
                        for kl, att in zip(pair_k, s["att"])]

    def state_operands():
        for s in streams:
            b = s["b"]
            b_last = b[0:1] if s["reverse"] else b[c - 1:c]
            s["q_in"] = (s["q"] * jnp.exp2(b)).astype(BF16)
            s["k_hat"] = (s["k"] * jnp.exp2(b_last - b)).astype(BF16)
            s["decay"] = jnp.exp2(b_last)
            s["att"] = [att.astype(BF16) for att in s["att"]]

    def output_and_update(s):
        vf = s["consts"]["val_first"]
        for p, (kl, vl) in enumerate(zip(pair_k, pair_v)):
            vp = s["v"][:, vl]
            st = s["st_ref"][p]
            o = jnp.dot(s["att"][p], _pair_blockdiag_rows(vp, vf), preferred_element_type=F32)
            o = o + _dot_nt(s["q_in"][:, kl], st.astype(BF16))
            s["o_ref"][:, vl] = o
            upd = _dot_tn(vp, s["k_hat"][:, kl])
            s["st_ref"][p] = st * s["decay"][:, kl] + jnp.where(s["consts"]["state_mask"], upd, 0.0)

    n_levels = len(streams[0]["consts"]["levels"])
    prep = [cumulative_decay, diagonal] + [functools.partial(level, lvl) for lvl in range(n_levels)]
    prep.append(state_operands)
    tail = [functools.partial(output_and_update, s) for s in streams]
    return prep, tail


def _interleave(first, second):
    done = 0
    for idx, thunk in enumerate(first):
        thunk()
        upto = (len(second) * (idx + 1)) // len(first)
        for extra in second[done:upto]:
            extra()
        done = upto


def _gla_kernel(qf_ref, kf_ref, vf_ref, gf_ref, qb_ref, kb_ref, vb_ref, gb_ref, s0_ref,
                of_ref, ob_ref, sout_ref, st_ref, *, chunks):
    @pl.when(pl.program_id(0) == 0)
    def _():
        st_ref[...] = s0_ref[...]

    bsz = qf_ref.shape[0]
    dirs = ((0, qf_ref, kf_ref, vf_ref, gf_ref, of_ref), (1, qb_ref, kb_ref, vb_ref, gb_ref, ob_ref))
    consts = [_gla_consts(reverse=False), _gla_consts(reverse=True)]

    def chunk_stages(ci):
        streams = []
        for d, q_ref, k_ref, v_ref, g_ref, o_ref in dirs:
            rows = pl.ds(((chunks - 1 - ci) if d else ci) * GLA_CHUNK, GLA_CHUNK)
            for bi in range(bsz):
                streams.append(dict(q=q_ref[bi, rows, :], k=k_ref[bi, rows, :], v=v_ref[bi, rows, :],
                                    g=g_ref[bi, rows, :], st_ref=st_ref.at[d, bi], o_ref=o_ref.at[bi, rows],
                                    consts=consts[d], reverse=bool(d)))
        return _gla_stages(streams)

    prep, tail = chunk_stages(0)
    for thunk in prep:
        thunk()
    for ci in range(chunks):
        if ci + 1 < chunks:
            next_prep, next_tail = chunk_stages(ci + 1)
            _interleave(next_prep, tail)
            tail = next_tail
        else:
            for thunk in tail:
                thunk()

    @pl.when(pl.program_id(0) == pl.num_programs(0) - 1)
    def _():
        sout_ref[...] = st_ref[...]


def _gla_call(q, k, v, lg, state0):
    bsz, n, kw = q.shape
    vw = v.shape[2]
    chunks = min(GLA_CHUNKS_PER_STEP, n // GLA_CHUNK)
    br = chunks * GLA_CHUNK
    nstep = n // br

    def fwd(i):
        return (0, i, 0)

    def bwd(i):
        return (0, nstep - 1 - i, 0)

    return pl.pallas_call(
        functools.partial(_gla_kernel, chunks=chunks),
        grid=(nstep,),
        in_specs=[
            pl.BlockSpec((bsz, br, kw), fwd),
            pl.BlockSpec((bsz, br, kw), fwd),
            pl.BlockSpec((bsz, br, vw), fwd),
            pl.BlockSpec((bsz, br, kw), fwd),
            pl.BlockSpec((bsz, br, kw), bwd),
            pl.BlockSpec((bsz, br, kw), bwd),
            pl.BlockSpec((bsz, br, vw), bwd),
            pl.BlockSpec((bsz, br, kw), lambda i: (0, nstep - 1 - i, 1)),
            _full_spec(state0),
        ],
        out_specs=[pl.BlockSpec((bsz, br, vw), fwd), pl.BlockSpec((bsz, br, vw), bwd),
                   pl.BlockSpec(state0.shape, lambda i: (0,) * state0.ndim)],
        out_shape=[jax.ShapeDtypeStruct((bsz, n, vw), F32)] * 2
                  + [jax.ShapeDtypeStruct(state0.shape, F32)],
        scratch_shapes=[pltpu.VMEM(state0.shape, F32)],
        compiler_params=_params(("arbitrary",)),
        name="gla",
    )(q, k, v, lg, q, k, v, lg, state0)


def _pair_blockdiag(w):
    hd = w.shape[-1]
    per = LANES // hd
    lead = w.shape[:-3]
    wg = w.reshape(lead + (RG_HEADS // per, per, hd, hd))
    eye = jnp.eye(per, dtype=w.dtype)
    return jnp.einsum('...gpij,pq->...gpiqj', wg, eye).reshape(lead + (RG_HEADS // per, LANES, LANES))


def kernel(x, c, ctx, c_ctx, w_mod, b_mod, norm_g, ffn_w1, ffn_w3, ffn_w2, w_in, conv_w, conv_b,
           rg_lam, rg_wa, rg_ba, rg_wi, rg_bi, gla_wup, gla_bup, gla_norm_g, w_out, final_g):
    bsz, t, d = x.shape
    n_ctx = ctx.shape[1]
    depth = w_mod.shape[0]
    assert t % TOKEN_BLOCK == 0 and t % FFN_TOKENS == 0 and t % (GRID_W * SUBLANES) == 0
    assert n_ctx % (SUBLANES * SUBLANES) == 0 and n_ctx % GLA_CHUNK == 0 and bsz * n_ctx <= TOKEN_BLOCK

    rows = -(-(bsz + 1) // SUBLANES) * SUBLANES
    cvec = jnp.zeros((rows, d), F32).at[0].set(c_ctx).at[1:bsz + 1].set(c)
    mods = _mod_call(cvec, w_mod, b_mod).reshape(depth, rows, N_MOD, d)

    w1, w3, w2 = ffn_w1.astype(BF16), ffn_w3.astype(BF16), ffn_w2.astype(BF16)
    w_in_b, w_out_b = w_in.astype(BF16), w_out.astype(BF16)
    norm_g4 = norm_g.reshape(depth, 3, 1, d)
    zero = jnp.zeros((depth, GLA_RANK, GLA_KW), F32)
    wup_bd = jnp.concatenate([jnp.concatenate([gla_wup[:, 0], zero], axis=2),
                              jnp.concatenate([zero, gla_wup[:, 1]], axis=2)], axis=1).astype(BF16)
    bup = gla_bup.reshape(depth, 1, 2 * GLA_KW)
    wg = jnp.concatenate([_pair_blockdiag(rg_wa), _pair_blockdiag(rg_wi)], axis=-1).astype(BF16)
    conv_b3 = conv_b.reshape(depth, 1, RG_WIDTH)
    lam4 = rg_lam.reshape(depth, 2, 1, RG_WIDTH)
    ba4 = rg_ba.reshape(depth, 2, 1, RG_WIDTH)
    bi4 = rg_bi.reshape(depth, 2, 1, RG_WIDTH)
    gnorm = gla_norm_g.reshape(depth, 1, GLA_VW)
    fg = final_g.reshape(1, d)

    h_c = ctx.reshape(1, bsz * n_ctx, d)
    grid_rows = t // GRID_W
    h_l = x.reshape(bsz, grid_rows, GRID_W, d).swapaxes(1, 2).reshape(bsz, t, d)
    ctx_seg = n_ctx // SUBLANES
    for l in range(depth):
        last = l == depth - 1
        ffn = functools.partial(_ffn_call, mods=mods, norm_g4=norm_g4, w1=w1, w3=w3, w2=w2, final_g=fg, l=l)
        h_l = ffn(h_l, half=0, is_ctx=False, final=False)
        h_c = ffn(h_c, half=0, is_ctx=True, final=False)

        proj = functools.partial(_proj_call, mods=mods, norm_g4=norm_g4, w_in=w_in_b, wup_bd=wup_bd,
                                 bup=bup, l=l)
        xr_l, gr_l, q_l, k_l, v_l, og_l, lg_l = proj(h_l, is_ctx=False, seqs_per_block=0)
        xr_c, gr_c, q_c, k_c, v_c, og_c, lg_c = proj(h_c, is_ctx=True, seqs_per_block=bsz)

        scans_l, scans_c = [], []
        for dr in range(2):
            rg = functools.partial(_rg_call, conv_w=conv_w, conv_b=conv_b3, lam=lam4, wg=wg, ba=ba4,
                                   bi=bi4, l=l, d=dr, reverse=bool(dr))
            s_c, carry = rg(xr_c.reshape(bsz, ctx_seg, SUBLANES, RG_WIDTH), jnp.zeros((bsz, 1, RG_WIDTH), F32))
            s_l, _ = rg(xr_l.reshape(bsz, GRID_W, grid_rows, RG_WIDTH), carry)
            scans_c.append(s_c.reshape(xr_c.shape))
            scans_l.append(s_l.reshape(xr_l.shape))

        per_seq = lambda a: a.reshape(bsz, n_ctx, a.shape[-1])
        state0 = jnp.zeros((2, bsz, GLA_PAIRS, PAIR_VW, LANES), F32)
        oc_f, oc_b, state = _gla_call(per_seq(q_c), per_seq(k_c), per_seq(v_c), per_seq(lg_c), state0)
        ol_f, ol_b, _ = _gla_call(q_l, k_l, v_l, lg_l, state)

        mixer_l = dict(gr=gr_l, hf=scans_l[0], hb=scans_l[1], o_f=ol_f, o_b=ol_b, og=og_l, gnorm=gnorm,
                       w_out=w_out_b, seqs_per_block=0)
        h_l = ffn(h_l, half=1, is_ctx=False, final=last, mixer=mixer_l)
        if not last:
            flat = lambda a: a.reshape(1, bsz * n_ctx, a.shape[-1])
            mixer_c = dict(gr=gr_c, hf=scans_c[0], hb=scans_c[1], o_f=flat(oc_f), o_b=flat(oc_b), og=og_c,
                           gnorm=gnorm, w_out=w_out_b, seqs_per_block=bsz)
            h_c = ffn(h_c, half=1, is_ctx=True, final=False, mixer=mixer_c)
    return h_l.reshape(bsz, GRID_W, grid_rows, d).swapaxes(1, 2).reshape(bsz, t, d)
```

```python
import functools
import math

import jax
import jax.numpy as jnp
from jax import lax
from jax.experimental import pallas as pl
from jax.experimental.pallas import tpu as pltpu

F32 = jnp.float32
BF16 = jnp.bfloat16

LANES = 128
SUBLANES = 8
FFN_TOKENS = 512
TOKEN_BLOCK = 512
SUB_ROWS = 256
FF_STEP = 1024
GLA_CHUNKS_PER_STEP = 4
VMEM_LIMIT = 56 * 1024 * 1024

N_MOD = 9
EPS = 1e-6
GRID_W = 64
RG_WIDTH = 512
RG_HEADS = 8
RG_C = 8.0
GLA_HEADS = 4
GLA_DK = 64
GLA_DV = 128
GLA_KW = GLA_HEADS * GLA_DK
GLA_VW = GLA_HEADS * GLA_DV
GLA_RANK = 16
GLA_GATE_NORM = 16.0
GLA_CHUNK = 64
GLA_PAIRS = GLA_KW // LANES
PAIR_VW = GLA_VW // GLA_PAIRS
LOG2E = math.log2(math.e)


def _params(sem):
    return pltpu.CompilerParams(dimension_semantics=sem, vmem_limit_bytes=VMEM_LIMIT)


def _fixed_spec(shape, index):
    return pl.BlockSpec(shape, lambda *_: index, pipeline_mode=pl.Buffered(1))


def _full_spec(arr):
    return _fixed_spec(arr.shape, (0,) * arr.ndim)


def _sigmoid(x):
    return 1.0 / (1.0 + jnp.exp(-x))


def _softplus(x):
    return jnp.maximum(x, 0.0) + jnp.log(1.0 + jnp.exp(-jnp.abs(x)))


def _ada_norm(x, m, g, j):
    ms = jnp.mean(x * x, axis=-1, keepdims=True)
    y = x * lax.rsqrt(ms + EPS) * g
    return y * (1.0 + m[3 * j + 1:3 * j + 2]) + m[3 * j:3 * j + 1]


def _token_block(n, target):
    return min(n, target)


def _row_blocks(n_rows):
    step = min(SUB_ROWS, n_rows)
    return [slice(r, r + step) for r in range(0, n_rows, step)]


def _staged(blocks, prologue, matmul, epilogue):
    operand = prologue(blocks[0])
    pending = None
    for i, rows in enumerate(blocks):
        acc = matmul(operand)
        if i + 1 < len(blocks):
            operand = prologue(blocks[i + 1])
        if pending is not None:
            epilogue(*pending)
        pending = (rows, acc)
    epilogue(*pending)


def _mod_kernel(c_ref, w_ref, b_ref, o_ref):
    c = c_ref[...]
    sc = (c * _sigmoid(c)).astype(BF16)
    o_ref[...] = jnp.dot(sc, w_ref[...].astype(BF16), preferred_element_type=F32) + b_ref[...]


def _mod_call(cvec, w_mod, b_mod):
    depth, d, n = w_mod.shape
    rows = cvec.shape[0]
    tn = n // 8
    return pl.pallas_call(
        _mod_kernel,
        grid=(depth, n // tn),
        in_specs=[
            pl.BlockSpec((rows, d), lambda l, i: (0, 0)),
            pl.BlockSpec((None, d, tn), lambda l, i: (l, 0, i)),
            pl.BlockSpec((None, 1, tn), lambda l, i: (l, 0, i)),
        ],
        out_specs=pl.BlockSpec((None, rows, tn), lambda l, i: (l, 0, i)),
        out_shape=jax.ShapeDtypeStruct((depth, rows, n), F32),
        compiler_params=_params(("parallel", "parallel")),
        name="mod",
    )(cvec, w_mod, b_mod.reshape(depth, 1, n))


def _mod_spec(mods, l, is_ctx):
    d = mods.shape[-1]
    if is_ctx:
        return pl.BlockSpec((None, None, N_MOD, d), lambda b, i: (l, 0, 0, 0))
    return pl.BlockSpec((None, None, N_MOD, d), lambda b, i: (l, b + 1, 0, 0))


def _tok_spec(tm, w):
    return pl.BlockSpec((None, tm, w), lambda b, i: (b, i, 0))


def _store_interleaved(ref, val):
    nb, seg_len, _ = ref.shape
    w = val.shape[1]
    for bb in range(nb):
        for s in range(SUBLANES):
            r0 = (bb * SUBLANES + s) * seg_len
            ref[bb, :, s * w:(s + 1) * w] = val[r0:r0 + seg_len]


def _load_interleaved(ref):
    nb, _, sw = ref.shape
    w = sw // SUBLANES
    return jnp.concatenate([ref[bb, :, s * w:(s + 1) * w]
                            for bb in range(nb) for s in range(SUBLANES)], axis=0)


def _interleaved_spec(nb, seg_len, w, nblk):
    return pl.BlockSpec((nb, seg_len, SUBLANES * w), lambda b, i: (b * nblk + i, 0, 0))


def _mixer_rows(rows, gr_ref, hf_ref, hb_ref, of_ref, ob_ref, og_ref, gn):
    gr = gr_ref[rows, :]
    gelu = 0.5 * gr * (1.0 + jnp.tanh(0.7978845608028654 * (gr + 0.044715 * (gr * gr * gr))))
    if len(hf_ref.shape) == 3:
        rg = gelu * (_load_interleaved(hf_ref) + _load_interleaved(hb_ref))
    else:
        rg = gelu * (hf_ref[rows, :] + hb_ref[rows, :])
    o = of_ref[rows, :] + ob_ref[rows, :]
    og = og_ref[rows, :]
    parts = []
    for hd in range(GLA_HEADS):
        sl = slice(hd * GLA_DV, (hd + 1) * GLA_DV)
        oh = o[:, sl]
        ms = jnp.mean(oh * oh, axis=-1, keepdims=True)
        parts.append(oh * lax.rsqrt(ms + EPS) * gn[:, sl])
    gla = jnp.concatenate(parts, axis=-1) * (og * _sigmoid(og))
    return jnp.concatenate([rg, gla], axis=-1).astype(BF16)


def _ffn_kernel(*refs, j, final, ff_chunks, with_mixer):
    if with_mixer:
        (x_ref, m_ref, g_ref, gr_ref, hf_ref, hb_ref, of_ref, ob_ref, og_ref, gn_ref, wo_ref,
         w1_ref, w3_ref, w2_ref, fg_ref, o_ref) = refs
        whole = len(hf_ref.shape) == 3
    else:
        x_ref, m_ref, g_ref, w1_ref, w3_ref, w2_ref, fg_ref, o_ref = refs
        whole = False
    m = m_ref[...]
    n_rows = x_ref.shape[0]

    def prologue(rows):
        x = x_ref[rows, :]
        if with_mixer:
            mix = _mixer_rows(rows, gr_ref, hf_ref, hb_ref, of_ref, ob_ref, og_ref, gn_ref[...])
            x = x + m[5:6] * jnp.dot(mix, wo_ref[...], preferred_element_type=F32)
        return x, _ada_norm(x, m, g_ref[...], j).astype(BF16)

    def matmul(operand):
        x, u = operand
        y = None
        for lo, hi in ff_chunks:
            h1 = jnp.dot(u, w1_ref[:, lo:hi], preferred_element_type=F32)
            h3 = jnp.dot(u, w3_ref[:, lo:hi], preferred_element_type=F32)
            act = (h1 * _sigmoid(h1) * h3).astype(BF16)
            part = jnp.dot(act, w2_ref[lo:hi, :], preferred_element_type=F32)
            y = part if y is None else y + part
        return x, y

    def epilogue(rows, acc):
        x, y = acc
        out = x + (0.5 * m[3 * j + 2:3 * j + 3]) * y
        if final:
            ms = jnp.mean(out * out, axis=-1, keepdims=True)
            out = out * lax.rsqrt(ms + EPS) * fg_ref[...]
        o_ref[rows, :] = out

    _staged([slice(0, n_rows)] if whole else _row_blocks(n_rows), prologue, matmul, epilogue)


def _ffn_call(h, mods, norm_g4, w1, w3, w2, final_g, *, l, half, is_ctx, final, mixer=None):
    bsz, n, d = h.shape
    dff = w1.shape[-1]
    tm = _token_block(n, FFN_TOKENS)
    nblk = n // tm
    j = 2 * half
    ff_chunks = tuple((lo, min(lo + FF_STEP, dff)) for lo in range(0, dff, FF_STEP))
    operands = [h, mods, norm_g4]
    in_specs = [_tok_spec(tm, d), _mod_spec(mods, l, is_ctx), _fixed_spec((None, None, 1, d), (l, j, 0, 0))]
    if mixer is not None:
        spb = mixer["seqs_per_block"]
        if spb:
            scan_spec = _interleaved_spec(spb, tm // (spb * SUBLANES), RG_WIDTH, nblk)
        else:
            scan_spec = _tok_spec(tm, RG_WIDTH)
        operands += [mixer[name] for name in ("gr", "hf", "hb", "o_f", "o_b", "og", "gnorm", "w_out")]
        in_specs += [_tok_spec(tm, RG_WIDTH), scan_spec, scan_spec,
                     _tok_spec(tm, GLA_VW), _tok_spec(tm, GLA_VW), _tok_spec(tm, GLA_VW),
                     _fixed_spec((None, 1, GLA_VW), (l, 0, 0)),
                     _fixed_spec((None,) + mixer["w_out"].shape[1:], (l, 0, 0))]
    operands += [w1, w3, w2, final_g]
    in_specs += [_fixed_spec((None, None, d, dff), (l, half, 0, 0)),
                 _fixed_spec((None, None, d, dff), (l, half, 0, 0)),
                 _fixed_spec((None, None, dff, d), (l, half, 0, 0)),
                 _full_spec(final_g)]
    return pl.pallas_call(
        functools.partial(_ffn_kernel, j=j, final=final, ff_chunks=ff_chunks, with_mixer=mixer is not None),
        grid=(bsz, nblk),
        in_specs=in_specs,
        out_specs=_tok_spec(tm, d),
        out_shape=jax.ShapeDtypeStruct((bsz, n, d), F32),
        compiler_params=_params(("parallel", "parallel")),
        name="mix_ffn" if mixer is not None else "ffn",
    )(*operands)


def _proj_kernel(x_ref, m_ref, g_ref, w_ref, wup_ref, bup_ref,
                 xr_ref, gr_ref, q_ref, k_ref, v_ref, og_ref, lg_ref):
    interleaved = len(xr_ref.shape) == 3
    m = m_ref[...]

    def prologue(rows):
        return _ada_norm(x_ref[rows, :], m, g_ref[...], 1).astype(BF16)

    def matmul(u):
        return jnp.dot(u, w_ref[...], preferred_element_type=F32)

    def epilogue(rows, p):
        o = 0
        if interleaved:
            _store_interleaved(xr_ref, p[:, o:o + RG_WIDTH])
        else:
            xr_ref[rows, :] = p[:, o:o + RG_WIDTH]
        o += RG_WIDTH
        gr_ref[rows, :] = p[:, o:o + RG_WIDTH]; o += RG_WIDTH
        q_ref[rows, :] = p[:, o:o + GLA_KW] * (GLA_DK ** -0.5); o += GLA_KW
        k_ref[rows, :] = p[:, o:o + GLA_KW]; o += GLA_KW
        v_ref[rows, :] = p[:, o:o + GLA_VW].astype(BF16); o += GLA_VW
        og_ref[rows, :] = p[:, o:o + GLA_VW]; o += GLA_VW
        lr = p[:, o:o + 2 * GLA_RANK].astype(BF16)
        z = jnp.dot(lr, wup_ref[...], preferred_element_type=F32) + bup_ref[...]
        lg_ref[rows, :] = _softplus(-z) * (-LOG2E / GLA_GATE_NORM)

    n_rows = x_ref.shape[0]
    _staged([slice(0, n_rows)] if interleaved else _row_blocks(n_rows), prologue, matmul, epilogue)


def _proj_call(h, mods, norm_g4, w_in, wup_bd, bup, *, l, is_ctx, seqs_per_block):
    bsz, n, d = h.shape
    tm = _token_block(n, TOKEN_BLOCK)
    nblk = n // tm
    outs = ((RG_WIDTH, F32), (GLA_KW, F32), (GLA_KW, F32), (GLA_VW, BF16), (GLA_VW, F32), (2 * GLA_KW, F32))
    if seqs_per_block:
        seg_len = tm // (seqs_per_block * SUBLANES)
        xr_shape = (bsz * nblk * seqs_per_block, seg_len, SUBLANES * RG_WIDTH)
        xr_spec = _interleaved_spec(seqs_per_block, seg_len, RG_WIDTH, nblk)
    else:
        xr_shape = (bsz, n, RG_WIDTH)
        xr_spec = _tok_spec(tm, RG_WIDTH)
    return pl.pallas_call(
        _proj_kernel,
        grid=(bsz, nblk),
        in_specs=[
            _tok_spec(tm, d),
            _mod_spec(mods, l, is_ctx),
            _fixed_spec((None, None, 1, d), (l, 1, 0, 0)),
            _fixed_spec((None,) + w_in.shape[1:], (l, 0, 0)),
            _fixed_spec((None,) + wup_bd.shape[1:], (l, 0, 0)),
            _fixed_spec((None, 1, bup.shape[-1]), (l, 0, 0)),
        ],
        out_specs=[xr_spec] + [_tok_spec(tm, w) for w, _ in outs],
        out_shape=[jax.ShapeDtypeStruct(xr_shape, F32)]
                  + [jax.ShapeDtypeStruct((bsz, n, w), dt) for w, dt in outs],
        compiler_params=_params(("parallel", "parallel")),
        name="proj",
    )(h, mods, norm_g4, w_in, wup_bd, bup)


def _rg_kernel(x_ref, xp_ref, xn_ref, c0_ref, cw_ref, cb_ref, lam_ref, wg_ref, ba_ref, bi_ref,
               o_ref, cout_ref, carry_ref, *, reverse, nblk):
    seg_len = x_ref.shape[0]
    j = pl.program_id(1)
    blk = (nblk - 1 - j) if reverse else j

    @pl.when(j == 0)
    def _():
        carry_ref[...] = c0_ref[...]

    x = x_ref[...]
    last = SUBLANES - 1
    prev2 = jnp.where(blk == 0, 0.0, xp_ref[last - 1, last:, :])
    prev1 = jnp.where(blk == 0, 0.0, xp_ref[last, last:, :])
    next1 = jnp.where(blk == nblk - 1, 0.0, xn_ref[0, 0:1, :])
    row8 = lax.broadcasted_iota(jnp.int32, (SUBLANES, 1), 0)

    def from_prev_segment(tile, halo):
        return jnp.where(row8 == 0, halo, pltpu.roll(tile, 1, axis=0))

    def from_next_segment(tile, halo):
        return jnp.where(row8 == last, halo, pltpu.roll(tile, last, axis=0))

    xm2 = from_prev_segment(x[seg_len - 2], prev2)
    xm1 = from_prev_segment(x[seg_len - 1], prev1)
    xp1 = from_next_segment(x[0], next1)
    xe = jnp.concatenate([xm2[None], xm1[None], x, xp1[None]], axis=0)
    cw = cw_ref[...]
    xc = xe[0:seg_len] * cw[0:1] + xe[1:seg_len + 1] * cw[1:2]
    xc = xc + xe[2:seg_len + 2] * cw[2:3]
    xc = xc + xe[3:seg_len + 3] * cw[3:4]
    xc = (xc + cb_ref[...]).reshape(seg_len * SUBLANES, RG_WIDTH)

    sp = _softplus(-lam_ref[...])
    order = range(seg_len - 1, -1, -1) if reverse else range(seg_len)
    for g in range(RG_WIDTH // LANES):
        sl = slice(g * LANES, (g + 1) * LANES)
        xg = xc[:, sl]
        pre = jnp.dot(xg.astype(BF16), wg_ref[g], preferred_element_type=F32)
        r = _sigmoid(pre[:, :LANES] + ba_ref[:, sl])
        i = _sigmoid(pre[:, LANES:] + bi_ref[:, sl])
        log_a = (-RG_C) * r * sp[:, sl]
        a = jnp.exp(log_a)
        bv = jnp.sqrt(1.0 - a * a) * (i * xg)
        hs, ps = [None] * seg_len, [None] * seg_len
        h_run = p_run = None
        for t in order:
            rows = slice(t * SUBLANES, (t + 1) * SUBLANES)
            if h_run is None:
                h_run, p_run = bv[rows], a[rows]
            else:
                h_run = a[rows] * h_run + bv[rows]
                p_run = a[rows] * p_run
            hs[t], ps[t] = h_run, p_run
        pc, hc = p_run, h_run
        s = 1
        while s < SUBLANES:
            shift = SUBLANES - s if reverse else s
            valid = (row8 < SUBLANES - s) if reverse else (row8 >= s)
            hc = jnp.where(valid, pc * pltpu.roll(hc, shift, axis=0) + hc, hc)
            pc = jnp.where(valid, pc * pltpu.roll(pc, shift, axis=0), pc)
            s *= 2
        carry = carry_ref[:, sl]
        seg_end = pc * carry + hc
        if reverse:
            seg_in = jnp.where(row8 == last, carry, pltpu.roll(seg_end, last, axis=0))
            carry_ref[:, sl] = seg_end[0:1]
        else:
            seg_in = jnp.where(row8 == 0, carry, pltpu.roll(seg_end, 1, axis=0))
            carry_ref[:, sl] = seg_end[last:]
        for t in range(seg_len):
            o_ref[t, :, sl] = hs[t] + ps[t] * seg_in

    @pl.when(j == nblk - 1)
    def _():
        cout_ref[...] = carry_ref[...]


def _rg_call(x4, carry0, conv_w, conv_b, lam, wg, ba, bi, *, l, d, reverse):
    bsz, seg_len, nseg, w = x4.shape
    nblk = nseg // SUBLANES
    tiles = seg_len // SUBLANES

    def blk(j):
        return (nblk - 1 - j) if reverse else j

    vec = _fixed_spec((None, None, 1, w), (l, d, 0, 0))
    return pl.pallas_call(
        functools.partial(_rg_kernel, reverse=reverse, nblk=nblk),
        grid=(bsz, nblk),
        in_specs=[
            pl.BlockSpec((None, seg_len, SUBLANES, w), lambda b, j: (b, 0, blk(j), 0)),
            pl.BlockSpec((None, SUBLANES, SUBLANES, w),
                         lambda b, j: (b, tiles - 1, jnp.maximum(blk(j) - 1, 0), 0)),
            pl.BlockSpec((None, SUBLANES, SUBLANES, w),
                         lambda b, j: (b, 0, jnp.minimum(blk(j) + 1, nblk - 1), 0)),
            pl.BlockSpec((None, 1, w), lambda b, j: (b, 0, 0)),
            _fixed_spec((None,) + conv_w.shape[1:], (l, 0, 0)),
            _fixed_spec((None, 1, w), (l, 0, 0)),
            vec,
            _fixed_spec((None, None) + wg.shape[2:], (l, d, 0, 0, 0)),
            vec,
            vec,
        ],
        out_specs=[pl.BlockSpec((None, seg_len, SUBLANES, w), lambda b, j: (b, 0, blk(j), 0)),
                   pl.BlockSpec((None, 1, w), lambda b, j: (b, 0, 0))],
        out_shape=[jax.ShapeDtypeStruct(x4.shape, F32), jax.ShapeDtypeStruct((bsz, 1, w), F32)],
        scratch_shapes=[pltpu.VMEM((1, w), F32)],
        compiler_params=_params(("arbitrary", "arbitrary")),
        name="rg_bwd" if reverse else "rg_fwd",
    )(x4, x4, x4, carry0, conv_w, conv_b, lam, wg, ba, bi)


def _level_ref(b, m, r):
    c, w = b.shape
    if m >= SUBLANES:
        b3 = b.reshape(c // m, m, w)
        return jnp.broadcast_to(b3[:, r:r + 1, :], (c // m, m, w)).reshape(c, w)
    bt = b.reshape(c // SUBLANES, SUBLANES, w)
    sub = lax.broadcasted_iota(jnp.int32, (1, SUBLANES, 1), 1)
    out = None
    for blk in range(SUBLANES // m):
        src = blk * m + r
        piece = jnp.broadcast_to(bt[:, src:src + 1, :], bt.shape)
        out = piece if out is None else jnp.where(sub >= blk * m, piece, out)
    return out.reshape(c, w)


def _halves_select(m, q_rows, q, k, reverse):
    half = m // 2
    if half % SUBLANES:
        return jnp.where(q_rows, q, k)
    first, second = (q, k) if reverse else (k, q)
    pieces = []
    for lo in range(0, q.shape[0], m):
        pieces += [first[lo:lo + half], second[lo + half:lo + m]]
    return jnp.concatenate(pieces, axis=0)


def _dot_nt(a, b):
    return lax.dot_general(a, b, (((1,), (1,)), ((), ())), preferred_element_type=F32)


def _dot_tn(a, b):
    return lax.dot_general(a, b, (((0,), (0,)), ((), ())), preferred_element_type=F32)


def _gla_consts(reverse):
    c = GLA_CHUNK
    row = lax.broadcasted_iota(jnp.int32, (c, 1), 0)
    r2 = lax.broadcasted_iota(jnp.int32, (c, c), 0)
    c2 = lax.broadcasted_iota(jnp.int32, (c, c), 1)
    tri = jnp.where((c2 >= r2) if reverse else (c2 <= r2), 1.0, 0.0).astype(BF16)
    klane = lax.broadcasted_iota(jnp.int32, (1, LANES), 1)
    vlane = lax.broadcasted_iota(jnp.int32, (1, PAIR_VW), 1)
    s = lax.broadcasted_iota(jnp.int32, (2 * c, c), 0) % c
    t = lax.broadcasted_iota(jnp.int32, (2 * c, c), 1)
    levels = []
    m = c
    while m >= 2:
        half = m // 2
        late = (row % m) >= half
        q_rows = ~late if reverse else late
        same = (t // m) == (s // m)
        t_late = (t % m) >= half
        s_late = (s % m) >= half
        mask = same & (~t_late) & s_late if reverse else same & t_late & (~s_late)
        levels.append((m, q_rows, mask))
        m = half
    key_first = klane < GLA_DK
    srow = lax.broadcasted_iota(jnp.int32, (PAIR_VW, 1), 0)
    state_mask = (srow < GLA_DV) == key_first
    return dict(tri=tri, key_first=key_first, val_first=vlane < GLA_DV, state_mask=state_mask,
                levels=levels, diag=(t == s))


def _pair_blockdiag_rows(x, first_mask):
    zero = jnp.zeros_like(x)
    return jnp.concatenate([jnp.where(first_mask, x, zero), jnp.where(first_mask, zero, x)], axis=0)


def _gla_stages(streams):
    c = GLA_CHUNK
    pair_k = [slice(p * LANES, (p + 1) * LANES) for p in range(GLA_PAIRS)]
    pair_v = [slice(p * PAIR_VW, (p + 1) * PAIR_VW) for p in range(GLA_PAIRS)]

    def cumulative_decay():
        for s in streams:
            g, tri = s["g"], s["consts"]["tri"]
            g1 = g.astype(BF16)
            g2 = (g - g1.astype(F32)).astype(BF16)
            g3 = (g - g1.astype(F32) - g2.astype(F32)).astype(BF16)
            s["b"] = (jnp.dot(tri, g1, preferred_element_type=F32)
                      + jnp.dot(tri, g2, preferred_element_type=F32)
                      + jnp.dot(tri, g3, preferred_element_type=F32))

    def diagonal():
        for s in streams:
            kf = s["consts"]["key_first"]
            qb, kb = s["q"].astype(BF16), s["k"].astype(BF16)
            s["att"] = [jnp.where(s["consts"]["diag"],
                                  _dot_nt(_pair_blockdiag_rows(kb[:, kl], kf), qb[:, kl]), 0.0) for kl in pair_k]

    def level(lvl):
        for s in streams:
            m, q_rows, mask = s["consts"]["levels"][lvl]
            half = m // 2
            if m == 2:
                x = jnp.where(q_rows, s["q"] * jnp.exp2(s["g"]), s["k"])
            else:
                ref = _level_ref(s["b"], m, half if s["reverse"] else half - 1)
                x = jnp.exp2(-jnp.abs(s["b"] - ref)) * _halves_select(m, q_rows, s["q"], s["k"], s["reverse"])
            x = x.astype(BF16)
            kf = s["consts"]["key_first"]
            s["att"] = [jnp.where(mask, _dot_nt(_pair_blockdiag_rows(x[:, kl], kf), x[:, kl]), att)
                        for kl, att in zip(pair_k, s["att"])]

    def state_operands():
        for s in streams:
            b = s["b"]
            b_last = b[0:1] if s["reverse"] else b[c - 1:c]
            s["q_in"] = (s["q"] * jnp.exp2(b)).astype(BF16)
            s["k_hat"] = (s["k"] * jnp.exp2(b_last - b)).astype(BF16)
            s["decay"] = jnp.exp2(b_last)
            s["att"] = [att.astype(BF16) for att in s["att"]]

    def output_and_update(s):
        vf = s["consts"]["val_first"]
        for p, (kl, vl) in enumerate(zip(pair_k, pair_v)):
            vp = s["v"][:, vl]
            st = s["st_ref"][p]
            o = _dot_tn(s["att"][p], _pair_blockdiag_rows(vp, vf))
            o = o + _dot_nt(s["q_in"][:, kl], st.astype(BF16))
            s["o_ref"][:, vl] = o
            upd = _dot_tn(vp, s["k_hat"][:, kl])
            s["st_ref"][p] = st * s["decay"][:, kl] + jnp.where(s["consts"]["state_mask"], upd, 0.0)

    n_levels = len(streams[0]["consts"]["levels"])
    prep = [cumulative_decay, diagonal] + [functools.partial(level, lvl) for lvl in range(n_levels)]
    prep.append(state_operands)
    tail = [functools.partial(output_and_update, s) for s in streams]
    return prep, tail


def _interleave(first, second):
    done = 0
    for idx, thunk in enumerate(first):
        thunk()
        upto = (len(second) * (idx + 1)) // len(first)
        for extra in second[done:upto]:
            extra()
        done = upto


def _gla_kernel(qf_ref, kf_ref, vf_ref, gf_ref, qb_ref, kb_ref, vb_ref, gb_ref, s0_ref,
                of_ref, ob_ref, sout_ref, st_ref, *, chunks):
    @pl.when(pl.program_id(0) == 0)
    def _():
        st_ref[...] = s0_ref[...]

    bsz = qf_ref.shape[0]
    dirs = ((0, qf_ref, kf_ref, vf_ref, gf_ref, of_ref), (1, qb_ref, kb_ref, vb_ref, gb_ref, ob_ref))
    consts = [_gla_consts(reverse=False), _gla_consts(reverse=True)]

    def chunk_stages(ci):
        streams = []
        for d, q_ref, k_ref, v_ref, g_ref, o_ref in dirs:
            rows = pl.ds(((chunks - 1 - ci) if d else ci) * GLA_CHUNK, GLA_CHUNK)
            for bi in range(bsz):
                streams.append(dict(q=q_ref[bi, rows, :], k=k_ref[bi, rows, :], v=v_ref[bi, rows, :],
                                    g=g_ref[bi, rows, :], st_ref=st_ref.at[d, bi], o_ref=o_ref.at[bi, rows],
                                    consts=consts[d], reverse=bool(d)))
        return _gla_stages(streams)

    prep, tail = chunk_stages(0)
    for thunk in prep:
        thunk()
    for ci in range(chunks):
        if ci + 1 < chunks:
            next_prep, next_tail = chunk_stages(ci + 1)
            _interleave(next_prep, tail)
            tail = next_tail
        else:
            for thunk in tail:
                thunk()

    @pl.when(pl.program_id(0) == pl.num_programs(0) - 1)
    def _():
        sout_ref[...] = st_ref[...]


def _gla_call(q, k, v, lg, state0):
    bsz, n, kw = q.shape
    vw = v.shape[2]
    chunks = min(GLA_CHUNKS_PER_STEP, n // GLA_CHUNK)
    br = chunks * GLA_CHUNK
    nstep = n // br

    def fwd(i):
        return (0, i, 0)

    def bwd(i):
        return (0, nstep - 1 - i, 0)

    return pl.pallas_call(
        functools.partial(_gla_kernel, chunks=chunks),
        grid=(nstep,),
        in_specs=[
            pl.BlockSpec((bsz, br, kw), fwd),
            pl.BlockSpec((bsz, br, kw), fwd),
            pl.BlockSpec((bsz, br, vw), fwd),
            pl.BlockSpec((bsz, br, kw), fwd),
            pl.BlockSpec((bsz, br, kw), bwd),
            pl.BlockSpec((bsz, br, kw), bwd),
            pl.BlockSpec((bsz, br, vw), bwd),
            pl.BlockSpec((bsz, br, kw), lambda i: (0, nstep - 1 - i, 1)),
            _full_spec(state0),
        ],
        out_specs=[pl.BlockSpec((bsz, br, vw), fwd), pl.BlockSpec((bsz, br, vw), bwd),
                   pl.BlockSpec(state0.shape, lambda i: (0,) * state0.ndim)],
        out_shape=[jax.ShapeDtypeStruct((bsz, n, vw), F32)] * 2
                  + [jax.ShapeDtypeStruct(state0.shape, F32)],
        scratch_shapes=[pltpu.VMEM(state0.shape, F32)],
        compiler_params=_params(("arbitrary",)),
        name="gla",
    )(q, k, v, lg, q, k, v, lg, state0)


def _pair_blockdiag(w):
    hd = w.shape[-1]
    per = LANES // hd
    lead = w.shape[:-3]
    wg = w.reshape(lead + (RG_HEADS // per, per, hd, hd))
    eye = jnp.eye(per, dtype=w.dtype)
    return jnp.einsum('...gpij,pq->...gpiqj', wg, eye).reshape(lead + (RG_HEADS // per, LANES, LANES))


def kernel(x, c, ctx, c_ctx, w_mod, b_mod, norm_g, ffn_w1, ffn_w3, ffn_w2, w_in, conv_w, conv_b,
           rg_lam, rg_wa, rg_ba, rg_wi, rg_bi, gla_wup, gla_bup, gla_norm_g, w_out, final_g):
    bsz, t, d = x.shape
    n_ctx = ctx.shape[1]
    depth = w_mod.shape[0]
    assert t % TOKEN_BLOCK == 0 and t % FFN_TOKENS == 0 and t % (GRID_W * SUBLANES) == 0
    assert n_ctx % (SUBLANES * SUBLANES) == 0 and n_ctx % GLA_CHUNK == 0 and bsz * n_ctx <= TOKEN_BLOCK

    rows = -(-(bsz + 1) // SUBLANES) * SUBLANES
    cvec = jnp.zeros((rows, d), F32).at[0].set(c_ctx).at[1:bsz + 1].set(c)
    mods = _mod_call(cvec, w_mod, b_mod).reshape(depth, rows, N_MOD, d)

    w1, w3, w2 = ffn_w1.astype(BF16), ffn_w3.astype(BF16), ffn_w2.astype(BF16)
    w_in_b, w_out_b = w_in.astype(BF16), w_out.astype(BF16)
    norm_g4 = norm_g.reshape(depth, 3, 1, d)
    zero = jnp.zeros((depth, GLA_RANK, GLA_KW), F32)
    wup_bd = jnp.concatenate([jnp.concatenate([gla_wup[:, 0], zero], axis=2),
                              jnp.concatenate([zero, gla_wup[:, 1]], axis=2)], axis=1).astype(BF16)
    bup = gla_bup.reshape(depth, 1, 2 * GLA_KW)
    wg = jnp.concatenate([_pair_blockdiag(rg_wa), _pair_blockdiag(rg_wi)], axis=-1).astype(BF16)
    conv_b3 = conv_b.reshape(depth, 1, RG_WIDTH)
    lam4 = rg_lam.reshape(depth, 2, 1, RG_WIDTH)
    ba4 = rg_ba.reshape(depth, 2, 1, RG_WIDTH)
    bi4 = rg_bi.reshape(depth, 2, 1, RG_WIDTH)
    gnorm = gla_norm_g.reshape(depth, 1, GLA_VW)
    fg = final_g.reshape(1, d)

    h_c = ctx.reshape(1, bsz * n_ctx, d)
    grid_rows = t // GRID_W
    h_l = x.reshape(bsz, grid_rows, GRID_W, d).swapaxes(1, 2).reshape(bsz, t, d)
    ctx_seg = n_ctx // SUBLANES
    for l in range(depth):
        last = l == depth - 1
        ffn = functools.partial(_ffn_call, mods=mods, norm_g4=norm_g4, w1=w1, w3=w3, w2=w2, final_g=fg, l=l)
        h_l = ffn(h_l, half=0, is_ctx=False, final=False)
        h_c = ffn(h_c, half=0, is_ctx=True, final=False)

        proj = functools.partial(_proj_call, mods=mods, norm_g4=norm_g4, w_in=w_in_b, wup_bd=wup_bd,
                                 bup=bup, l=l)
        xr_l, gr_l, q_l, k_l, v_l, og_l, lg_l = proj(h_l, is_ctx=False, seqs_per_block=0)
        xr_c, gr_c, q_c, k_c, v_c, og_c, lg_c = proj(h_c, is_ctx=True, seqs_per_block=bsz)

        scans_l, scans_c = [], []
        for dr in range(2):
            rg = functools.partial(_rg_call, conv_w=conv_w, conv_b=conv_b3, lam=lam4, wg=wg, ba=ba4,
                                   bi=bi4, l=l, d=dr, reverse=bool(dr))
            s_c, carry = rg(xr_c.reshape(bsz, ctx_seg, SUBLANES, RG_WIDTH), jnp.zeros((bsz, 1, RG_WIDTH), F32))
            s_l, _ = rg(xr_l.reshape(bsz, GRID_W, grid_rows, RG_WIDTH), carry)
            scans_c.append(s_c.reshape(xr_c.shape))
            scans_l.append(s_l.reshape(xr_l.shape))

        per_seq = lambda a: a.reshape(bsz, n_ctx, a.shape[-1])
        state0 = jnp.zeros((2, bsz, GLA_PAIRS, PAIR_VW, LANES), F32)
        oc_f, oc_b, state = _gla_call(per_seq(q_c), per_seq(k_c), per_seq(v_c), per_seq(lg_c), state0)
        ol_f, ol_b, _ = _gla_call(q_l, k_l, v_l, lg_l, state)

        mixer_l = dict(gr=gr_l, hf=scans_l[0], hb=scans_l[1], o_f=ol_f, o_b=ol_b, og=og_l, gnorm=gnorm,
                       w_out=w_out_b, seqs_per_block=0)
        h_l = ffn(h_l, half=1, is_ctx=False, final=last, mixer=mixer_l)
        if not last:
            flat = lambda a: a.reshape(1, bsz * n_ctx, a.shape[-1])
            mixer_c = dict(gr=gr_c, hf=scans_c[0], hb=scans_c[1], o_f=flat(oc_f), o_b=flat(oc_b), og=og_c,
                           gnorm=gnorm, w_out=w_out_b, seqs_per_block=bsz)
            h_c = ffn(h_c, half=1, is_ctx=True, final=False, mixer=mixer_c)
    return h_l.reshape(bsz, GRID_W, grid_rows, d).swapaxes(1, 2).reshape(bsz, t, d)
```

```python
import functools
import math

import jax
import jax.numpy as jnp
from jax import lax
from jax.experimental import pallas as pl
from jax.experimental.pallas import tpu as pltpu

F32 = jnp.float32
BF16 = jnp.bfloat16

LANES = 128
SUBLANES = 8
FFN_TOKENS = 512
TOKEN_BLOCK = 1024
SUB_ROWS = 256
FF_STEP = 1024
GLA_CHUNKS_PER_STEP = 4
VMEM_LIMIT = 56 * 1024 * 1024

N_MOD = 9
EPS = 1e-6
GRID_W = 64
RG_WIDTH = 512
RG_HEADS = 8
RG_C = 8.0
GLA_HEADS = 4
GLA_DK = 64
GLA_DV = 128
GLA_KW = GLA_HEADS * GLA_DK
GLA_VW = GLA_HEADS * GLA_DV
GLA_RANK = 16
GLA_GATE_NORM = 16.0
GLA_CHUNK = 64
GLA_PAIRS = GLA_KW // LANES
PAIR_VW = GLA_VW // GLA_PAIRS
LOG2E = math.log2(math.e)


def _params(sem):
    return pltpu.CompilerParams(dimension_semantics=sem, vmem_limit_bytes=VMEM_LIMIT)


def _fixed_spec(shape, index):
    return pl.BlockSpec(shape, lambda *_: index, pipeline_mode=pl.Buffered(1))


def _full_spec(arr):
    return _fixed_spec(arr.shape, (0,) * arr.ndim)


def _sigmoid(x):
    return 1.0 / (1.0 + jnp.exp(-x))


def _softplus(x):
    return jnp.maximum(x, 0.0) + jnp.log(1.0 + jnp.exp(-jnp.abs(x)))


def _ada_norm(x, m, g, j):
    ms = jnp.mean(x * x, axis=-1, keepdims=True)
    gain = g * (1.0 + m[3 * j + 1:3 * j + 2])
    return x * lax.rsqrt(ms + EPS) * gain + m[3 * j:3 * j + 1]


def _token_block(n, target):
    return min(n, target)


def _row_blocks(n_rows):
    step = min(SUB_ROWS, n_rows)
    return [slice(r, r + step) for r in range(0, n_rows, step)]


def _staged(blocks, prologue, matmul, epilogue):
    operand = prologue(blocks[0])
    pending = None
    for i, rows in enumerate(blocks):
        acc = matmul(operand)
        if i + 1 < len(blocks):
            operand = prologue(blocks[i + 1])
        if pending is not None:
            epilogue(*pending)
        pending = (rows, acc)
    epilogue(*pending)


def _mod_kernel(c_ref, w_ref, b_ref, o_ref):
    c = c_ref[...]
    sc = (c * _sigmoid(c)).astype(BF16)
    o_ref[...] = jnp.dot(sc, w_ref[...].astype(BF16), preferred_element_type=F32) + b_ref[...]


def _mod_call(cvec, w_mod, b_mod):
    depth, d, n = w_mod.shape
    rows = cvec.shape[0]
    tn = n // 8
    return pl.pallas_call(
        _mod_kernel,
        grid=(depth, n // tn),
        in_specs=[
            pl.BlockSpec((rows, d), lambda l, i: (0, 0)),
            pl.BlockSpec((None, d, tn), lambda l, i: (l, 0, i)),
            pl.BlockSpec((None, 1, tn), lambda l, i: (l, 0, i)),
        ],
        out_specs=pl.BlockSpec((None, rows, tn), lambda l, i: (l, 0, i)),
        out_shape=jax.ShapeDtypeStruct((depth, rows, n), F32),
        compiler_params=_params(("parallel", "parallel")),
        name="mod",
    )(cvec, w_mod, b_mod.reshape(depth, 1, n))


def _mod_spec(mods, l, is_ctx):
    d = mods.shape[-1]
    if is_ctx:
        return pl.BlockSpec((None, None, N_MOD, d), lambda b, i: (l, 0, 0, 0))
    return pl.BlockSpec((None, None, N_MOD, d), lambda b, i: (l, b + 1, 0, 0))


def _tok_spec(tm, w):
    return pl.BlockSpec((None, tm, w), lambda b, i: (b, i, 0))


def _store_interleaved(ref, val):
    nb, seg_len, _ = ref.shape
    w = val.shape[1]
    for bb in range(nb):
        for s in range(SUBLANES):
            r0 = (bb * SUBLANES + s) * seg_len
            ref[bb, :, s * w:(s + 1) * w] = val[r0:r0 + seg_len]


def _load_interleaved(ref):
    nb, _, sw = ref.shape
    w = sw // SUBLANES
    return jnp.concatenate([ref[bb, :, s * w:(s + 1) * w]
                            for bb in range(nb) for s in range(SUBLANES)], axis=0)


def _interleaved_spec(nb, seg_len, w, nblk):
    return pl.BlockSpec((nb, seg_len, SUBLANES * w), lambda b, i: (b * nblk + i, 0, 0))


def _mixer_rows(rows, gr_ref, hf_ref, hb_ref, of_ref, ob_ref, og_ref, gn):
    gr = gr_ref[rows, :]
    gelu = 0.5 * gr * (1.0 + jnp.tanh(0.7978845608028654 * (gr + 0.044715 * (gr * gr * gr))))
    if len(hf_ref.shape) == 3:
        rg = gelu * (_load_interleaved(hf_ref) + _load_interleaved(hb_ref))
    else:
        rg = gelu * (hf_ref[rows, :] + hb_ref[rows, :])
    o = of_ref[rows, :] + ob_ref[rows, :]
    og = og_ref[rows, :]
    parts = []
    for hd in range(GLA_HEADS):
        sl = slice(hd * GLA_DV, (hd + 1) * GLA_DV)
        oh = o[:, sl]
        ms = jnp.mean(oh * oh, axis=-1, keepdims=True)
        parts.append(oh * lax.rsqrt(ms + EPS) * gn[:, sl])
    gla = jnp.concatenate(parts, axis=-1) * (og * _sigmoid(og))
    return jnp.concatenate([rg, gla], axis=-1).astype(BF16)


def _ffn_kernel(*refs, j, final, ff_chunks, with_mixer):
    if with_mixer:
        (x_ref, m_ref, g_ref, gr_ref, hf_ref, hb_ref, of_ref, ob_ref, og_ref, gn_ref, wo_ref,
         w1_ref, w3_ref, w2_ref, fg_ref, o_ref) = refs
        whole = len(hf_ref.shape) == 3
    else:
        x_ref, m_ref, g_ref, w1_ref, w3_ref, w2_ref, fg_ref, o_ref = refs
        whole = False
    m = m_ref[...]
    n_rows = x_ref.shape[0]

    def prologue(rows):
        x = x_ref[rows, :]
        if with_mixer:
            mix = _mixer_rows(rows, gr_ref, hf_ref, hb_ref, of_ref, ob_ref, og_ref, gn_ref[...])
            x = x + m[5:6] * jnp.dot(mix, wo_ref[...], preferred_element_type=F32)
        return x, _ada_norm(x, m, g_ref[...], j).astype(BF16)

    def matmul(operand):
        x, u = operand
        y = None
        for lo, hi in ff_chunks:
            h1 = jnp.dot(u, w1_ref[:, lo:hi], preferred_element_type=F32)
            h3 = jnp.dot(u, w3_ref[:, lo:hi], preferred_element_type=F32)
            act = (h1 * _sigmoid(h1) * h3).astype(BF16)
            part = jnp.dot(act, w2_ref[lo:hi, :], preferred_element_type=F32)
            y = part if y is None else y + part
        return x, y

    def epilogue(rows, acc):
        x, y = acc
        out = x + (0.5 * m[3 * j + 2:3 * j + 3]) * y
        if final:
            ms = jnp.mean(out * out, axis=-1, keepdims=True)
            out = out * lax.rsqrt(ms + EPS) * fg_ref[...]
        o_ref[rows, :] = out

    _staged([slice(0, n_rows)] if whole else _row_blocks(n_rows), prologue, matmul, epilogue)


def _ffn_call(h, mods, norm_g4, w1, w3, w2, final_g, *, l, half, is_ctx, final, mixer=None):
    bsz, n, d = h.shape
    dff = w1.shape[-1]
    tm = _token_block(n, FFN_TOKENS)
    nblk = n // tm
    j = 2 * half
    ff_chunks = tuple((lo, min(lo + FF_STEP, dff)) for lo in range(0, dff, FF_STEP))
    operands = [h, mods, norm_g4]
    in_specs = [_tok_spec(tm, d), _mod_spec(mods, l, is_ctx), _fixed_spec((None, None, 1, d), (l, j, 0, 0))]
    if mixer is not None:
        spb = mixer["seqs_per_block"]
        if spb:
            scan_spec = _interleaved_spec(spb, tm // (spb * SUBLANES), RG_WIDTH, nblk)
        else:
            scan_spec = _tok_spec(tm, RG_WIDTH)
        operands += [mixer[name] for name in ("gr", "hf", "hb", "o_f", "o_b", "og", "gnorm", "w_out")]
        in_specs += [_tok_spec(tm, RG_WIDTH), scan_spec, scan_spec,
                     _tok_spec(tm, GLA_VW), _tok_spec(tm, GLA_VW), _tok_spec(tm, GLA_VW),
                     _fixed_spec((None, 1, GLA_VW), (l, 0, 0)),
                     _fixed_spec((None,) + mixer["w_out"].shape[1:], (l, 0, 0))]
    operands += [w1, w3, w2, final_g]
    in_specs += [_fixed_spec((None, None, d, dff), (l, half, 0, 0)),
                 _fixed_spec((None, None, d, dff), (l, half, 0, 0)),
                 _fixed_spec((None, None, dff, d), (l, half, 0, 0)),
                 _full_spec(final_g)]
    return pl.pallas_call(
        functools.partial(_ffn_kernel, j=j, final=final, ff_chunks=ff_chunks, with_mixer=mixer is not None),
        grid=(bsz, nblk),
        in_specs=in_specs,
        out_specs=_tok_spec(tm, d),
        out_shape=jax.ShapeDtypeStruct((bsz, n, d), F32),
        compiler_params=_params(("parallel", "parallel")),
        name="mix_ffn" if mixer is not None else "ffn",
    )(*operands)


def _proj_kernel(x_ref, m_ref, g_ref, w_ref, wup_ref, bup_ref,
                 xr_ref, gr_ref, q_ref, k_ref, v_ref, og_ref, lg_ref):
    interleaved = len(xr_ref.shape) == 3
    m = m_ref[...]

    def cols(u, lo, width):
        return jnp.dot(u, w_ref[:, lo:lo + width], preferred_element_type=F32)

    def project(rows, u):
        lr = cols(u, 2 * RG_WIDTH + 2 * GLA_KW + 2 * GLA_VW, 2 * GLA_RANK).astype(BF16)
        z = jnp.dot(lr, wup_ref[...], preferred_element_type=F32) + bup_ref[...]
        lg_ref[rows, :] = _softplus(-z) * (-LOG2E / GLA_GATE_NORM)
        o = 0
        xr = cols(u, o, RG_WIDTH); o += RG_WIDTH
        if interleaved:
            _store_interleaved(xr_ref, xr)
        else:
            xr_ref[rows, :] = xr
        gr_ref[rows, :] = cols(u, o, RG_WIDTH); o += RG_WIDTH
        q_ref[rows, :] = cols(u, o, GLA_KW) * (GLA_DK ** -0.5); o += GLA_KW
        k_ref[rows, :] = cols(u, o, GLA_KW); o += GLA_KW
        v_ref[rows, :] = cols(u, o, GLA_VW).astype(BF16); o += GLA_VW
        og_ref[rows, :] = cols(u, o, GLA_VW)

    n_rows = x_ref.shape[0]
    blocks = [slice(0, n_rows)] if interleaved else _row_blocks(n_rows)
    u = _ada_norm(x_ref[blocks[0], :], m, g_ref[...], 1).astype(BF16)
    for i, rows in enumerate(blocks):
        u_next = None
        if i + 1 < len(blocks):
            u_next = _ada_norm(x_ref[blocks[i + 1], :], m, g_ref[...], 1).astype(BF16)
        project(rows, u)
        u = u_next


def _proj_call(h, mods, norm_g4, w_in, wup_bd, bup, *, l, is_ctx, seqs_per_block):
    bsz, n, d = h.shape
    tm = _token_block(n, TOKEN_BLOCK)
    nblk = n // tm
    outs = ((RG_WIDTH, F32), (GLA_KW, F32), (GLA_KW, F32), (GLA_VW, BF16), (GLA_VW, F32), (2 * GLA_KW, F32))
    if seqs_per_block:
        seg_len = tm // (seqs_per_block * SUBLANES)
        xr_shape = (bsz * nblk * seqs_per_block, seg_len, SUBLANES * RG_WIDTH)
        xr_spec = _interleaved_spec(seqs_per_block, seg_len, RG_WIDTH, nblk)
    else:
        xr_shape = (bsz, n, RG_WIDTH)
        xr_spec = _tok_spec(tm, RG_WIDTH)
    return pl.pallas_call(
        _proj_kernel,
        grid=(bsz, nblk),
        in_specs=[
            _tok_spec(tm, d),
            _mod_spec(mods, l, is_ctx),
            _fixed_spec((None, None, 1, d), (l, 1, 0, 0)),
            _fixed_spec((None,) + w_in.shape[1:], (l, 0, 0)),
            _fixed_spec((None,) + wup_bd.shape[1:], (l, 0, 0)),
            _fixed_spec((None, 1, bup.shape[-1]), (l, 0, 0)),
        ],
        out_specs=[xr_spec] + [_tok_spec(tm, w) for w, _ in outs],
        out_shape=[jax.ShapeDtypeStruct(xr_shape, F32)]
                  + [jax.ShapeDtypeStruct((bsz, n, w), dt) for w, dt in outs],
        compiler_params=_params(("parallel", "parallel")),
        name="proj",
    )(h, mods, norm_g4, w_in, wup_bd, bup)


def _rg_kernel(x_ref, xp_ref, xn_ref, c0_ref, cw_ref, cb_ref, lam_ref, wg_ref, ba_ref, bi_ref,
               o_ref, cout_ref, carry_ref, *, reverse, nblk):
    seg_len = x_ref.shape[0]
    j = pl.program_id(1)
    blk = (nblk - 1 - j) if reverse else j

    @pl.when(j == 0)
    def _():
        carry_ref[...] = c0_ref[...]

    x = x_ref[...]
    last = SUBLANES - 1
    prev2 = jnp.where(blk == 0, 0.0, xp_ref[last - 1, last:, :])
    prev1 = jnp.where(blk == 0, 0.0, xp_ref[last, last:, :])
    next1 = jnp.where(blk == nblk - 1, 0.0, xn_ref[0, 0:1, :])
    row8 = lax.broadcasted_iota(jnp.int32, (SUBLANES, 1), 0)

    def from_prev_segment(tile, halo):
        return jnp.where(row8 == 0, halo, pltpu.roll(tile, 1, axis=0))

    def from_next_segment(tile, halo):
        return jnp.where(row8 == last, halo, pltpu.roll(tile, last, axis=0))

    xm2 = from_prev_segment(x[seg_len - 2], prev2)
    xm1 = from_prev_segment(x[seg_len - 1], prev1)
    xp1 = from_next_segment(x[0], next1)
    xe = jnp.concatenate([xm2[None], xm1[None], x, xp1[None]], axis=0)
    cw = cw_ref[...]
    xc = xe[0:seg_len] * cw[0:1] + xe[1:seg_len + 1] * cw[1:2]
    xc = xc + xe[2:seg_len + 2] * cw[2:3]
    xc = xc + xe[3:seg_len + 3] * cw[3:4]
    xc = (xc + cb_ref[...]).reshape(seg_len * SUBLANES, RG_WIDTH)

    neg_c_sp = (-RG_C) * _softplus(-lam_ref[...])
    order = range(seg_len - 1, -1, -1) if reverse else range(seg_len)
    for g in range(RG_WIDTH // LANES):
        sl = slice(g * LANES, (g + 1) * LANES)
        xg = xc[:, sl]
        pre = jnp.dot(xg.astype(BF16), wg_ref[g], preferred_element_type=F32)
        r = _sigmoid(pre[:, :LANES] + ba_ref[:, sl])
        i = _sigmoid(pre[:, LANES:] + bi_ref[:, sl])
        log_a = r * neg_c_sp[:, sl]
        a = jnp.exp(log_a)
        bv = jnp.sqrt(1.0 - a * a) * (i * xg)
        hs, ps = [None] * seg_len, [None] * seg_len
        h_run = p_run = None
        for t in order:
            rows = slice(t * SUBLANES, (t + 1) * SUBLANES)
            if h_run is None:
                h_run, p_run = bv[rows], a[rows]
            else:
                h_run = a[rows] * h_run + bv[rows]
                p_run = a[rows] * p_run
            hs[t], ps[t] = h_run, p_run
        pc, hc = p_run, h_run
        s = 1
        while s < SUBLANES:
            shift = SUBLANES - s if reverse else s
            valid = (row8 < SUBLANES - s) if reverse else (row8 >= s)
            hc = jnp.where(valid, pc * pltpu.roll(hc, shift, axis=0) + hc, hc)
            pc = jnp.where(valid, pc * pltpu.roll(pc, shift, axis=0), pc)
            s *= 2
        carry = carry_ref[:, sl]
        seg_end = pc * carry + hc
        if reverse:
            seg_in = jnp.where(row8 == last, carry, pltpu.roll(seg_end, last, axis=0))
            carry_ref[:, sl] = seg_end[0:1]
        else:
            seg_in = jnp.where(row8 == 0, carry, pltpu.roll(seg_end, 1, axis=0))
            carry_ref[:, sl] = seg_end[last:]
        for t in range(seg_len):
            o_ref[t, :, sl] = hs[t] + ps[t] * seg_in

    @pl.when(j == nblk - 1)
    def _():
        cout_ref[...] = carry_ref[...]


def _rg_call(x4, carry0, conv_w, conv_b, lam, wg, ba, bi, *, l, d, reverse):
    bsz, seg_len, nseg, w = x4.shape
    nblk = nseg // SUBLANES
    tiles = seg_len // SUBLANES

    def blk(j):
        return (nblk - 1 - j) if reverse else j

    vec = _fixed_spec((None, None, 1, w), (l, d, 0, 0))
    return pl.pallas_call(
        functools.partial(_rg_kernel, reverse=reverse, nblk=nblk),
        grid=(bsz, nblk),
        in_specs=[
            pl.BlockSpec((None, seg_len, SUBLANES, w), lambda b, j: (b, 0, blk(j), 0)),
            pl.BlockSpec((None, SUBLANES, SUBLANES, w),
                         lambda b, j: (b, tiles - 1, jnp.maximum(blk(j) - 1, 0), 0)),
            pl.BlockSpec((None, SUBLANES, SUBLANES, w),
                         lambda b, j: (b, 0, jnp.minimum(blk(j) + 1, nblk - 1), 0)),
            pl.BlockSpec((None, 1, w), lambda b, j: (b, 0, 0)),
            _fixed_spec((None,) + conv_w.shape[1:], (l, 0, 0)),
            _fixed_spec((None, 1, w), (l, 0, 0)),
            vec,
            _fixed_spec((None, None) + wg.shape[2:], (l, d, 0, 0, 0)),
            vec,
            vec,
        ],
        out_specs=[pl.BlockSpec((None, seg_len, SUBLANES, w), lambda b, j: (b, 0, blk(j), 0)),
                   pl.BlockSpec((None, 1, w), lambda b, j: (b, 0, 0))],
        out_shape=[jax.ShapeDtypeStruct(x4.shape, F32), jax.ShapeDtypeStruct((bsz, 1, w), F32)],
        scratch_shapes=[pltpu.VMEM((1, w), F32)],
        compiler_params=_params(("arbitrary", "arbitrary")),
        name="rg_bwd" if reverse else "rg_fwd",
    )(x4, x4, x4, carry0, conv_w, conv_b, lam, wg, ba, bi)


def _level_ref(b, m, r):
    c, w = b.shape
    if m >= SUBLANES:
        b3 = b.reshape(c // m, m, w)
        return jnp.broadcast_to(b3[:, r:r + 1, :], (c // m, m, w)).reshape(c, w)
    bt = b.reshape(c // SUBLANES, SUBLANES, w)
    sub = lax.broadcasted_iota(jnp.int32, (1, SUBLANES, 1), 1)
    out = None
    for blk in range(SUBLANES // m):
        src = blk * m + r
        piece = jnp.broadcast_to(bt[:, src:src + 1, :], bt.shape)
        out = piece if out is None else jnp.where(sub >= blk * m, piece, out)
    return out.reshape(c, w)


def _halves_select(m, q_rows, q, k, reverse):
    half = m // 2
    if half % SUBLANES:
        return jnp.where(q_rows, q, k)
    first, second = (q, k) if reverse else (k, q)
    pieces = []
    for lo in range(0, q.shape[0], m):
        pieces += [first[lo:lo + half], second[lo + half:lo + m]]
    return jnp.concatenate(pieces, axis=0)


def _dot_nt(a, b):
    return lax.dot_general(a, b, (((1,), (1,)), ((), ())), preferred_element_type=F32)


def _dot_tn(a, b):
    return lax.dot_general(a, b, (((0,), (0,)), ((), ())), preferred_element_type=F32)


def _gla_consts(reverse):
    c = GLA_CHUNK
    row = lax.broadcasted_iota(jnp.int32, (c, 1), 0)
    r2 = lax.broadcasted_iota(jnp.int32, (c, c), 0)
    c2 = lax.broadcasted_iota(jnp.int32, (c, c), 1)
    tri = jnp.where((c2 >= r2) if reverse else (c2 <= r2), 1.0, 0.0).astype(BF16)
    klane = lax.broadcasted_iota(jnp.int32, (1, LANES), 1)
    vlane = lax.broadcasted_iota(jnp.int32, (1, PAIR_VW), 1)
    s = lax.broadcasted_iota(jnp.int32, (2 * c, c), 0) % c
    t = lax.broadcasted_iota(jnp.int32, (2 * c, c), 1)
    levels = []
    m = c
    while m >= 2:
        half = m // 2
        late = (row % m) >= half
        q_rows = ~late if reverse else late
        same = (t // m) == (s // m)
        t_late = (t % m) >= half
        s_late = (s % m) >= half
        mask = same & (~t_late) & s_late if reverse else same & t_late & (~s_late)
        levels.append((m, q_rows, mask))
        m = half
    key_first = klane < GLA_DK
    srow = lax.broadcasted_iota(jnp.int32, (PAIR_VW, 1), 0)
    state_mask = (srow < GLA_DV) == key_first
    return dict(tri=tri, key_first=key_first, val_first=vlane < GLA_DV, state_mask=state_mask,
                levels=levels, diag=(t == s))


def _pair_blockdiag_rows(x, first_mask):
    zero = jnp.zeros_like(x)
    return jnp.concatenate([jnp.where(first_mask, x, zero), jnp.where(first_mask, zero, x)], axis=0)


def _gla_stages(streams):
    c = GLA_CHUNK
    pair_k = [slice(p * LANES, (p + 1) * LANES) for p in range(GLA_PAIRS)]
    pair_v = [slice(p * PAIR_VW, (p + 1) * PAIR_VW) for p in range(GLA_PAIRS)]

    def cumulative_decay():
        for s in streams:
            g, tri = s["g"], s["consts"]["tri"]
            g1 = g.astype(BF16)
            g2 = (g - g1.astype(F32)).astype(BF16)
            g3 = (g - g1.astype(F32) - g2.astype(F32)).astype(BF16)
            s["b"] = (jnp.dot(tri, g1, preferred_element_type=F32)
                      + jnp.dot(tri, g2, preferred_element_type=F32)
                      + jnp.dot(tri, g3, preferred_element_type=F32))

    def diagonal():
        for s in streams:
            kf = s["consts"]["key_first"]
            qb, kb = s["q"].astype(BF16), s["k"].astype(BF16)
            s["att"] = [jnp.where(s["consts"]["diag"],
                                  _dot_nt(_pair_blockdiag_rows(kb[:, kl], kf), qb[:, kl]), 0.0) for kl in pair_k]

    def level(lvl):
        for s in streams:
            m, q_rows, mask = s["consts"]["levels"][lvl]
            half = m // 2
            if m == 2:
                x = jnp.where(q_rows, s["q"] * jnp.exp2(s["g"]), s["k"])
            else:
                ref = _level_ref(s["b"], m, half if s["reverse"] else half - 1)
                x = jnp.exp2(-jnp.abs(s["b"] - ref)) * _halves_select(m, q_rows, s["q"], s["k"], s["reverse"])
            x = x.astype(BF16)
            kf = s["consts"]["key_first"]
            s["att"] = [jnp.where(mask, _dot_nt(_pair_blockdiag_rows(x[:, kl], kf), x[:, kl]), att)
                        for kl, att in zip(pair_k, s["att"])]

    def state_operands():
        for s in streams:
            b = s["b"]
            b_last = b[0:1] if s["reverse"] else b[c - 1:c]
            s["q_in"] = (s["q"] * jnp.exp2(b)).astype(BF16)
            s["k_hat"] = (s["k"] * jnp.exp2(b_last - b)).astype(BF16)
            s["decay"] = jnp.exp2(b_last)
            s["att"] = [att.astype(BF16) for att in s["att"]]

    def output_and_update(s):
        vf = s["consts"]["val_first"]
        for p, (kl, vl) in enumerate(zip(pair_k, pair_v)):
            vp = s["v"][:, vl]
            st = s["st_ref"][p]
            o = _dot_tn(s["att"][p], _pair_blockdiag_rows(vp, vf))
            o = o + _dot_nt(s["q_in"][:, kl], st.astype(BF16))
            s["o_ref"][:, vl] = o
            upd = _dot_tn(vp, s["k_hat"][:, kl])
            s["st_ref"][p] = st * s["decay"][:, kl] + jnp.where(s["consts"]["state_mask"], upd, 0.0)

    n_levels = len(streams[0]["consts"]["levels"])
    prep = [cumulative_decay, diagonal] + [functools.partial(level, lvl) for lvl in range(n_levels)]
    prep.append(state_operands)
    tail = [functools.partial(output_and_update, s) for s in streams]
    return prep, tail


def _interleave(first, second):
    done = 0
    for idx, thunk in enumerate(first):
        thunk()
        upto = (len(second) * (idx + 1)) // len(first)
        for extra in second[done:upto]:
            extra()
        done = upto


def _gla_kernel(qf_ref, kf_ref, vf_ref, gf_ref, qb_ref, kb_ref, vb_ref, gb_ref, s0_ref,
                of_ref, ob_ref, sout_ref, st_ref, *, chunks):
    @pl.when(pl.program_id(0) == 0)
    def _():
        st_ref[...] = s0_ref[...]

    bsz = qf_ref.shape[0]
    dirs = ((0, qf_ref, kf_ref, vf_ref, gf_ref, of_ref), (1, qb_ref, kb_ref, vb_ref, gb_ref, ob_ref))
    consts = [_gla_consts(reverse=False), _gla_consts(reverse=True)]

    def chunk_stages(ci):
        streams = []
        for d, q_ref, k_ref, v_ref, g_ref, o_ref in dirs:
            rows = pl.ds(((chunks - 1 - ci) if d else ci) * GLA_CHUNK, GLA_CHUNK)
            for bi in range(bsz):
                streams.append(dict(q=q_ref[bi, rows, :], k=k_ref[bi, rows, :], v=v_ref[bi, rows, :],
                                    g=g_ref[bi, rows, :], st_ref=st_ref.at[d, bi], o_ref=o_ref.at[bi, rows],
                                    consts=consts[d], reverse=bool(d)))
        return _gla_stages(streams)

    prep, tail = chunk_stages(0)
    for thunk in prep:
        thunk()
    for ci in range(chunks):
        if ci + 1 < chunks:
            next_prep, next_tail = chunk_stages(ci + 1)
            _interleave(next_prep, tail)
            tail = next_tail
        else:
            for thunk in tail:
                thunk()

    @pl.when(pl.program_id(0) == pl.num_programs(0) - 1)
    def _():
        sout_ref[...] = st_ref[...]


def _gla_call(q, k, v, lg, state0):
    bsz, n, kw = q.shape
    vw = v.shape[2]
    chunks = min(GLA_CHUNKS_PER_STEP, n // GLA_CHUNK)
    br = chunks * GLA_CHUNK
    nstep = n // br

    def fwd(i):
        return (0, i, 0)

    def bwd(i):
        return (0, nstep - 1 - i, 0)

    return pl.pallas_call(
        functools.partial(_gla_kernel, chunks=chunks),
        grid=(nstep,),
        in_specs=[
            pl.BlockSpec((bsz, br, kw), fwd),
            pl.BlockSpec((bsz, br, kw), fwd),
            pl.BlockSpec((bsz, br, vw), fwd),
            pl.BlockSpec((bsz, br, kw), fwd),
            pl.BlockSpec((bsz, br, kw), bwd),
            pl.BlockSpec((bsz, br, kw), bwd),
            pl.BlockSpec((bsz, br, vw), bwd),
            pl.BlockSpec((bsz, br, kw), lambda i: (0, nstep - 1 - i, 1)),
            _full_spec(state0),
        ],
        out_specs=[pl.BlockSpec((bsz, br, vw), fwd), pl.BlockSpec((bsz, br, vw), bwd),
                   pl.BlockSpec(state0.shape, lambda i: (0,) * state0.ndim)],
        out_shape=[jax.ShapeDtypeStruct((bsz, n, vw), F32)] * 2
                  + [jax.ShapeDtypeStruct(state0.shape, F32)],
        scratch_shapes=[pltpu.VMEM(state0.shape, F32)],
        compiler_params=_params(("arbitrary",)),
        name="gla",
    )(q, k, v, lg, q, k, v, lg, state0)


def _pair_blockdiag(w):
    hd = w.shape[-1]
    per = LANES // hd
    lead = w.shape[:-3]
    wg = w.reshape(lead + (RG_HEADS // per, per, hd, hd))
    eye = jnp.eye(per, dtype=w.dtype)
    return jnp.einsum('...gpij,pq->...gpiqj', wg, eye).reshape(lead + (RG_HEADS // per, LANES, LANES))


def kernel(x, c, ctx, c_ctx, w_mod, b_mod, norm_g, ffn_w1, ffn_w3, ffn_w2, w_in, conv_w, conv_b,
           rg_lam, rg_wa, rg_ba, rg_wi, rg_bi, gla_wup, gla_bup, gla_norm_g, w_out, final_g):
    bsz, t, d = x.shape
    n_ctx = ctx.shape[1]
    depth = w_mod.shape[0]
    assert t % TOKEN_BLOCK == 0 and t % FFN_TOKENS == 0 and t % (GRID_W * SUBLANES) == 0
    assert n_ctx % (SUBLANES * SUBLANES) == 0 and n_ctx % GLA_CHUNK == 0 and bsz * n_ctx <= TOKEN_BLOCK

    rows = -(-(bsz + 1) // SUBLANES) * SUBLANES
    cvec = jnp.zeros((rows, d), F32).at[0].set(c_ctx).at[1:bsz + 1].set(c)
    mods = _mod_call(cvec, w_mod, b_mod).reshape(depth, rows, N_MOD, d)

    w1, w3, w2 = ffn_w1.astype(BF16), ffn_w3.astype(BF16), ffn_w2.astype(BF16)
    w_in_b, w_out_b = w_in.astype(BF16), w_out.astype(BF16)
    norm_g4 = norm_g.reshape(depth, 3, 1, d)
    zero = jnp.zeros((depth, GLA_RANK, GLA_KW), F32)
    wup_bd = jnp.concatenate([jnp.concatenate([gla_wup[:, 0], zero], axis=2),
                              jnp.concatenate([zero, gla_wup[:, 1]], axis=2)], axis=1).astype(BF16)
    bup = gla_bup.reshape(depth, 1, 2 * GLA_KW)
    wg = jnp.concatenate([_pair_blockdiag(rg_wa), _pair_blockdiag(rg_wi)], axis=-1).astype(BF16)
    conv_b3 = conv_b.reshape(depth, 1, RG_WIDTH)
    lam4 = rg_lam.reshape(depth, 2, 1, RG_WIDTH)
    ba4 = rg_ba.reshape(depth, 2, 1, RG_WIDTH)
    bi4 = rg_bi.reshape(depth, 2, 1, RG_WIDTH)
    gnorm = gla_norm_g.reshape(depth, 1, GLA_VW)
    fg = final_g.reshape(1, d)

    h_c = ctx.reshape(1, bsz * n_ctx, d)
    grid_rows = t // GRID_W
    h_l = x.reshape(bsz, grid_rows, GRID_W, d).swapaxes(1, 2).reshape(bsz, t, d)
    ctx_seg = n_ctx // SUBLANES
    for l in range(depth):
        last = l == depth - 1
        ffn = functools.partial(_ffn_call, mods=mods, norm_g4=norm_g4, w1=w1, w3=w3, w2=w2, final_g=fg, l=l)
        h_l = ffn(h_l, half=0, is_ctx=False, final=False)
        h_c = ffn(h_c, half=0, is_ctx=True, final=False)

        proj = functools.partial(_proj_call, mods=mods, norm_g4=norm_g4, w_in=w_in_b, wup_bd=wup_bd,
                                 bup=bup, l=l)
        xr_l, gr_l, q_l, k_l, v_l, og_l, lg_l = proj(h_l, is_ctx=False, seqs_per_block=0)
        xr_c, gr_c, q_c, k_c, v_c, og_c, lg_c = proj(h_c, is_ctx=True, seqs_per_block=bsz)

        scans_l, scans_c = [], []
        for dr in range(2):
            rg = functools.partial(_rg_call, conv_w=conv_w, conv_b=conv_b3, lam=lam4, wg=wg, ba=ba4,
                                   bi=bi4, l=l, d=dr, reverse=bool(dr))
            s_c, carry = rg(xr_c.reshape(bsz, ctx_seg, SUBLANES, RG_WIDTH), jnp.zeros((bsz, 1, RG_WIDTH), F32))
            s_l, _ = rg(xr_l.reshape(bsz, GRID_W, grid_rows, RG_WIDTH), carry)
            scans_c.append(s_c.reshape(xr_c.shape))
            scans_l.append(s_l.reshape(xr_l.shape))

        per_seq = lambda a: a.reshape(bsz, n_ctx, a.shape[-1])
        state0 = jnp.zeros((2, bsz, GLA_PAIRS, PAIR_VW, LANES), F32)
        oc_f, oc_b, state = _gla_call(per_seq(q_c), per_seq(k_c), per_seq(v_c), per_seq(lg_c), state0)
        ol_f, ol_b, _ = _gla_call(q_l, k_l, v_l, lg_l, state)

        mixer_l = dict(gr=gr_l, hf=scans_l[0], hb=scans_l[1], o_f=ol_f, o_b=ol_b, og=og_l, gnorm=gnorm,
                       w_out=w_out_b, seqs_per_block=0)
        h_l = ffn(h_l, half=1, is_ctx=False, final=last, mixer=mixer_l)
        if not last:
            flat = lambda a: a.reshape(1, bsz * n_ctx, a.shape[-1])
            mixer_c = dict(gr=gr_c, hf=scans_c[0], hb=scans_c[1], o_f=flat(oc_f), o_b=flat(oc_b), og=og_c,
                           gnorm=gnorm, w_out=w_out_b, seqs_per_block=bsz)
            h_c = ffn(h_c, half=1, is_ctx=True, final=False, mixer=mixer_c)
    return h_l.reshape(bsz, GRID_W, grid_rows, d).swapaxes(1, 2).reshape(bsz, t, d)
```

```python
import functools
import math

import jax
import jax.numpy as jnp
from jax import lax
from jax.experimental import pallas as pl
from jax.experimental.pallas import tpu as pltpu

F32 = jnp.float32
BF16 = jnp.bfloat16

LANES = 128
SUBLANES = 8
FFN_TOKENS = 512
TOKEN_BLOCK = 1024
SUB_ROWS = 256
FF_STEP = 1024
GLA_CHUNKS_PER_STEP = 8
VMEM_LIMIT = 56 * 1024 * 1024

N_MOD = 9
EPS = 1e-6
GRID_W = 64
RG_WIDTH = 512
RG_HEADS = 8
RG_C = 8.0
GLA_HEADS = 4
GLA_DK = 64
GLA_DV = 128
GLA_KW = GLA_HEADS * GLA_DK
GLA_VW = GLA_HEADS * GLA_DV
GLA_RANK = 16
GLA_GATE_NORM = 16.0
GLA_CHUNK = 64
GLA_PAIRS = GLA_KW // LANES
PAIR_VW = GLA_VW // GLA_PAIRS
LOG2E = math.log2(math.e)


def _params(sem):
    return pltpu.CompilerParams(dimension_semantics=sem, vmem_limit_bytes=VMEM_LIMIT)


def _fixed_spec(shape, index):
    return pl.BlockSpec(shape, lambda *_: index, pipeline_mode=pl.Buffered(1))


def _full_spec(arr):
    return _fixed_spec(arr.shape, (0,) * arr.ndim)


def _sigmoid(x):
    return 1.0 / (1.0 + jnp.exp(-x))


def _softplus(x):
    return jnp.maximum(x, 0.0) + jnp.log(1.0 + jnp.exp(-jnp.abs(x)))


def _ada_norm(x, m, g, j):
    ms = jnp.mean(x * x, axis=-1, keepdims=True)
    gain = g * (1.0 + m[3 * j + 1:3 * j + 2])
    return x * lax.rsqrt(ms + EPS) * gain + m[3 * j:3 * j + 1]


def _token_block(n, target):
    return min(n, target)


def _row_blocks(n_rows):
    step = min(SUB_ROWS, n_rows)
    return [slice(r, r + step) for r in range(0, n_rows, step)]


def _staged(blocks, prologue, matmul, epilogue):
    operand = prologue(blocks[0])
    pending = None
    for i, rows in enumerate(blocks):
        acc = matmul(operand)
        if i + 1 < len(blocks):
            operand = prologue(blocks[i + 1])
        if pending is not None:
            epilogue(*pending)
        pending = (rows, acc)
    epilogue(*pending)


def _mod_kernel(c_ref, w_ref, b_ref, o_ref):
    c = c_ref[...]
    sc = (c * _sigmoid(c)).astype(BF16)
    o_ref[...] = jnp.dot(sc, w_ref[...].astype(BF16), preferred_element_type=F32) + b_ref[...]


def _mod_call(cvec, w_mod, b_mod):
    depth, d, n = w_mod.shape
    rows = cvec.shape[0]
    tn = n // 8
    return pl.pallas_call(
        _mod_kernel,
        grid=(depth, n // tn),
        in_specs=[
            pl.BlockSpec((rows, d), lambda l, i: (0, 0)),
            pl.BlockSpec((None, d, tn), lambda l, i: (l, 0, i)),
            pl.BlockSpec((None, 1, tn), lambda l, i: (l, 0, i)),
        ],
        out_specs=pl.BlockSpec((None, rows, tn), lambda l, i: (l, 0, i)),
        out_shape=jax.ShapeDtypeStruct((depth, rows, n), F32),
        compiler_params=_params(("parallel", "parallel")),
        name="mod",
    )(cvec, w_mod, b_mod.reshape(depth, 1, n))


def _mod_spec(mods, l, is_ctx):
    d = mods.shape[-1]
    if is_ctx:
        return pl.BlockSpec((None, None, N_MOD, d), lambda b, i: (l, 0, 0, 0))
    return pl.BlockSpec((None, None, N_MOD, d), lambda b, i: (l, b + 1, 0, 0))


def _tok_spec(tm, w):
    return pl.BlockSpec((None, tm, w), lambda b, i: (b, i, 0))


def _store_interleaved(ref, val):
    nb, seg_len, _ = ref.shape
    w = val.shape[1]
    for bb in range(nb):
        for s in range(SUBLANES):
            r0 = (bb * SUBLANES + s) * seg_len
            ref[bb, :, s * w:(s + 1) * w] = val[r0:r0 + seg_len]


def _load_interleaved(ref):
    nb, _, sw = ref.shape
    w = sw // SUBLANES
    return jnp.concatenate([ref[bb, :, s * w:(s + 1) * w]
                            for bb in range(nb) for s in range(SUBLANES)], axis=0)


def _interleaved_spec(nb, seg_len, w, nblk):
    return pl.BlockSpec((nb, seg_len, SUBLANES * w), lambda b, i: (b * nblk + i, 0, 0))


def _mixer_rows(rows, gr_ref, hf_ref, hb_ref, of_ref, ob_ref, og_ref, gn):
    gr = gr_ref[rows, :]
    gelu = 0.5 * gr * (1.0 + jnp.tanh(0.7978845608028654 * (gr + 0.044715 * (gr * gr * gr))))
    if len(hf_ref.shape) == 3:
        rg = gelu * (_load_interleaved(hf_ref) + _load_interleaved(hb_ref))
    else:
        rg = gelu * (hf_ref[rows, :] + hb_ref[rows, :])
    o = of_ref[rows, :] + ob_ref[rows, :]
    og = og_ref[rows, :]
    parts = []
    for hd in range(GLA_HEADS):
        sl = slice(hd * GLA_DV, (hd + 1) * GLA_DV)
        oh = o[:, sl]
        ms = jnp.mean(oh * oh, axis=-1, keepdims=True)
        parts.append(oh * lax.rsqrt(ms + EPS) * gn[:, sl])
    gla = jnp.concatenate(parts, axis=-1) * (og * _sigmoid(og))
    return jnp.concatenate([rg, gla], axis=-1).astype(BF16)


def _ffn_kernel(*refs, j, final, ff_chunks, with_mixer):
    if with_mixer:
        (x_ref, m_ref, g_ref, gr_ref, hf_ref, hb_ref, of_ref, ob_ref, og_ref, gn_ref, wo_ref,
         w1_ref, w3_ref, w2_ref, fg_ref, o_ref) = refs
        whole = len(hf_ref.shape) == 3
    else:
        x_ref, m_ref, g_ref, w1_ref, w3_ref, w2_ref, fg_ref, o_ref = refs
        whole = False
    m = m_ref[...]
    n_rows = x_ref.shape[0]

    def prologue(rows):
        x = x_ref[rows, :]
        if with_mixer:
            mix = _mixer_rows(rows, gr_ref, hf_ref, hb_ref, of_ref, ob_ref, og_ref, gn_ref[...])
            x = x + m[5:6] * jnp.dot(mix, wo_ref[...], preferred_element_type=F32)
        return x, _ada_norm(x, m, g_ref[...], j).astype(BF16)

    def matmul(operand):
        x, u = operand
        y = None
        for lo, hi in ff_chunks:
            h1 = jnp.dot(u, w1_ref[:, lo:hi], preferred_element_type=F32)
            h3 = jnp.dot(u, w3_ref[:, lo:hi], preferred_element_type=F32)
            act = (h1 * _sigmoid(h1) * h3).astype(BF16)
            part = jnp.dot(act, w2_ref[lo:hi, :], preferred_element_type=F32)
            y = part if y is None else y + part
        return x, y

    def epilogue(rows, acc):
        x, y = acc
        out = x + (0.5 * m[3 * j + 2:3 * j + 3]) * y
        if final:
            ms = jnp.mean(out * out, axis=-1, keepdims=True)
            out = out * lax.rsqrt(ms + EPS) * fg_ref[...]
        o_ref[rows, :] = out

    _staged([slice(0, n_rows)] if whole else _row_blocks(n_rows), prologue, matmul, epilogue)


def _ffn_call(h, mods, norm_g4, w1, w3, w2, final_g, *, l, half, is_ctx, final, mixer=None):
    bsz, n, d = h.shape
    dff = w1.shape[-1]
    tm = _token_block(n, FFN_TOKENS)
    nblk = n // tm
    j = 2 * half
    ff_chunks = tuple((lo, min(lo + FF_STEP, dff)) for lo in range(0, dff, FF_STEP))
    operands = [h, mods, norm_g4]
    in_specs = [_tok_spec(tm, d), _mod_spec(mods, l, is_ctx), _fixed_spec((None, None, 1, d), (l, j, 0, 0))]
    if mixer is not None:
        spb = mixer["seqs_per_block"]
        if spb:
            scan_spec = _interleaved_spec(spb, tm // (spb * SUBLANES), RG_WIDTH, nblk)
        else:
            scan_spec = _tok_spec(tm, RG_WIDTH)
        operands += [mixer[name] for name in ("gr", "hf", "hb", "o_f", "o_b", "og", "gnorm", "w_out")]
        in_specs += [_tok_spec(tm, RG_WIDTH), scan_spec, scan_spec,
                     _tok_spec(tm, GLA_VW), _tok_spec(tm, GLA_VW), _tok_spec(tm, GLA_VW),
                     _fixed_spec((None, 1, GLA_VW), (l, 0, 0)),
                     _fixed_spec((None,) + mixer["w_out"].shape[1:], (l, 0, 0))]
    operands += [w1, w3, w2, final_g]
    in_specs += [_fixed_spec((None, None, d, dff), (l, half, 0, 0)),
                 _fixed_spec((None, None, d, dff), (l, half, 0, 0)),
                 _fixed_spec((None, None, dff, d), (l, half, 0, 0)),
                 _full_spec(final_g)]
    return pl.pallas_call(
        functools.partial(_ffn_kernel, j=j, final=final, ff_chunks=ff_chunks, with_mixer=mixer is not None),
        grid=(bsz, nblk),
        in_specs=in_specs,
        out_specs=_tok_spec(tm, d),
        out_shape=jax.ShapeDtypeStruct((bsz, n, d), F32),
        compiler_params=_params(("parallel", "parallel")),
        name="mix_ffn" if mixer is not None else "ffn",
    )(*operands)


def _proj_kernel(x_ref, m_ref, g_ref, w_ref, wup_ref, bup_ref,
                 xr_ref, gr_ref, q_ref, k_ref, v_ref, og_ref, lg_ref):
    interleaved = len(xr_ref.shape) == 3
    m = m_ref[...]

    def cols(u, lo, width):
        return jnp.dot(u, w_ref[:, lo:lo + width], preferred_element_type=F32)

    def project(rows, u):
        lr = cols(u, 2 * RG_WIDTH + 2 * GLA_KW + 2 * GLA_VW, 2 * GLA_RANK).astype(BF16)
        z = jnp.dot(lr, wup_ref[...], preferred_element_type=F32) + bup_ref[...]
        lg_ref[rows, :] = _softplus(-z) * (-LOG2E / GLA_GATE_NORM)
        o = 0
        xr = cols(u, o, RG_WIDTH); o += RG_WIDTH
        if interleaved:
            _store_interleaved(xr_ref, xr)
        else:
            xr_ref[rows, :] = xr
        gr_ref[rows, :] = cols(u, o, RG_WIDTH); o += RG_WIDTH
        q_ref[rows, :] = cols(u, o, GLA_KW) * (GLA_DK ** -0.5); o += GLA_KW
        k_ref[rows, :] = cols(u, o, GLA_KW); o += GLA_KW
        v_ref[rows, :] = cols(u, o, GLA_VW).astype(BF16); o += GLA_VW
        og_ref[rows, :] = cols(u, o, GLA_VW)

    n_rows = x_ref.shape[0]
    blocks = [slice(0, n_rows)] if interleaved else _row_blocks(n_rows)
    u = _ada_norm(x_ref[blocks[0], :], m, g_ref[...], 1).astype(BF16)
    for i, rows in enumerate(blocks):
        u_next = None
        if i + 1 < len(blocks):
            u_next = _ada_norm(x_ref[blocks[i + 1], :], m, g_ref[...], 1).astype(BF16)
        project(rows, u)
        u = u_next


def _proj_call(h, mods, norm_g4, w_in, wup_bd, bup, *, l, is_ctx, seqs_per_block):
    bsz, n, d = h.shape
    tm = _token_block(n, TOKEN_BLOCK)
    nblk = n // tm
    outs = ((RG_WIDTH, F32), (GLA_KW, F32), (GLA_KW, F32), (GLA_VW, BF16), (GLA_VW, F32), (2 * GLA_KW, F32))
    if seqs_per_block:
        seg_len = tm // (seqs_per_block * SUBLANES)
        xr_shape = (bsz * nblk * seqs_per_block, seg_len, SUBLANES * RG_WIDTH)
        xr_spec = _interleaved_spec(seqs_per_block, seg_len, RG_WIDTH, nblk)
    else:
        xr_shape = (bsz, n, RG_WIDTH)
        xr_spec = _tok_spec(tm, RG_WIDTH)
    return pl.pallas_call(
        _proj_kernel,
        grid=(bsz, nblk),
        in_specs=[
            _tok_spec(tm, d),
            _mod_spec(mods, l, is_ctx),
            _fixed_spec((None, None, 1, d), (l, 1, 0, 0)),
            _fixed_spec((None,) + w_in.shape[1:], (l, 0, 0)),
            _fixed_spec((None,) + wup_bd.shape[1:], (l, 0, 0)),
            _fixed_spec((None, 1, bup.shape[-1]), (l, 0, 0)),
        ],
        out_specs=[xr_spec] + [_tok_spec(tm, w) for w, _ in outs],
        out_shape=[jax.ShapeDtypeStruct(xr_shape, F32)]
                  + [jax.ShapeDtypeStruct((bsz, n, w), dt) for w, dt in outs],
        compiler_params=_params(("parallel", "parallel")),
        name="proj",
    )(h, mods, norm_g4, w_in, wup_bd, bup)


def _rg_kernel(xf_ref, xfp_ref, xfn_ref, xb_ref, xbp_ref, xbn_ref, c0_ref, cw_ref, cb_ref, lam_ref, wg_ref,
               ba_ref, bi_ref, of_ref, ob_ref, cout_ref, carry_ref, *, nblk):
    for d, (x_ref, xp_ref, xn_ref, o_ref) in enumerate(((xf_ref, xfp_ref, xfn_ref, of_ref),
                                                         (xb_ref, xbp_ref, xbn_ref, ob_ref))):
        _rg_direction(x_ref, xp_ref, xn_ref, c0_ref.at[d], cw_ref, cb_ref, lam_ref.at[d], wg_ref.at[d],
                      ba_ref.at[d], bi_ref.at[d], o_ref, cout_ref.at[d], carry_ref.at[d],
                      reverse=bool(d), nblk=nblk)


def _rg_direction(x_ref, xp_ref, xn_ref, c0_ref, cw_ref, cb_ref, lam_ref, wg_ref, ba_ref, bi_ref,
                  o_ref, cout_ref, carry_ref, *, reverse, nblk):
    seg_len = x_ref.shape[0]
    j = pl.program_id(1)
    blk = (nblk - 1 - j) if reverse else j

    @pl.when(j == 0)
    def _():
        carry_ref[...] = c0_ref[...]

    x = x_ref[...]
    last = SUBLANES - 1
    prev2 = jnp.where(blk == 0, 0.0, xp_ref[last - 1, last:, :])
    prev1 = jnp.where(blk == 0, 0.0, xp_ref[last, last:, :])
    next1 = jnp.where(blk == nblk - 1, 0.0, xn_ref[0, 0:1, :])
    row8 = lax.broadcasted_iota(jnp.int32, (SUBLANES, 1), 0)

    def from_prev_segment(tile, halo):
        return jnp.where(row8 == 0, halo, pltpu.roll(tile, 1, axis=0))

    def from_next_segment(tile, halo):
        return jnp.where(row8 == last, halo, pltpu.roll(tile, last, axis=0))

    xm2 = from_prev_segment(x[seg_len - 2], prev2)
    xm1 = from_prev_segment(x[seg_len - 1], prev1)
    xp1 = from_next_segment(x[0], next1)
    xe = jnp.concatenate([xm2[None], xm1[None], x, xp1[None]], axis=0)
    cw = cw_ref[...]
    xc = xe[0:seg_len] * cw[0:1] + xe[1:seg_len + 1] * cw[1:2]
    xc = xc + xe[2:seg_len + 2] * cw[2:3]
    xc = xc + xe[3:seg_len + 3] * cw[3:4]
    xc = (xc + cb_ref[...]).reshape(seg_len * SUBLANES, RG_WIDTH)

    neg_c_sp = (-RG_C) * _softplus(-lam_ref[...])
    order = range(seg_len - 1, -1, -1) if reverse else range(seg_len)
    for g in range(RG_WIDTH // LANES):
        sl = slice(g * LANES, (g + 1) * LANES)
        xg = xc[:, sl]
        pre = jnp.dot(xg.astype(BF16), wg_ref[g], preferred_element_type=F32)
        r = _sigmoid(pre[:, :LANES] + ba_ref[:, sl])
        i = _sigmoid(pre[:, LANES:] + bi_ref[:, sl])
        log_a = r * neg_c_sp[:, sl]
        a = jnp.exp(log_a)
        bv = jnp.sqrt(1.0 - a * a) * (i * xg)
        hs, ps = [None] * seg_len, [None] * seg_len
        h_run = p_run = None
        for t in order:
            rows = slice(t * SUBLANES, (t + 1) * SUBLANES)
            if h_run is None:
                h_run, p_run = bv[rows], a[rows]
            else:
                h_run = a[rows] * h_run + bv[rows]
                p_run = a[rows] * p_run
            hs[t], ps[t] = h_run, p_run
        pc, hc = p_run, h_run
        s = 1
        while s < SUBLANES:
            shift = SUBLANES - s if reverse else s
            valid = (row8 < SUBLANES - s) if reverse else (row8 >= s)
            hc = jnp.where(valid, pc * pltpu.roll(hc, shift, axis=0) + hc, hc)
            pc = jnp.where(valid, pc * pltpu.roll(pc, shift, axis=0), pc)
            s *= 2
        carry = carry_ref[:, sl]
        seg_end = pc * carry + hc
        if reverse:
            seg_in = jnp.where(row8 == last, carry, pltpu.roll(seg_end, last, axis=0))
            carry_ref[:, sl] = seg_end[0:1]
        else:
            seg_in = jnp.where(row8 == 0, carry, pltpu.roll(seg_end, 1, axis=0))
            carry_ref[:, sl] = seg_end[last:]
        for t in range(seg_len):
            o_ref[t, :, sl] = hs[t] + ps[t] * seg_in

    @pl.when(j == nblk - 1)
    def _():
        cout_ref[...] = carry_ref[...]


def _rg_call(x4, carry0, conv_w, conv_b, lam, wg, ba, bi, *, l):
    bsz, seg_len, nseg, w = x4.shape
    nblk = nseg // SUBLANES
    tiles = seg_len // SUBLANES

    def block_specs(blk):
        return [
            pl.BlockSpec((None, seg_len, SUBLANES, w), lambda b, j: (b, 0, blk(j), 0)),
            pl.BlockSpec((None, SUBLANES, SUBLANES, w),
                         lambda b, j: (b, tiles - 1, jnp.maximum(blk(j) - 1, 0), 0)),
            pl.BlockSpec((None, SUBLANES, SUBLANES, w),
                         lambda b, j: (b, 0, jnp.minimum(blk(j) + 1, nblk - 1), 0)),
        ]

    fwd = block_specs(lambda j: j)
    bwd = block_specs(lambda j: nblk - 1 - j)
    both_dirs = _fixed_spec((None, 2, 1, w), (l, 0, 0, 0))
    state_spec = pl.BlockSpec((2, None, 1, w), lambda b, j: (0, b, 0, 0))
    return pl.pallas_call(
        functools.partial(_rg_kernel, nblk=nblk),
        grid=(bsz, nblk),
        in_specs=fwd + bwd + [
            state_spec,
            _fixed_spec((None,) + conv_w.shape[1:], (l, 0, 0)),
            _fixed_spec((None, 1, w), (l, 0, 0)),
            both_dirs,
            _fixed_spec((None,) + wg.shape[1:], (l, 0, 0, 0, 0)),
            both_dirs,
            both_dirs,
        ],
        out_specs=[fwd[0], bwd[0], state_spec],
        out_shape=[jax.ShapeDtypeStruct(x4.shape, F32)] * 2 + [jax.ShapeDtypeStruct(carry0.shape, F32)],
        scratch_shapes=[pltpu.VMEM((2, 1, w), F32)],
        compiler_params=_params(("arbitrary", "arbitrary")),
        name="rg",
    )(x4, x4, x4, x4, x4, x4, carry0, conv_w, conv_b, lam, wg, ba, bi)


def _level_ref(b, m, r):
    c, w = b.shape
    if m >= SUBLANES:
        b3 = b.reshape(c // m, m, w)
        return jnp.broadcast_to(b3[:, r:r + 1, :], (c // m, m, w)).reshape(c, w)
    bt = b.reshape(c // SUBLANES, SUBLANES, w)
    sub = lax.broadcasted_iota(jnp.int32, (1, SUBLANES, 1), 1)
    out = None
    for blk in range(SUBLANES // m):
        src = blk * m + r
        piece = jnp.broadcast_to(bt[:, src:src + 1, :], bt.shape)
        out = piece if out is None else jnp.where(sub >= blk * m, piece, out)
    return out.reshape(c, w)


def _halves_select(m, q_rows, q, k, reverse):
    half = m // 2
    if half % SUBLANES:
        return jnp.where(q_rows, q, k)
    first, second = (q, k) if reverse else (k, q)
    pieces = []
    for lo in range(0, q.shape[0], m):
        pieces += [first[lo:lo + half], second[lo + half:lo + m]]
    return jnp.concatenate(pieces, axis=0)


def _dot_nt(a, b):
    return lax.dot_general(a, b, (((1,), (1,)), ((), ())), preferred_element_type=F32)


def _dot_tn(a, b):
    return lax.dot_general(a, b, (((0,), (0,)), ((), ())), preferred_element_type=F32)


def _gla_consts(reverse):
    c = GLA_CHUNK
    row = lax.broadcasted_iota(jnp.int32, (c, 1), 0)
    r2 = lax.broadcasted_iota(jnp.int32, (c, c), 0)
    c2 = lax.broadcasted_iota(jnp.int32, (c, c), 1)
    tri = jnp.where((c2 >= r2) if reverse else (c2 <= r2), 1.0, 0.0).astype(BF16)
    klane = lax.broadcasted_iota(jnp.int32, (1, LANES), 1)
    vlane = lax.broadcasted_iota(jnp.int32, (1, PAIR_VW), 1)
    s = lax.broadcasted_iota(jnp.int32, (2 * c, c), 0) % c
    t = lax.broadcasted_iota(jnp.int32, (2 * c, c), 1)
    levels = []
    m = c
    while m >= 2:
        half = m // 2
        late = (row % m) >= half
        q_rows = ~late if reverse else late
        same = (t // m) == (s // m)
        t_late = (t % m) >= half
        s_late = (s % m) >= half
        mask = same & (~t_late) & s_late if reverse else same & t_late & (~s_late)
        levels.append((m, q_rows, mask))
        m = half
    key_first = klane < GLA_DK
    srow = lax.broadcasted_iota(jnp.int32, (PAIR_VW, 1), 0)
    state_mask = (srow < GLA_DV) == key_first
    return dict(tri=tri, key_first=key_first, val_first=vlane < GLA_DV, state_mask=state_mask,
                levels=levels, diag=(t == s))


def _pair_blockdiag_rows(x, first_mask):
    zero = jnp.zeros_like(x)
    return jnp.concatenate([jnp.where(first_mask, x, zero), jnp.where(first_mask, zero, x)], axis=0)


def _gla_stages(streams):
    c = GLA_CHUNK
    pair_k = [slice(p * LANES, (p + 1) * LANES) for p in range(GLA_PAIRS)]
    pair_v = [slice(p * PAIR_VW, (p + 1) * PAIR_VW) for p in range(GLA_PAIRS)]

    def cumulative_decay():
        for s in streams:
            g, tri = s["g"], s["consts"]["tri"]
            g1 = g.astype(BF16)
            g2 = (g - g1.astype(F32)).astype(BF16)
            g3 = (g - g1.astype(F32) - g2.astype(F32)).astype(BF16)
            s["b"] = (jnp.dot(tri, g1, preferred_element_type=F32)
                      + jnp.dot(tri, g2, preferred_element_type=F32)
                      + jnp.dot(tri, g3, preferred_element_type=F32))

    def diagonal():
        for s in streams:
            kf = s["consts"]["key_first"]
            qb, kb = s["q"].astype(BF16), s["k"].astype(BF16)
            s["att"] = [jnp.where(s["consts"]["diag"],
                                  _dot_nt(_pair_blockdiag_rows(kb[:, kl], kf), qb[:, kl]), 0.0) for kl in pair_k]

    def level(lvl):
        for s in streams:
            m, q_rows, mask = s["consts"]["levels"][lvl]
            half = m // 2
            if m == 2:
                x = jnp.where(q_rows, s["q"] * jnp.exp2(s["g"]), s["k"])
            else:
                ref = _level_ref(s["b"], m, half if s["reverse"] else half - 1)
                x = jnp.exp2(-jnp.abs(s["b"] - ref)) * _halves_select(m, q_rows, s["q"], s["k"], s["reverse"])
            x = x.astype(BF16)
            kf = s["consts"]["key_first"]
            s["att"] = [jnp.where(mask, _dot_nt(_pair_blockdiag_rows(x[:, kl], kf), x[:, kl]), att)
                        for kl, att in zip(pair_k, s["att"])]

    def state_operands():
        for s in streams:
            b = s["b"]
            b_last = b[0:1] if s["reverse"] else b[c - 1:c]
            s["q_in"] = (s["q"] * jnp.exp2(b)).astype(BF16)
            s["k_hat"] = (s["k"] * jnp.exp2(b_last - b)).astype(BF16)
            s["decay"] = jnp.exp2(b_last)
            s["att"] = [att.astype(BF16) for att in s["att"]]

    def output_and_update(s):
        vf = s["consts"]["val_first"]
        for p, (kl, vl) in enumerate(zip(pair_k, pair_v)):
            vp = s["v"][:, vl]
            st = s["st_ref"][p]
            o = _dot_tn(s["att"][p], _pair_blockdiag_rows(vp, vf))
            o = o + _dot_nt(s["q_in"][:, kl], st.astype(BF16))
            s["o_ref"][:, vl] = o
            upd = _dot_tn(vp, s["k_hat"][:, kl])
            s["st_ref"][p] = st * s["decay"][:, kl] + jnp.where(s["consts"]["state_mask"], upd, 0.0)

    n_levels = len(streams[0]["consts"]["levels"])
    prep = [cumulative_decay, diagonal] + [functools.partial(level, lvl) for lvl in range(n_levels)]
    prep.append(state_operands)
    tail = [functools.partial(output_and_update, s) for s in streams]
    return prep, tail


def _interleave(first, second):
    done = 0
    for idx, thunk in enumerate(first):
        thunk()
        upto = (len(second) * (idx + 1)) // len(first)
        for extra in second[done:upto]:
            extra()
        done = upto


def _gla_kernel(qf_ref, kf_ref, vf_ref, gf_ref, qb_ref, kb_ref, vb_ref, gb_ref, s0_ref,
                of_ref, ob_ref, sout_ref, st_ref, *, chunks):
    @pl.when(pl.program_id(0) == 0)
    def _():
        st_ref[...] = s0_ref[...]

    bsz = qf_ref.shape[0]
    dirs = ((0, qf_ref, kf_ref, vf_ref, gf_ref, of_ref), (1, qb_ref, kb_ref, vb_ref, gb_ref, ob_ref))
    consts = [_gla_consts(reverse=False), _gla_consts(reverse=True)]

    def chunk_stages(ci):
        streams = []
        for d, q_ref, k_ref, v_ref, g_ref, o_ref in dirs:
            rows = pl.ds(((chunks - 1 - ci) if d else ci) * GLA_CHUNK, GLA_CHUNK)
            for bi in range(bsz):
                streams.append(dict(q=q_ref[bi, rows, :], k=k_ref[bi, rows, :], v=v_ref[bi, rows, :],
                                    g=g_ref[bi, rows, :], st_ref=st_ref.at[d, bi], o_ref=o_ref.at[bi, rows],
                                    consts=consts[d], reverse=bool(d)))
        return _gla_stages(streams)

    prep, tail = chunk_stages(0)
    for thunk in prep:
        thunk()
    for ci in range(chunks):
        if ci + 1 < chunks:
            next_prep, next_tail = chunk_stages(ci + 1)
            _interleave(next_prep, tail)
            tail = next_tail
        else:
            for thunk in tail:
                thunk()

    @pl.when(pl.program_id(0) == pl.num_programs(0) - 1)
    def _():
        sout_ref[...] = st_ref[...]


def _gla_call(q, k, v, lg, state0):
    bsz, n, kw = q.shape
    vw = v.shape[2]
    chunks = min(GLA_CHUNKS_PER_STEP, n // GLA_CHUNK)
    br = chunks * GLA_CHUNK
    nstep = n // br

    def fwd(i):
        return (0, i, 0)

    def bwd(i):
        return (0, nstep - 1 - i, 0)

    return pl.pallas_call(
        functools.partial(_gla_kernel, chunks=chunks),
        grid=(nstep,),
        in_specs=[
            pl.BlockSpec((bsz, br, kw), fwd),
            pl.BlockSpec((bsz, br, kw), fwd),
            pl.BlockSpec((bsz, br, vw), fwd),
            pl.BlockSpec((bsz, br, kw), fwd),
            pl.BlockSpec((bsz, br, kw), bwd),
            pl.BlockSpec((bsz, br, kw), bwd),
            pl.BlockSpec((bsz, br, vw), bwd),
            pl.BlockSpec((bsz, br, kw), lambda i: (0, nstep - 1 - i, 1)),
            _full_spec(state0),
        ],
        out_specs=[pl.BlockSpec((bsz, br, vw), fwd), pl.BlockSpec((bsz, br, vw), bwd),
                   pl.BlockSpec(state0.shape, lambda i: (0,) * state0.ndim)],
        out_shape=[jax.ShapeDtypeStruct((bsz, n, vw), F32)] * 2
                  + [jax.ShapeDtypeStruct(state0.shape, F32)],
        scratch_shapes=[pltpu.VMEM(state0.shape, F32)],
        compiler_params=_params(("arbitrary",)),
        name="gla",
    )(q, k, v, lg, q, k, v, lg, state0)


def _pair_blockdiag(w):
    hd = w.shape[-1]
    per = LANES // hd
    lead = w.shape[:-3]
    wg = w.reshape(lead + (RG_HEADS // per, per, hd, hd))
    eye = jnp.eye(per, dtype=w.dtype)
    return jnp.einsum('...gpij,pq->...gpiqj', wg, eye).reshape(lead + (RG_HEADS // per, LANES, LANES))


def kernel(x, c, ctx, c_ctx, w_mod, b_mod, norm_g, ffn_w1, ffn_w3, ffn_w2, w_in, conv_w, conv_b,
           rg_lam, rg_wa, rg_ba, rg_wi, rg_bi, gla_wup, gla_bup, gla_norm_g, w_out, final_g):
    bsz, t, d = x.shape
    n_ctx = ctx.shape[1]
    depth = w_mod.shape[0]
    assert t % TOKEN_BLOCK == 0 and t % FFN_TOKENS == 0 and t % (GRID_W * SUBLANES) == 0
    assert n_ctx % (SUBLANES * SUBLANES) == 0 and n_ctx % GLA_CHUNK == 0 and bsz * n_ctx <= TOKEN_BLOCK

    rows = -(-(bsz + 1) // SUBLANES) * SUBLANES
    cvec = jnp.zeros((rows, d), F32).at[0].set(c_ctx).at[1:bsz + 1].set(c)
    mods = _mod_call(cvec, w_mod, b_mod).reshape(depth, rows, N_MOD, d)

    w1, w3, w2 = ffn_w1.astype(BF16), ffn_w3.astype(BF16), ffn_w2.astype(BF16)
    w_in_b, w_out_b = w_in.astype(BF16), w_out.astype(BF16)
    norm_g4 = norm_g.reshape(depth, 3, 1, d)
    zero = jnp.zeros((depth, GLA_RANK, GLA_KW), F32)
    wup_bd = jnp.concatenate([jnp.concatenate([gla_wup[:, 0], zero], axis=2),
                              jnp.concatenate([zero, gla_wup[:, 1]], axis=2)], axis=1).astype(BF16)
    bup = gla_bup.reshape(depth, 1, 2 * GLA_KW)
    wg = jnp.concatenate([_pair_blockdiag(rg_wa), _pair_blockdiag(rg_wi)], axis=-1).astype(BF16)
    conv_b3 = conv_b.reshape(depth, 1, RG_WIDTH)
    lam4 = rg_lam.reshape(depth, 2, 1, RG_WIDTH)
    ba4 = rg_ba.reshape(depth, 2, 1, RG_WIDTH)
    bi4 = rg_bi.reshape(depth, 2, 1, RG_WIDTH)
    gnorm = gla_norm_g.reshape(depth, 1, GLA_VW)
    fg = final_g.reshape(1, d)

    h_c = ctx.reshape(1, bsz * n_ctx, d)
    grid_rows = t // GRID_W
    h_l = x.reshape(bsz, grid_rows, GRID_W, d).swapaxes(1, 2).reshape(bsz, t, d)
    ctx_seg = n_ctx // SUBLANES
    for l in range(depth):
        last = l == depth - 1
        ffn = functools.partial(_ffn_call, mods=mods, norm_g4=norm_g4, w1=w1, w3=w3, w2=w2, final_g=fg, l=l)
        h_l = ffn(h_l, half=0, is_ctx=False, final=False)
        h_c = ffn(h_c, half=0, is_ctx=True, final=False)

        proj = functools.partial(_proj_call, mods=mods, norm_g4=norm_g4, w_in=w_in_b, wup_bd=wup_bd,
                                 bup=bup, l=l)
        xr_l, gr_l, q_l, k_l, v_l, og_l, lg_l = proj(h_l, is_ctx=False, seqs_per_block=0)
        xr_c, gr_c, q_c, k_c, v_c, og_c, lg_c = proj(h_c, is_ctx=True, seqs_per_block=bsz)

        rg = functools.partial(_rg_call, conv_w=conv_w, conv_b=conv_b3, lam=lam4, wg=wg, ba=ba4, bi=bi4, l=l)
        *scans_c, carry = rg(xr_c.reshape(bsz, ctx_seg, SUBLANES, RG_WIDTH),
                             jnp.zeros((2, bsz, 1, RG_WIDTH), F32))
        *scans_l, _ = rg(xr_l.reshape(bsz, GRID_W, grid_rows, RG_WIDTH), carry)
        scans_c = [a.reshape(xr_c.shape) for a in scans_c]
        scans_l = [a.reshape(xr_l.shape) for a in scans_l]

        per_seq = lambda a: a.reshape(bsz, n_ctx, a.shape[-1])
        state0 = jnp.zeros((2, bsz, GLA_PAIRS, PAIR_VW, LANES), F32)
        oc_f, oc_b, state = _gla_call(per_seq(q_c), per_seq(k_c), per_seq(v_c), per_seq(lg_c), state0)
        ol_f, ol_b, _ = _gla_call(q_l, k_l, v_l, lg_l, state)

        mixer_l = dict(gr=gr_l, hf=scans_l[0], hb=scans_l[1], o_f=ol_f, o_b=ol_b, og=og_l, gnorm=gnorm,
                       w_out=w_out_b, seqs_per_block=0)
        h_l = ffn(h_l, half=1, is_ctx=False, final=last, mixer=mixer_l)
        if not last:
            flat = lambda a: a.reshape(1, bsz * n_ctx, a.shape[-1])
            mixer_c = dict(gr=gr_c, hf=scans_c[0], hb=scans_c[1], o_f=flat(oc_f), o_b=flat(oc_b), og=og_c,
                           gnorm=gnorm, w_out=w_out_b, seqs_per_block=bsz)
            h_c = ffn(h_c, half=1, is_ctx=True, final=False, mixer=mixer_c)
    return h_l.reshape(bsz, GRID_W, grid_rows, d).swapaxes(1, 2).reshape(bsz, t, d)
```

```python
import functools
import math

import jax
import jax.numpy as jnp
from jax import lax
from jax.experimental import pallas as pl
from jax.experimental.pallas import tpu as pltpu

F32 = jnp.float32
BF16 = jnp.bfloat16

LANES = 128
SUBLANES = 8
FFN_TOKENS = 512
TOKEN_BLOCK = 512
SUB_ROWS = 256
FF_STEP = 1024
GLA_CHUNKS_PER_STEP = 8
VMEM_LIMIT = 56 * 1024 * 1024

N_MOD = 9
EPS = 1e-6
GRID_W = 64
RG_WIDTH = 512
RG_HEADS = 8
RG_C = 8.0
GLA_HEADS = 4
GLA_DK = 64
GLA_DV = 128
GLA_KW = GLA_HEADS * GLA_DK
GLA_VW = GLA_HEADS * GLA_DV
GLA_RANK = 16
GLA_GATE_NORM = 16.0
GLA_CHUNK = 64
GLA_PAIRS = GLA_KW // LANES
PAIR_VW = GLA_VW // GLA_PAIRS
LOG2E = math.log2(math.e)


def _params(sem):
    return pltpu.CompilerParams(dimension_semantics=sem, vmem_limit_bytes=VMEM_LIMIT)


def _fixed_spec(shape, index):
    return pl.BlockSpec(shape, lambda *_: index, pipeline_mode=pl.Buffered(1))


def _full_spec(arr):
    return _fixed_spec(arr.shape, (0,) * arr.ndim)


def _sigmoid(x):
    return 1.0 / (1.0 + jnp.exp(-x))


def _softplus(x):
    return jnp.maximum(x, 0.0) + jnp.log(1.0 + jnp.exp(-jnp.abs(x)))


def _ada_norm(x, m, g, j):
    ms = jnp.mean(x * x, axis=-1, keepdims=True)
    gain = g * (1.0 + m[3 * j + 1:3 * j + 2])
    return x * lax.rsqrt(ms + EPS) * gain + m[3 * j:3 * j + 1]


def _token_block(n, target):
    return min(n, target)


def _row_blocks(n_rows):
    step = min(SUB_ROWS, n_rows)
    return [slice(r, r + step) for r in range(0, n_rows, step)]


def _staged(blocks, prologue, matmul, epilogue):
    operand = prologue(blocks[0])
    pending = None
    for i, rows in enumerate(blocks):
        acc = matmul(operand)
        if i + 1 < len(blocks):
            operand = prologue(blocks[i + 1])
        if pending is not None:
            epilogue(*pending)
        pending = (rows, acc)
    epilogue(*pending)


def _mod_kernel(c_ref, w_ref, b_ref, o_ref):
    c = c_ref[...]
    sc = (c * _sigmoid(c)).astype(BF16)
    o_ref[...] = jnp.dot(sc, w_ref[...].astype(BF16), preferred_element_type=F32) + b_ref[...]


def _mod_call(cvec, w_mod, b_mod):
    depth, d, n = w_mod.shape
    rows = cvec.shape[0]
    tn = n // 8
    return pl.pallas_call(
        _mod_kernel,
        grid=(depth, n // tn),
        in_specs=[
            pl.BlockSpec((rows, d), lambda l, i: (0, 0)),
            pl.BlockSpec((None, d, tn), lambda l, i: (l, 0, i)),
            pl.BlockSpec((None, 1, tn), lambda l, i: (l, 0, i)),
        ],
        out_specs=pl.BlockSpec((None, rows, tn), lambda l, i: (l, 0, i)),
        out_shape=jax.ShapeDtypeStruct((depth, rows, n), F32),
        compiler_params=_params(("parallel", "parallel")),
        name="mod",
    )(cvec, w_mod, b_mod.reshape(depth, 1, n))


def _mod_spec(mods, l, is_ctx):
    d = mods.shape[-1]
    if is_ctx:
        return pl.BlockSpec((None, None, N_MOD, d), lambda b, i: (l, 0, 0, 0))
    return pl.BlockSpec((None, None, N_MOD, d), lambda b, i: (l, b + 1, 0, 0))


def _tok_spec(tm, w):
    return pl.BlockSpec((None, tm, w), lambda b, i: (b, i, 0))


def _store_interleaved(ref, val):
    nb, seg_len, _ = ref.shape
    w = val.shape[1]
    for bb in range(nb):
        for s in range(SUBLANES):
            r0 = (bb * SUBLANES + s) * seg_len
            ref[bb, :, s * w:(s + 1) * w] = val[r0:r0 + seg_len]


def _load_interleaved(ref):
    nb, _, sw = ref.shape
    w = sw // SUBLANES
    return jnp.concatenate([ref[bb, :, s * w:(s + 1) * w]
                            for bb in range(nb) for s in range(SUBLANES)], axis=0)


def _interleaved_spec(nb, seg_len, w, nblk):
    return pl.BlockSpec((nb, seg_len, SUBLANES * w), lambda b, i: (b * nblk + i, 0, 0))


def _mixer_rows(rows, gr_ref, hf_ref, hb_ref, of_ref, ob_ref, og_ref, gn):
    gr = gr_ref[rows, :]
    gelu = 0.5 * gr * (1.0 + jnp.tanh(0.7978845608028654 * (gr + 0.044715 * (gr * gr * gr))))
    if len(hf_ref.shape) == 3:
        rg = gelu * (_load_interleaved(hf_ref) + _load_interleaved(hb_ref))
    else:
        rg = gelu * (hf_ref[rows, :] + hb_ref[rows, :])
    o = of_ref[rows, :] + ob_ref[rows, :]
    og = og_ref[rows, :]
    parts = []
    for hd in range(GLA_HEADS):
        sl = slice(hd * GLA_DV, (hd + 1) * GLA_DV)
        oh = o[:, sl]
        ms = jnp.mean(oh * oh, axis=-1, keepdims=True)
        parts.append(oh * lax.rsqrt(ms + EPS) * gn[:, sl])
    gla = jnp.concatenate(parts, axis=-1) * (og * _sigmoid(og))
    return jnp.concatenate([rg, gla], axis=-1).astype(BF16)


def _ffn_kernel(*refs, j, final, ff_chunks, with_mixer):
    if with_mixer:
        (x_ref, m_ref, g_ref, gr_ref, hf_ref, hb_ref, of_ref, ob_ref, og_ref, gn_ref, wo_ref,
         w1_ref, w3_ref, w2_ref, fg_ref, o_ref) = refs
        whole = len(hf_ref.shape) == 3
    else:
        x_ref, m_ref, g_ref, w1_ref, w3_ref, w2_ref, fg_ref, o_ref = refs
        whole = False
    m = m_ref[...]
    n_rows = x_ref.shape[0]

    def prologue(rows):
        x = x_ref[rows, :]
        if with_mixer:
            mix = _mixer_rows(rows, gr_ref, hf_ref, hb_ref, of_ref, ob_ref, og_ref, gn_ref[...])
            x = x + m[5:6] * jnp.dot(mix, wo_ref[...], preferred_element_type=F32)
        return x, _ada_norm(x, m, g_ref[...], j).astype(BF16)

    def matmul(operand):
        x, u = operand
        y = None
        for lo, hi in ff_chunks:
            h1 = jnp.dot(u, w1_ref[:, lo:hi], preferred_element_type=F32)
            h3 = jnp.dot(u, w3_ref[:, lo:hi], preferred_element_type=F32)
            act = (h1 * _sigmoid(h1) * h3).astype(BF16)
            part = jnp.dot(act, w2_ref[lo:hi, :], preferred_element_type=F32)
            y = part if y is None else y + part
        return x, y

    def epilogue(rows, acc):
        x, y = acc
        out = x + (0.5 * m[3 * j + 2:3 * j + 3]) * y
        if final:
            ms = jnp.mean(out * out, axis=-1, keepdims=True)
            out = out * lax.rsqrt(ms + EPS) * fg_ref[...]
        o_ref[rows, :] = out

    _staged([slice(0, n_rows)] if whole else _row_blocks(n_rows), prologue, matmul, epilogue)


def _ffn_call(h, mods, norm_g4, w1, w3, w2, final_g, *, l, half, is_ctx, final, mixer=None):
    bsz, n, d = h.shape
    dff = w1.shape[-1]
    tm = _token_block(n, FFN_TOKENS)
    nblk = n // tm
    j = 2 * half
    ff_chunks = tuple((lo, min(lo + FF_STEP, dff)) for lo in range(0, dff, FF_STEP))
    operands = [h, mods, norm_g4]
    in_specs = [_tok_spec(tm, d), _mod_spec(mods, l, is_ctx), _fixed_spec((None, None, 1, d), (l, j, 0, 0))]
    if mixer is not None:
        spb = mixer["seqs_per_block"]
        if spb:
            scan_spec = _interleaved_spec(spb, tm // (spb * SUBLANES), RG_WIDTH, nblk)
        else:
            scan_spec = _tok_spec(tm, RG_WIDTH)
        operands += [mixer[name] for name in ("gr", "hf", "hb", "o_f", "o_b", "og", "gnorm", "w_out")]
        in_specs += [_tok_spec(tm, RG_WIDTH), scan_spec, scan_spec,
                     _tok_spec(tm, GLA_VW), _tok_spec(tm, GLA_VW), _tok_spec(tm, GLA_VW),
                     _fixed_spec((None, 1, GLA_VW), (l, 0, 0)),
                     _fixed_spec((None,) + mixer["w_out"].shape[1:], (l, 0, 0))]
    operands += [w1, w3, w2, final_g]
    in_specs += [_fixed_spec((None, None, d, dff), (l, half, 0, 0)),
                 _fixed_spec((None, None, d, dff), (l, half, 0, 0)),
                 _fixed_spec((None, None, dff, d), (l, half, 0, 0)),
                 _full_spec(final_g)]
    return pl.pallas_call(
        functools.partial(_ffn_kernel, j=j, final=final, ff_chunks=ff_chunks, with_mixer=mixer is not None),
        grid=(bsz, nblk),
        in_specs=in_specs,
        out_specs=_tok_spec(tm, d),
        out_shape=jax.ShapeDtypeStruct((bsz, n, d), F32),
        compiler_params=_params(("parallel", "parallel")),
        name="mix_ffn" if mixer is not None else "ffn",
    )(*operands)


def _proj_kernel(x_ref, m_ref, g_ref, w_ref, wup_ref, bup_ref,
                 xr_ref, gr_ref, q_ref, k_ref, v_ref, og_ref, lg_ref):
    interleaved = len(xr_ref.shape) == 3
    m = m_ref[...]

    def cols(u, lo, width):
        return jnp.dot(u, w_ref[:, lo:lo + width], preferred_element_type=F32)

    def project(rows, u):
        lr = cols(u, 2 * RG_WIDTH + 2 * GLA_KW + 2 * GLA_VW, 2 * GLA_RANK).astype(BF16)
        z = jnp.dot(lr, wup_ref[...], preferred_element_type=F32) + bup_ref[...]
        lg_ref[rows, :] = _softplus(-z) * (-LOG2E / GLA_GATE_NORM)
        o = 0
        xr = cols(u, o, RG_WIDTH); o += RG_WIDTH
        if interleaved:
            _store_interleaved(xr_ref, xr)
        else:
            xr_ref[rows, :] = xr
        gr_ref[rows, :] = cols(u, o, RG_WIDTH); o += RG_WIDTH
        q_ref[rows, :] = cols(u, o, GLA_KW) * (GLA_DK ** -0.5); o += GLA_KW
        k_ref[rows, :] = cols(u, o, GLA_KW); o += GLA_KW
        v_ref[rows, :] = cols(u, o, GLA_VW).astype(BF16); o += GLA_VW
        og_ref[rows, :] = cols(u, o, GLA_VW)

    n_rows = x_ref.shape[0]
    blocks = [slice(0, n_rows)] if interleaved else _row_blocks(n_rows)
    u = _ada_norm(x_ref[blocks[0], :], m, g_ref[...], 1).astype(BF16)
    for i, rows in enumerate(blocks):
        u_next = None
        if i + 1 < len(blocks):
            u_next = _ada_norm(x_ref[blocks[i + 1], :], m, g_ref[...], 1).astype(BF16)
        project(rows, u)
        u = u_next


def _proj_call(h, mods, norm_g4, w_in, wup_bd, bup, *, l, is_ctx, seqs_per_block):
    bsz, n, d = h.shape
    tm = _token_block(n, TOKEN_BLOCK)
    nblk = n // tm
    outs = ((RG_WIDTH, F32), (GLA_KW, F32), (GLA_KW, F32), (GLA_VW, BF16), (GLA_VW, F32), (2 * GLA_KW, F32))
    if seqs_per_block:
        seg_len = tm // (seqs_per_block * SUBLANES)
        xr_shape = (bsz * nblk * seqs_per_block, seg_len, SUBLANES * RG_WIDTH)
        xr_spec = _interleaved_spec(seqs_per_block, seg_len, RG_WIDTH, nblk)
    else:
        xr_shape = (bsz, n, RG_WIDTH)
        xr_spec = _tok_spec(tm, RG_WIDTH)
    return pl.pallas_call(
        _proj_kernel,
        grid=(bsz, nblk),
        in_specs=[
            _tok_spec(tm, d),
            _mod_spec(mods, l, is_ctx),
            _fixed_spec((None, None, 1, d), (l, 1, 0, 0)),
            _fixed_spec((None,) + w_in.shape[1:], (l, 0, 0)),
            _fixed_spec((None,) + wup_bd.shape[1:], (l, 0, 0)),
            _fixed_spec((None, 1, bup.shape[-1]), (l, 0, 0)),
        ],
        out_specs=[xr_spec] + [_tok_spec(tm, w) for w, _ in outs],
        out_shape=[jax.ShapeDtypeStruct(xr_shape, F32)]
                  + [jax.ShapeDtypeStruct((bsz, n, w), dt) for w, dt in outs],
        compiler_params=_params(("parallel", "parallel")),
        name="proj",
    )(h, mods, norm_g4, w_in, wup_bd, bup)


def _rg_kernel(xf_ref, xfp_ref, xfn_ref, xb_ref, xbp_ref, xbn_ref, c0_ref, cw_ref, cb_ref, lam_ref, wg_ref,
               ba_ref, bi_ref, of_ref, ob_ref, cout_ref, carry_ref, *, nblk):
    for d, (x_ref, xp_ref, xn_ref, o_ref) in enumerate(((xf_ref, xfp_ref, xfn_ref, of_ref),
                                                         (xb_ref, xbp_ref, xbn_ref, ob_ref))):
        _rg_direction(x_ref, xp_ref, xn_ref, c0_ref.at[d], cw_ref, cb_ref, lam_ref.at[d], wg_ref.at[d],
                      ba_ref.at[d], bi_ref.at[d], o_ref, cout_ref.at[d], carry_ref.at[d],
                      reverse=bool(d), nblk=nblk)


def _rg_direction(x_ref, xp_ref, xn_ref, c0_ref, cw_ref, cb_ref, lam_ref, wg_ref, ba_ref, bi_ref,
                  o_ref, cout_ref, carry_ref, *, reverse, nblk):
    seg_len = x_ref.shape[0]
    j = pl.program_id(1)
    blk = (nblk - 1 - j) if reverse else j

    @pl.when(j == 0)
    def _():
        carry_ref[...] = c0_ref[...]

    x = x_ref[...]
    last = SUBLANES - 1
    prev2 = jnp.where(blk == 0, 0.0, xp_ref[last - 1, last:, :])
    prev1 = jnp.where(blk == 0, 0.0, xp_ref[last, last:, :])
    next1 = jnp.where(blk == nblk - 1, 0.0, xn_ref[0, 0:1, :])
    row8 = lax.broadcasted_iota(jnp.int32, (SUBLANES, 1), 0)

    def from_prev_segment(tile, halo):
        return jnp.where(row8 == 0, halo, pltpu.roll(tile, 1, axis=0))

    def from_next_segment(tile, halo):
        return jnp.where(row8 == last, halo, pltpu.roll(tile, last, axis=0))

    xm2 = from_prev_segment(x[seg_len - 2], prev2)
    xm1 = from_prev_segment(x[seg_len - 1], prev1)
    xp1 = from_next_segment(x[0], next1)
    xe = jnp.concatenate([xm2[None], xm1[None], x, xp1[None]], axis=0)
    cw = cw_ref[...]
    xc = xe[0:seg_len] * cw[0:1] + xe[1:seg_len + 1] * cw[1:2]
    xc = xc + xe[2:seg_len + 2] * cw[2:3]
    xc = xc + xe[3:seg_len + 3] * cw[3:4]
    xc = (xc + cb_ref[...]).reshape(seg_len * SUBLANES, RG_WIDTH)

    neg_c_sp = (-RG_C) * _softplus(-lam_ref[...])
    order = range(seg_len - 1, -1, -1) if reverse else range(seg_len)
    for g in range(RG_WIDTH // LANES):
        sl = slice(g * LANES, (g + 1) * LANES)
        xg = xc[:, sl]
        pre = jnp.dot(xg.astype(BF16), wg_ref[g], preferred_element_type=F32)
        r = _sigmoid(pre[:, :LANES] + ba_ref[:, sl])
        i = _sigmoid(pre[:, LANES:] + bi_ref[:, sl])
        log_a = r * neg_c_sp[:, sl]
        a = jnp.exp(log_a)
        bv = jnp.sqrt(1.0 - a * a) * (i * xg)
        hs, ps = [None] * seg_len, [None] * seg_len
        h_run = p_run = None
        for t in order:
            rows = slice(t * SUBLANES, (t + 1) * SUBLANES)
            if h_run is None:
                h_run, p_run = bv[rows], a[rows]
            else:
                h_run = a[rows] * h_run + bv[rows]
                p_run = a[rows] * p_run
            hs[t], ps[t] = h_run, p_run
        pc, hc = p_run, h_run
        s = 1
        while s < SUBLANES:
            shift = SUBLANES - s if reverse else s
            valid = (row8 < SUBLANES - s) if reverse else (row8 >= s)
            hc = jnp.where(valid, pc * pltpu.roll(hc, shift, axis=0) + hc, hc)
            pc = jnp.where(valid, pc * pltpu.roll(pc, shift, axis=0), pc)
            s *= 2
        carry = carry_ref[:, sl]
        seg_end = pc * carry + hc
        if reverse:
            seg_in = jnp.where(row8 == last, carry, pltpu.roll(seg_end, last, axis=0))
            carry_ref[:, sl] = seg_end[0:1]
        else:
            seg_in = jnp.where(row8 == 0, carry, pltpu.roll(seg_end, 1, axis=0))
            carry_ref[:, sl] = seg_end[last:]
        for t in range(seg_len):
            o_ref[t, :, sl] = hs[t] + ps[t] * seg_in

    @pl.when(j == nblk - 1)
    def _():
        cout_ref[...] = carry_ref[...]


def _rg_call(x4, carry0, conv_w, conv_b, lam, wg, ba, bi, *, l):
    bsz, seg_len, nseg, w = x4.shape
    nblk = nseg // SUBLANES
    tiles = seg_len // SUBLANES

    def block_specs(blk):
        return [
            pl.BlockSpec((None, seg_len, SUBLANES, w), lambda b, j: (b, 0, blk(j), 0)),
            pl.BlockSpec((None, SUBLANES, SUBLANES, w),
                         lambda b, j: (b, tiles - 1, jnp.maximum(blk(j) - 1, 0), 0)),
            pl.BlockSpec((None, SUBLANES, SUBLANES, w),
                         lambda b, j: (b, 0, jnp.minimum(blk(j) + 1, nblk - 1), 0)),
        ]

    fwd = block_specs(lambda j: j)
    bwd = block_specs(lambda j: nblk - 1 - j)
    both_dirs = _fixed_spec((None, 2, 1, w), (l, 0, 0, 0))
    state_spec = pl.BlockSpec((2, None, 1, w), lambda b, j: (0, b, 0, 0))
    return pl.pallas_call(
        functools.partial(_rg_kernel, nblk=nblk),
        grid=(bsz, nblk),
        in_specs=fwd + bwd + [
            state_spec,
            _fixed_spec((None,) + conv_w.shape[1:], (l, 0, 0)),
            _fixed_spec((None, 1, w), (l, 0, 0)),
            both_dirs,
            _fixed_spec((None,) + wg.shape[1:], (l, 0, 0, 0, 0)),
            both_dirs,
            both_dirs,
        ],
        out_specs=[fwd[0], bwd[0], state_spec],
        out_shape=[jax.ShapeDtypeStruct(x4.shape, F32)] * 2 + [jax.ShapeDtypeStruct(carry0.shape, F32)],
        scratch_shapes=[pltpu.VMEM((2, 1, w), F32)],
        compiler_params=_params(("arbitrary", "arbitrary")),
        name="rg",
    )(x4, x4, x4, x4, x4, x4, carry0, conv_w, conv_b, lam, wg, ba, bi)


def _level_ref(b, m, r):
    c, w = b.shape
    if m >= SUBLANES:
        b3 = b.reshape(c // m, m, w)
        return jnp.broadcast_to(b3[:, r:r + 1, :], (c // m, m, w)).reshape(c, w)
    bt = b.reshape(c // SUBLANES, SUBLANES, w)
    sub = lax.broadcasted_iota(jnp.int32, (1, SUBLANES, 1), 1)
    out = None
    for blk in range(SUBLANES // m):
        src = blk * m + r
        piece = jnp.broadcast_to(bt[:, src:src + 1, :], bt.shape)
        out = piece if out is None else jnp.where(sub >= blk * m, piece, out)
    return out.reshape(c, w)


def _halves_select(m, q_rows, q, k, reverse):
    half = m // 2
    if half % SUBLANES:
        return jnp.where(q_rows, q, k)
    first, second = (q, k) if reverse else (k, q)
    pieces = []
    for lo in range(0, q.shape[0], m):
        pieces += [first[lo:lo + half], second[lo + half:lo + m]]
    return jnp.concatenate(pieces, axis=0)


def _dot_nt(a, b):
    return lax.dot_general(a, b, (((1,), (1,)), ((), ())), preferred_element_type=F32)


def _dot_tn(a, b):
    return lax.dot_general(a, b, (((0,), (0,)), ((), ())), preferred_element_type=F32)


def _gla_consts(reverse):
    c = GLA_CHUNK
    row = lax.broadcasted_iota(jnp.int32, (c, 1), 0)
    r2 = lax.broadcasted_iota(jnp.int32, (c, c), 0)
    c2 = lax.broadcasted_iota(jnp.int32, (c, c), 1)
    tri = jnp.where((c2 >= r2) if reverse else (c2 <= r2), 1.0, 0.0).astype(BF16)
    klane = lax.broadcasted_iota(jnp.int32, (1, LANES), 1)
    vlane = lax.broadcasted_iota(jnp.int32, (1, PAIR_VW), 1)
    s = lax.broadcasted_iota(jnp.int32, (2 * c, c), 0) % c
    t = lax.broadcasted_iota(jnp.int32, (2 * c, c), 1)
    levels = []
    m = c
    while m >= 2:
        half = m // 2
        late = (row % m) >= half
        q_rows = ~late if reverse else late
        same = (t // m) == (s // m)
        t_late = (t % m) >= half
        s_late = (s % m) >= half
        mask = same & (~t_late) & s_late if reverse else same & t_late & (~s_late)
        levels.append((m, q_rows, mask))
        m = half
    key_first = klane < GLA_DK
    srow = lax.broadcasted_iota(jnp.int32, (PAIR_VW, 1), 0)
    state_mask = (srow < GLA_DV) == key_first
    return dict(tri=tri, key_first=key_first, val_first=vlane < GLA_DV, state_mask=state_mask,
                levels=levels, diag=(t == s))


def _pair_blockdiag_rows(x, first_mask):
    zero = jnp.zeros_like(x)
    return jnp.concatenate([jnp.where(first_mask, x, zero), jnp.where(first_mask, zero, x)], axis=0)


def _gla_stages(streams):
    c = GLA_CHUNK
    pair_k = [slice(p * LANES, (p + 1) * LANES) for p in range(GLA_PAIRS)]
    pair_v = [slice(p * PAIR_VW, (p + 1) * PAIR_VW) for p in range(GLA_PAIRS)]

    def cumulative_decay():
        for s in streams:
            g, tri = s["g"], s["consts"]["tri"]
            g1 = g.astype(BF16)
            g2 = (g - g1.astype(F32)).astype(BF16)
            g3 = (g - g1.astype(F32) - g2.astype(F32)).astype(BF16)
            s["b"] = (jnp.dot(tri, g1, preferred_element_type=F32)
                      + jnp.dot(tri, g2, preferred_element_type=F32)
                      + jnp.dot(tri, g3, preferred_element_type=F32))

    def diagonal():
        for s in streams:
            kf = s["consts"]["key_first"]
            qb, kb = s["q"].astype(BF16), s["k"].astype(BF16)
            s["att"] = [jnp.where(s["consts"]["diag"],
                                  _dot_nt(_pair_blockdiag_rows(kb[:, kl], kf), qb[:, kl]), 0.0) for kl in pair_k]

    def level(lvl):
        for s in streams:
            m, q_rows, mask = s["consts"]["levels"][lvl]
            half = m // 2
            if m == 2:
                x = jnp.where(q_rows, s["q"] * jnp.exp2(s["g"]), s["k"])
            else:
                ref = _level_ref(s["b"], m, half if s["reverse"] else half - 1)
                x = jnp.exp2(-jnp.abs(s["b"] - ref)) * _halves_select(m, q_rows, s["q"], s["k"], s["reverse"])
            x = x.astype(BF16)
            kf = s["consts"]["key_first"]
            s["att"] = [jnp.where(mask, _dot_nt(_pair_blockdiag_rows(x[:, kl], kf), x[:, kl]), att)
                        for kl, att in zip(pair_k, s["att"])]

    def state_operands():
        for s in streams:
            b = s["b"]
            b_last = b[0:1] if s["reverse"] else b[c - 1:c]
            s["q_in"] = (s["q"] * jnp.exp2(b)).astype(BF16)
            s["k_hat"] = (s["k"] * jnp.exp2(b_last - b)).astype(BF16)
            s["decay"] = jnp.exp2(b_last)
            s["att"] = [att.astype(BF16) for att in s["att"]]

    def output_and_update(s):
        vf = s["consts"]["val_first"]
        for p, (kl, vl) in enumerate(zip(pair_k, pair_v)):
            vp = s["v"][:, vl]
            st = s["st_ref"][p]
            o = _dot_tn(s["att"][p], _pair_blockdiag_rows(vp, vf))
            o = o + _dot_nt(s["q_in"][:, kl], st.astype(BF16))
            s["o_ref"][:, vl] = o
            upd = _dot_tn(vp, s["k_hat"][:, kl])
            s["st_ref"][p] = st * s["decay"][:, kl] + jnp.where(s["consts"]["state_mask"], upd, 0.0)

    n_levels = len(streams[0]["consts"]["levels"])
    prep = [cumulative_decay, diagonal] + [functools.partial(level, lvl) for lvl in range(n_levels)]
    prep.append(state_operands)
    tail = [functools.partial(output_and_update, s) for s in streams]
    return prep, tail


def _interleave(first, second):
    done = 0
    for idx, thunk in enumerate(first):
        thunk()
        upto = (len(second) * (idx + 1)) // len(first)
        for extra in second[done:upto]:
            extra()
        done = upto


def _gla_kernel(qf_ref, kf_ref, vf_ref, gf_ref, qb_ref, kb_ref, vb_ref, gb_ref, s0_ref,
                of_ref, ob_ref, sout_ref, st_ref, *, chunks):
    @pl.when(pl.program_id(0) == 0)
    def _():
        st_ref[...] = s0_ref[...]

    bsz = qf_ref.shape[0]
    dirs = ((0, qf_ref, kf_ref, vf_ref, gf_ref, of_ref), (1, qb_ref, kb_ref, vb_ref, gb_ref, ob_ref))
    consts = [_gla_consts(reverse=False), _gla_consts(reverse=True)]

    def chunk_stages(ci):
        streams = []
        for d, q_ref, k_ref, v_ref, g_ref, o_ref in dirs:
            rows = pl.ds(((chunks - 1 - ci) if d else ci) * GLA_CHUNK, GLA_CHUNK)
            for bi in range(bsz):
                streams.append(dict(q=q_ref[bi, rows, :], k=k_ref[bi, rows, :], v=v_ref[bi, rows, :],
                                    g=g_ref[bi, rows, :], st_ref=st_ref.at[d, bi], o_ref=o_ref.at[bi, rows],
                                    consts=consts[d], reverse=bool(d)))
        return _gla_stages(streams)

    prep, tail = chunk_stages(0)
    for thunk in prep:
        thunk()
    for ci in range(chunks):
        if ci + 1 < chunks:
            next_prep, next_tail = chunk_stages(ci + 1)
            _interleave(next_prep, tail)
            tail = next_tail
        else:
            for thunk in tail:
                thunk()

    @pl.when(pl.program_id(0) == pl.num_programs(0) - 1)
    def _():
        sout_ref[...] = st_ref[...]


def _gla_call(q, k, v, lg, state0):
    bsz, n, kw = q.shape
    vw = v.shape[2]
    chunks = min(GLA_CHUNKS_PER_STEP, n // GLA_CHUNK)
    br = chunks * GLA_CHUNK
    nstep = n // br

    def fwd(i):
        return (0, i, 0)

    def bwd(i):
        return (0, nstep - 1 - i, 0)

    return pl.pallas_call(
        functools.partial(_gla_kernel, chunks=chunks),
        grid=(nstep,),
        in_specs=[
            pl.BlockSpec((bsz, br, kw), fwd),
            pl.BlockSpec((bsz, br, kw), fwd),
            pl.BlockSpec((bsz, br, vw), fwd),
            pl.BlockSpec((bsz, br, kw), fwd),
            pl.BlockSpec((bsz, br, kw), bwd),
            pl.BlockSpec((bsz, br, kw), bwd),
            pl.BlockSpec((bsz, br, vw), bwd),
            pl.BlockSpec((bsz, br, kw), lambda i: (0, nstep - 1 - i, 1)),
            _full_spec(state0),
        ],
        out_specs=[pl.BlockSpec((bsz, br, vw), fwd), pl.BlockSpec((bsz, br, vw), bwd),
                   pl.BlockSpec(state0.shape, lambda i: (0,) * state0.ndim)],
        out_shape=[jax.ShapeDtypeStruct((bsz, n, vw), F32)] * 2
                  + [jax.ShapeDtypeStruct(state0.shape, F32)],
        scratch_shapes=[pltpu.VMEM(state0.shape, F32)],
        compiler_params=_params(("arbitrary",)),
        name="gla",
    )(q, k, v, lg, q, k, v, lg, state0)


def _pair_blockdiag(w):
    hd = w.shape[-1]
    per = LANES // hd
    lead = w.shape[:-3]
    wg = w.reshape(lead + (RG_HEADS // per, per, hd, hd))
    eye = jnp.eye(per, dtype=w.dtype)
    return jnp.einsum('...gpij,pq->...gpiqj', wg, eye).reshape(lead + (RG_HEADS // per, LANES, LANES))


def kernel(x, c, ctx, c_ctx, w_mod, b_mod, norm_g, ffn_w1, ffn_w3, ffn_w2, w_in, conv_w, conv_b,
           rg_lam, rg_wa, rg_ba, rg_wi, rg_bi, gla_wup, gla_bup, gla_norm_g, w_out, final_g):
    bsz, t, d = x.shape
    n_ctx = ctx.shape[1]
    depth = w_mod.shape[0]
    assert t % TOKEN_BLOCK == 0 and t % FFN_TOKENS == 0 and t % (GRID_W * SUBLANES) == 0
    assert n_ctx % (SUBLANES * SUBLANES) == 0 and n_ctx % GLA_CHUNK == 0 and bsz * n_ctx <= TOKEN_BLOCK

    rows = -(-(bsz + 1) // SUBLANES) * SUBLANES
    cvec = jnp.zeros((rows, d), F32).at[0].set(c_ctx).at[1:bsz + 1].set(c)
    mods = _mod_call(cvec, w_mod, b_mod).reshape(depth, rows, N_MOD, d)

    w1, w3, w2 = ffn_w1.astype(BF16), ffn_w3.astype(BF16), ffn_w2.astype(BF16)
    w_in_b, w_out_b = w_in.astype(BF16), w_out.astype(BF16)
    norm_g4 = norm_g.reshape(depth, 3, 1, d)
    zero = jnp.zeros((depth, GLA_RANK, GLA_KW), F32)
    wup_bd = jnp.concatenate([jnp.concatenate([gla_wup[:, 0], zero], axis=2),
                              jnp.concatenate([zero, gla_wup[:, 1]], axis=2)], axis=1).astype(BF16)
    bup = gla_bup.reshape(depth, 1, 2 * GLA_KW)
    wg = jnp.concatenate([_pair_blockdiag(rg_wa), _pair_blockdiag(rg_wi)], axis=-1).astype(BF16)
    conv_b3 = conv_b.reshape(depth, 1, RG_WIDTH)
    lam4 = rg_lam.reshape(depth, 2, 1, RG_WIDTH)
    ba4 = rg_ba.reshape(depth, 2, 1, RG_WIDTH)
    bi4 = rg_bi.reshape(depth, 2, 1, RG_WIDTH)
    gnorm = gla_norm_g.reshape(depth, 1, GLA_VW)
    fg = final_g.reshape(1, d)

    h_c = ctx.reshape(1, bsz * n_ctx, d)
    grid_rows = t // GRID_W
    h_l = x.reshape(bsz, grid_rows, GRID_W, d).swapaxes(1, 2).reshape(bsz, t, d)
    ctx_seg = n_ctx // SUBLANES
    for l in range(depth):
        last = l == depth - 1
        ffn = functools.partial(_ffn_call, mods=mods, norm_g4=norm_g4, w1=w1, w3=w3, w2=w2, final_g=fg, l=l)
        h_l = ffn(h_l, half=0, is_ctx=False, final=False)
        h_c = ffn(h_c, half=0, is_ctx=True, final=False)

        proj = functools.partial(_proj_call, mods=mods, norm_g4=norm_g4, w_in=w_in_b, wup_bd=wup_bd,
                                 bup=bup, l=l)
        xr_l, gr_l, q_l, k_l, v_l, og_l, lg_l = proj(h_l, is_ctx=False, seqs_per_block=0)
        xr_c, gr_c, q_c, k_c, v_c, og_c, lg_c = proj(h_c, is_ctx=True, seqs_per_block=bsz)

        rg = functools.partial(_rg_call, conv_w=conv_w, conv_b=conv_b3, lam=lam4, wg=wg, ba=ba4, bi=bi4, l=l)
        *scans_c, carry = rg(xr_c.reshape(bsz, ctx_seg, SUBLANES, RG_WIDTH),
                             jnp.zeros((2, bsz, 1, RG_WIDTH), F32))
        *scans_l, _ = rg(xr_l.reshape(bsz, GRID_W, grid_rows, RG_WIDTH), carry)
        scans_c = [a.reshape(xr_c.shape) for a in scans_c]
        scans_l = [a.reshape(xr_l.shape) for a in scans_l]

        per_seq = lambda a: a.reshape(bsz, n_ctx, a.shape[-1])
        state0 = jnp.zeros((2, bsz, GLA_PAIRS, PAIR_VW, LANES), F32)
        oc_f, oc_b, state = _gla_call(per_seq(q_c), per_seq(k_c), per_seq(v_c), per_seq(lg_c), state0)
        ol_f, ol_b, _ = _gla_call(q_l, k_l, v_l, lg_l, state)

        mixer_l = dict(gr=gr_l, hf=scans_l[0], hb=scans_l[1], o_f=ol_f, o_b=ol_b, og=og_l, gnorm=gnorm,
                       w_out=w_out_b, seqs_per_block=0)
        h_l = ffn(h_l, half=1, is_ctx=False, final=last, mixer=mixer_l)
        if not last:
            flat = lambda a: a.reshape(1, bsz * n_ctx, a.shape[-1])
            mixer_c = dict(gr=gr_c, hf=scans_c[0], hb=scans_c[1], o_f=flat(oc_f), o_b=flat(oc_b), og=og_c,
                           gnorm=gnorm, w_out=w_out_b, seqs_per_block=bsz)
            h_c = ffn(h_c, half=1, is_ctx=True, final=False, mixer=mixer_c)
    return h_l.reshape(bsz, GRID_W, grid_rows, d).swapaxes(1, 2).reshape(bsz, t, d)
```

```python
import functools
import math

import jax
import jax.numpy as jnp
from jax import lax
from jax.experimental import pallas as pl
from jax.experimental.pallas import tpu as pltpu

F32 = jnp.float32
BF16 = jnp.bfloat16

LANES = 128
SUBLANES = 8
FFN_TOKENS = 512
TOKEN_BLOCK = 1024
SUB_ROWS = 256
FF_STEP = 1024
GLA_CHUNKS_PER_STEP = 8
VMEM_LIMIT = 56 * 1024 * 1024

N_MOD = 9
EPS = 1e-6
GRID_W = 64
RG_WIDTH = 512
RG_HEADS = 8
RG_C = 8.0
GLA_HEADS = 4
GLA_DK = 64
GLA_DV = 128
GLA_KW = GLA_HEADS * GLA_DK
GLA_VW = GLA_HEADS * GLA_DV
GLA_RANK = 16
GLA_GATE_NORM = 16.0
GLA_CHUNK = 64
GLA_PAIRS = GLA_KW // LANES
PAIR_VW = GLA_VW // GLA_PAIRS
LOG2E = math.log2(math.e)


def _params(sem):
    return pltpu.CompilerParams(dimension_semantics=sem, vmem_limit_bytes=VMEM_LIMIT)


def _fixed_spec(shape, index):
    return pl.BlockSpec(shape, lambda *_: index, pipeline_mode=pl.Buffered(1))


def _full_spec(arr):
    return _fixed_spec(arr.shape, (0,) * arr.ndim)


def _sigmoid(x):
    return 1.0 / (1.0 + jnp.exp(-x))


def _softplus(x):
    return jnp.maximum(x, 0.0) + jnp.log(1.0 + jnp.exp(-jnp.abs(x)))


def _ada_norm(x, m, g, j):
    ms = jnp.mean(x * x, axis=-1, keepdims=True)
    gain = g * (1.0 + m[3 * j + 1:3 * j + 2])
    return x * lax.rsqrt(ms + EPS) * gain + m[3 * j:3 * j + 1]


def _token_block(n, target):
    return min(n, target)


def _row_blocks(n_rows):
    step = min(SUB_ROWS, n_rows)
    return [slice(r, r + step) for r in range(0, n_rows, step)]


def _staged(blocks, prologue, matmul, epilogue):
    operand = prologue(blocks[0])
    pending = None
    for i, rows in enumerate(blocks):
        acc = matmul(operand)
        if i + 1 < len(blocks):
            operand = prologue(blocks[i + 1])
        if pending is not None:
            epilogue(*pending)
        pending = (rows, acc)
    epilogue(*pending)


def _mod_kernel(c_ref, w_ref, b_ref, o_ref):
    c = c_ref[...]
    sc = (c * _sigmoid(c)).astype(BF16)
    o_ref[...] = jnp.dot(sc, w_ref[...].astype(BF16), preferred_element_type=F32) + b_ref[...]


def _mod_call(cvec, w_mod, b_mod):
    depth, d, n = w_mod.shape
    rows = cvec.shape[0]
    tn = n // 8
    return pl.pallas_call(
        _mod_kernel,
        grid=(depth, n // tn),
        in_specs=[
            pl.BlockSpec((rows, d), lambda l, i: (0, 0)),
            pl.BlockSpec((None, d, tn), lambda l, i: (l, 0, i)),
            pl.BlockSpec((None, 1, tn), lambda l, i: (l, 0, i)),
        ],
        out_specs=pl.BlockSpec((None, rows, tn), lambda l, i: (l, 0, i)),
        out_shape=jax.ShapeDtypeStruct((depth, rows, n), F32),
        compiler_params=_params(("parallel", "parallel")),
        name="mod",
    )(cvec, w_mod, b_mod.reshape(depth, 1, n))


def _mod_spec(mods, l, is_ctx):
    d = mods.shape[-1]
    if is_ctx:
        return pl.BlockSpec((None, None, N_MOD, d), lambda b, i: (l, 0, 0, 0))
    return pl.BlockSpec((None, None, N_MOD, d), lambda b, i: (l, b + 1, 0, 0))


def _tok_spec(tm, w):
    return pl.BlockSpec((None, tm, w), lambda b, i: (b, i, 0))


def _store_interleaved(ref, val):
    nb, seg_len, _ = ref.shape
    w = val.shape[1]
    for bb in range(nb):
        for s in range(SUBLANES):
            r0 = (bb * SUBLANES + s) * seg_len
            ref[bb, :, s * w:(s + 1) * w] = val[r0:r0 + seg_len]


def _load_interleaved(ref):
    nb, _, sw = ref.shape
    w = sw // SUBLANES
    return jnp.concatenate([ref[bb, :, s * w:(s + 1) * w]
                            for bb in range(nb) for s in range(SUBLANES)], axis=0)


def _interleaved_spec(nb, seg_len, w, nblk):
    return pl.BlockSpec((nb, seg_len, SUBLANES * w), lambda b, i: (b * nblk + i, 0, 0))


def _mixer_rows(rows, gr_ref, hf_ref, hb_ref, of_ref, ob_ref, og_ref, gn):
    gr = gr_ref[rows, :]
    gelu = 0.5 * gr * (1.0 + jnp.tanh(0.7978845608028654 * (gr + 0.044715 * (gr * gr * gr))))
    if len(hf_ref.shape) == 3:
        rg = gelu * (_load_interleaved(hf_ref) + _load_interleaved(hb_ref))
    else:
        rg = gelu * (hf_ref[rows, :] + hb_ref[rows, :])
    o = of_ref[rows, :] + ob_ref[rows, :]
    og = og_ref[rows, :]
    parts = []
    for hd in range(GLA_HEADS):
        sl = slice(hd * GLA_DV, (hd + 1) * GLA_DV)
        oh = o[:, sl]
        ms = jnp.mean(oh * oh, axis=-1, keepdims=True)
        parts.append(oh * lax.rsqrt(ms + EPS) * gn[:, sl])
    gla = jnp.concatenate(parts, axis=-1) * (og * _sigmoid(og))
    return jnp.concatenate([rg, gla], axis=-1).astype(BF16)


def _ffn_kernel(*refs, j, final, ff_chunks, with_mixer):
    if with_mixer:
        (x_ref, m_ref, g_ref, gr_ref, hf_ref, hb_ref, of_ref, ob_ref, og_ref, gn_ref, wo_ref,
         w1_ref, w3_ref, w2_ref, fg_ref, o_ref) = refs
        whole = len(hf_ref.shape) == 3
    else:
        x_ref, m_ref, g_ref, w1_ref, w3_ref, w2_ref, fg_ref, o_ref = refs
        whole = False
    m = m_ref[...]
    n_rows = x_ref.shape[0]

    def prologue(rows):
        x = x_ref[rows, :]
        if with_mixer:
            mix = _mixer_rows(rows, gr_ref, hf_ref, hb_ref, of_ref, ob_ref, og_ref, gn_ref[...])
            x = x + m[5:6] * jnp.dot(mix, wo_ref[...], preferred_element_type=F32)
        return x, _ada_norm(x, m, g_ref[...], j).astype(BF16)

    def matmul(operand):
        x, u = operand
        y = None
        for lo, hi in ff_chunks:
            h1 = jnp.dot(u, w1_ref[:, lo:hi], preferred_element_type=F32)
            h3 = jnp.dot(u, w3_ref[:, lo:hi], preferred_element_type=F32)
            act = (h1 * _sigmoid(h1) * h3).astype(BF16)
            part = jnp.dot(act, w2_ref[lo:hi, :], preferred_element_type=F32)
            y = part if y is None else y + part
        return x, y

    def epilogue(rows, acc):
        x, y = acc
        out = x + (0.5 * m[3 * j + 2:3 * j + 3]) * y
        if final:
            ms = jnp.mean(out * out, axis=-1, keepdims=True)
            out = out * lax.rsqrt(ms + EPS) * fg_ref[...]
        o_ref[rows, :] = out

    _staged([slice(0, n_rows)] if whole else _row_blocks(n_rows), prologue, matmul, epilogue)


def _ffn_call(h, mods, norm_g4, w1, w3, w2, final_g, *, l, half, is_ctx, final, mixer=None):
    bsz, n, d = h.shape
    dff = w1.shape[-1]
    tm = _token_block(n, FFN_TOKENS)
    nblk = n // tm
    j = 2 * half
    ff_chunks = tuple((lo, min(lo + FF_STEP, dff)) for lo in range(0, dff, FF_STEP))
    operands = [h, mods, norm_g4]
    in_specs = [_tok_spec(tm, d), _mod_spec(mods, l, is_ctx), _fixed_spec((None, None, 1, d), (l, j, 0, 0))]
    if mixer is not None:
        spb = mixer["seqs_per_block"]
        if spb:
            scan_spec = _interleaved_spec(spb, tm // (spb * SUBLANES), RG_WIDTH, nblk)
        else:
            scan_spec = _tok_spec(tm, RG_WIDTH)
        operands += [mixer[name] for name in ("gr", "hf", "hb", "o_f", "o_b", "og", "gnorm", "w_out")]
        in_specs += [_tok_spec(tm, RG_WIDTH), scan_spec, scan_spec,
                     _tok_spec(tm, GLA_VW), _tok_spec(tm, GLA_VW), _tok_spec(tm, GLA_VW),
                     _fixed_spec((None, 1, GLA_VW), (l, 0, 0)),
                     _fixed_spec((None,) + mixer["w_out"].shape[1:], (l, 0, 0))]
    operands += [w1, w3, w2, final_g]
    in_specs += [_fixed_spec((None, None, d, dff), (l, half, 0, 0)),
                 _fixed_spec((None, None, d, dff), (l, half, 0, 0)),
                 _fixed_spec((None, None, dff, d), (l, half, 0, 0)),
                 _full_spec(final_g)]
    return pl.pallas_call(
        functools.partial(_ffn_kernel, j=j, final=final, ff_chunks=ff_chunks, with_mixer=mixer is not None),
        grid=(bsz, nblk),
        in_specs=in_specs,
        out_specs=_tok_spec(tm, d),
        out_shape=jax.ShapeDtypeStruct((bsz, n, d), F32),
        compiler_params=_params(("parallel", "parallel")),
        name="mix_ffn" if mixer is not None else "ffn",
    )(*operands)


def _proj_kernel(x_ref, m_ref, g_ref, w_ref, wup_ref, bup_ref,
                 xr_ref, gr_ref, q_ref, k_ref, v_ref, og_ref, lg_ref):
    interleaved = len(xr_ref.shape) == 3
    m = m_ref[...]

    def prologue(rows):
        return _ada_norm(x_ref[rows, :], m, g_ref[...], 1).astype(BF16)

    def matmul(u):
        return jnp.dot(u, w_ref[...], preferred_element_type=F32)

    def epilogue(rows, p):
        o = 0
        if interleaved:
            _store_interleaved(xr_ref, p[:, o:o + RG_WIDTH])
        else:
            xr_ref[rows, :] = p[:, o:o + RG_WIDTH]
        o += RG_WIDTH
        gr_ref[rows, :] = p[:, o:o + RG_WIDTH]; o += RG_WIDTH
        q_ref[rows, :] = p[:, o:o + GLA_KW] * (GLA_DK ** -0.5); o += GLA_KW
        k_ref[rows, :] = p[:, o:o + GLA_KW]; o += GLA_KW
        v_ref[rows, :] = p[:, o:o + GLA_VW].astype(BF16); o += GLA_VW
        og_ref[rows, :] = p[:, o:o + GLA_VW]; o += GLA_VW
        lr = p[:, o:o + 2 * GLA_RANK].astype(BF16)
        z = jnp.dot(lr, wup_ref[...], preferred_element_type=F32) + bup_ref[...]
        lg_ref[rows, :] = _softplus(-z) * (-LOG2E / GLA_GATE_NORM)

    n_rows = x_ref.shape[0]
    _staged([slice(0, n_rows)] if interleaved else _row_blocks(n_rows), prologue, matmul, epilogue)


def _proj_call(h, mods, norm_g4, w_in, wup_bd, bup, *, l, is_ctx, seqs_per_block):
    bsz, n, d = h.shape
    tm = _token_block(n, TOKEN_BLOCK)
    nblk = n // tm
    outs = ((RG_WIDTH, F32), (GLA_KW, F32), (GLA_KW, F32), (GLA_VW, BF16), (GLA_VW, F32), (2 * GLA_KW, F32))
    if seqs_per_block:
        seg_len = tm // (seqs_per_block * SUBLANES)
        xr_shape = (bsz * nblk * seqs_per_block, seg_len, SUBLANES * RG_WIDTH)
        xr_spec = _interleaved_spec(seqs_per_block, seg_len, RG_WIDTH, nblk)
    else:
        xr_shape = (bsz, n, RG_WIDTH)
        xr_spec = _tok_spec(tm, RG_WIDTH)
    return pl.pallas_call(
        _proj_kernel,
        grid=(bsz, nblk),
        in_specs=[
            _tok_spec(tm, d),
            _mod_spec(mods, l, is_ctx),
            _fixed_spec((None, None, 1, d), (l, 1, 0, 0)),
            _fixed_spec((None,) + w_in.shape[1:], (l, 0, 0)),
            _fixed_spec((None,) + wup_bd.shape[1:], (l, 0, 0)),
            _fixed_spec((None, 1, bup.shape[-1]), (l, 0, 0)),
        ],
        out_specs=[xr_spec] + [_tok_spec(tm, w) for w, _ in outs],
        out_shape=[jax.ShapeDtypeStruct(xr_shape, F32)]
                  + [jax.ShapeDtypeStruct((bsz, n, w), dt) for w, dt in outs],
        compiler_params=_params(("parallel", "parallel")),
        name="proj",
    )(h, mods, norm_g4, w_in, wup_bd, bup)


def _rg_kernel(xf_ref, xfp_ref, xfn_ref, xb_ref, xbp_ref, xbn_ref, c0_ref, cw_ref, cb_ref, lam_ref, wg_ref,
               ba_ref, bi_ref, of_ref, ob_ref, cout_ref, carry_ref, *, nblk):
    for d, (x_ref, xp_ref, xn_ref, o_ref) in enumerate(((xf_ref, xfp_ref, xfn_ref, of_ref),
                                                         (xb_ref, xbp_ref, xbn_ref, ob_ref))):
        _rg_direction(x_ref, xp_ref, xn_ref, c0_ref.at[d], cw_ref, cb_ref, lam_ref.at[d], wg_ref.at[d],
                      ba_ref.at[d], bi_ref.at[d], o_ref, cout_ref.at[d], carry_ref.at[d],
                      reverse=bool(d), nblk=nblk)


def _rg_direction(x_ref, xp_ref, xn_ref, c0_ref, cw_ref, cb_ref, lam_ref, wg_ref, ba_ref, bi_ref,
                  o_ref, cout_ref, carry_ref, *, reverse, nblk):
    seg_len = x_ref.shape[0]
    j = pl.program_id(1)
    blk = (nblk - 1 - j) if reverse else j

    @pl.when(j == 0)
    def _():
        carry_ref[...] = c0_ref[...]

    x = x_ref[...]
    last = SUBLANES - 1
    prev2 = jnp.where(blk == 0, 0.0, xp_ref[last - 1, last:, :])
    prev1 = jnp.where(blk == 0, 0.0, xp_ref[last, last:, :])
    next1 = jnp.where(blk == nblk - 1, 0.0, xn_ref[0, 0:1, :])
    row8 = lax.broadcasted_iota(jnp.int32, (SUBLANES, 1), 0)

    def from_prev_segment(tile, halo):
        return jnp.where(row8 == 0, halo, pltpu.roll(tile, 1, axis=0))

    def from_next_segment(tile, halo):
        return jnp.where(row8 == last, halo, pltpu.roll(tile, last, axis=0))

    xm2 = from_prev_segment(x[seg_len - 2], prev2)
    xm1 = from_prev_segment(x[seg_len - 1], prev1)
    xp1 = from_next_segment(x[0], next1)
    xe = jnp.concatenate([xm2[None], xm1[None], x, xp1[None]], axis=0)
    cw = cw_ref[...]
    xc = xe[0:seg_len] * cw[0:1] + xe[1:seg_len + 1] * cw[1:2]
    xc = xc + xe[2:seg_len + 2] * cw[2:3]
    xc = xc + xe[3:seg_len + 3] * cw[3:4]
    xc = (xc + cb_ref[...]).reshape(seg_len * SUBLANES, RG_WIDTH)

    neg_c_sp = (-RG_C) * _softplus(-lam_ref[...])
    order = range(seg_len - 1, -1, -1) if reverse else range(seg_len)
    for g in range(RG_WIDTH // LANES):
        sl = slice(g * LANES, (g + 1) * LANES)
        xg = xc[:, sl]
        pre = jnp.dot(xg.astype(BF16), wg_ref[g], preferred_element_type=F32)
        r = _sigmoid(pre[:, :LANES] + ba_ref[:, sl])
        i = _sigmoid(pre[:, LANES:] + bi_ref[:, sl])
        log_a = r * neg_c_sp[:, sl]
        a = jnp.exp(log_a)
        bv = jnp.sqrt(1.0 - a * a) * (i * xg)
        hs, ps = [None] * seg_len, [None] * seg_len
        h_run = p_run = None
        for t in order:
            rows = slice(t * SUBLANES, (t + 1) * SUBLANES)
            if h_run is None:
                h_run, p_run = bv[rows], a[rows]
            else:
                h_run = a[rows] * h_run + bv[rows]
                p_run = a[rows] * p_run
            hs[t], ps[t] = h_run, p_run
        pc, hc = p_run, h_run
        s = 1
        while s < SUBLANES:
            shift = SUBLANES - s if reverse else s
            valid = (row8 < SUBLANES - s) if reverse else (row8 >= s)
            hc = jnp.where(valid, pc * pltpu.roll(hc, shift, axis=0) + hc, hc)
            pc = jnp.where(valid, pc * pltpu.roll(pc, shift, axis=0), pc)
            s *= 2
        carry = carry_ref[:, sl]
        seg_end = pc * carry + hc
        if reverse:
            seg_in = jnp.where(row8 == last, carry, pltpu.roll(seg_end, last, axis=0))
            carry_ref[:, sl] = seg_end[0:1]
        else:
            seg_in = jnp.where(row8 == 0, carry, pltpu.roll(seg_end, 1, axis=0))
            carry_ref[:, sl] = seg_end[last:]
        for t in range(seg_len):
            o_ref[t, :, sl] = hs[t] + ps[t] * seg_in

    @pl.when(j == nblk - 1)
    def _():
        cout_ref[...] = carry_ref[...]


def _rg_call(x4, carry0, conv_w, conv_b, lam, wg, ba, bi, *, l):
    bsz, seg_len, nseg, w = x4.shape
    nblk = nseg // SUBLANES
    tiles = seg_len // SUBLANES

    def block_specs(blk):
        return [
            pl.BlockSpec((None, seg_len, SUBLANES, w), lambda b, j: (b, 0, blk(j), 0)),
            pl.BlockSpec((None, SUBLANES, SUBLANES, w),
                         lambda b, j: (b, tiles - 1, jnp.maximum(blk(j) - 1, 0), 0)),
            pl.BlockSpec((None, SUBLANES, SUBLANES, w),
                         lambda b, j: (b, 0, jnp.minimum(blk(j) + 1, nblk - 1), 0)),
        ]

    fwd = block_specs(lambda j: j)
    bwd = block_specs(lambda j: nblk - 1 - j)
    both_dirs = _fixed_spec((None, 2, 1, w), (l, 0, 0, 0))
    state_spec = pl.BlockSpec((2, None, 1, w), lambda b, j: (0, b, 0, 0))
    return pl.pallas_call(
        functools.partial(_rg_kernel, nblk=nblk),
        grid=(bsz, nblk),
        in_specs=fwd + bwd + [
            state_spec,
            _fixed_spec((None,) + conv_w.shape[1:], (l, 0, 0)),
            _fixed_spec((None, 1, w), (l, 0, 0)),
            both_dirs,
            _fixed_spec((None,) + wg.shape[1:], (l, 0, 0, 0, 0)),
            both_dirs,
            both_dirs,
        ],
        out_specs=[fwd[0], bwd[0], state_spec],
        out_shape=[jax.ShapeDtypeStruct(x4.shape, F32)] * 2 + [jax.ShapeDtypeStruct(carry0.shape, F32)],
        scratch_shapes=[pltpu.VMEM((2, 1, w), F32)],
        compiler_params=_params(("arbitrary", "arbitrary")),
        name="rg",
    )(x4, x4, x4, x4, x4, x4, carry0, conv_w, conv_b, lam, wg, ba, bi)


def _level_ref(b, m, r):
    c, w = b.shape
    if m >= SUBLANES:
        b3 = b.reshape(c // m, m, w)
        return jnp.broadcast_to(b3[:, r:r + 1, :], (c // m, m, w)).reshape(c, w)
    bt = b.reshape(c // SUBLANES, SUBLANES, w)
    sub = lax.broadcasted_iota(jnp.int32, (1, SUBLANES, 1), 1)
    out = None
    for blk in range(SUBLANES // m):
        src = blk * m + r
        piece = jnp.broadcast_to(bt[:, src:src + 1, :], bt.shape)
        out = piece if out is None else jnp.where(sub >= blk * m, piece, out)
    return out.reshape(c, w)


def _halves_select(m, q_rows, q, k, reverse):
    half = m // 2
    if half % SUBLANES:
        return jnp.where(q_rows, q, k)
    first, second = (q, k) if reverse else (k, q)
    pieces = []
    for lo in range(0, q.shape[0], m):
        pieces += [first[lo:lo + half], second[lo + half:lo + m]]
    return jnp.concatenate(pieces, axis=0)


def _dot_nt(a, b):
    return lax.dot_general(a, b, (((1,), (1,)), ((), ())), preferred_element_type=F32)


def _dot_tn(a, b):
    return lax.dot_general(a, b, (((0,), (0,)), ((), ())), preferred_element_type=F32)


def _gla_consts(reverse):
    c = GLA_CHUNK
    row = lax.broadcasted_iota(jnp.int32, (c, 1), 0)
    r2 = lax.broadcasted_iota(jnp.int32, (c, c), 0)
    c2 = lax.broadcasted_iota(jnp.int32, (c, c), 1)
    tri = jnp.where((c2 >= r2) if reverse else (c2 <= r2), 1.0, 0.0).astype(BF16)
    klane = lax.broadcasted_iota(jnp.int32, (1, LANES), 1)
    vlane = lax.broadcasted_iota(jnp.int32, (1, PAIR_VW), 1)
    s = lax.broadcasted_iota(jnp.int32, (2 * c, c), 0) % c
    t = lax.broadcasted_iota(jnp.int32, (2 * c, c), 1)
    levels = []
    m = c
    while m >= 2:
        half = m // 2
        late = (row % m) >= half
        q_rows = ~late if reverse else late
        same = (t // m) == (s // m)
        t_late = (t % m) >= half
        s_late = (s % m) >= half
        mask = same & (~t_late) & s_late if reverse else same & t_late & (~s_late)
        levels.append((m, q_rows, mask))
        m = half
    key_first = klane < GLA_DK
    srow = lax.broadcasted_iota(jnp.int32, (PAIR_VW, 1), 0)
    state_mask = (srow < GLA_DV) == key_first
    return dict(tri=tri, key_first=key_first, val_first=vlane < GLA_DV, state_mask=state_mask,
                levels=levels, diag=(t == s))


def _pair_blockdiag_rows(x, first_mask):
    zero = jnp.zeros_like(x)
    return jnp.concatenate([jnp.where(first_mask, x, zero), jnp.where(first_mask, zero, x)], axis=0)


def _gla_stages(streams):
    c = GLA_CHUNK
    pair_k = [slice(p * LANES, (p + 1) * LANES) for p in range(GLA_PAIRS)]
    pair_v = [slice(p * PAIR_VW, (p + 1) * PAIR_VW) for p in range(GLA_PAIRS)]

    def cumulative_decay():
        for s in streams:
            g, tri = s["g"], s["consts"]["tri"]
            g1 = g.astype(BF16)
            g2 = (g - g1.astype(F32)).astype(BF16)
            g3 = (g - g1.astype(F32) - g2.astype(F32)).astype(BF16)
            s["b"] = (jnp.dot(tri, g1, preferred_element_type=F32)
                      + jnp.dot(tri, g2, preferred_element_type=F32)
                      + jnp.dot(tri, g3, preferred_element_type=F32))

    def diagonal():
        for s in streams:
            kf = s["consts"]["key_first"]
            qb, kb = s["q"].astype(BF16), s["k"].astype(BF16)
            s["att"] = [jnp.where(s["consts"]["diag"],
                                  _dot_nt(_pair_blockdiag_rows(kb[:, kl], kf), qb[:, kl]), 0.0) for kl in pair_k]

    def level(lvl):
        for s in streams:
            m, q_rows, mask = s["consts"]["levels"][lvl]
            half = m // 2
            if m == 2:
                x = jnp.where(q_rows, s["q"] * jnp.exp2(s["g"]), s["k"])
            else:
                ref = _level_ref(s["b"], m, half if s["reverse"] else half - 1)
                x = jnp.exp2(-jnp.abs(s["b"] - ref)) * _halves_select(m, q_rows, s["q"], s["k"], s["reverse"])
            x = x.astype(BF16)
            kf = s["consts"]["key_first"]
            s["att"] = [jnp.where(mask, _dot_nt(_pair_blockdiag_rows(x[:, kl], kf), x[:, kl]), att)
                        for kl, att in zip(pair_k, s["att"])]

    def state_operands():
        for s in streams:
            b = s["b"]
            b_last = b[0:1] if s["reverse"] else b[c - 1:c]
            s["q_in"] = (s["q"] * jnp.exp2(b)).astype(BF16)
            s["k_hat"] = (s["k"] * jnp.exp2(b_last - b)).astype(BF16)
            s["decay"] = jnp.exp2(b_last)
            s["att"] = [att.astype(BF16) for att in s["att"]]

    def output_and_update(s):
        vf = s["consts"]["val_first"]
        for p, (kl, vl) in enumerate(zip(pair_k, pair_v)):
            vp = s["v"][:, vl]
            st = s["st_ref"][p]
            o = _dot_tn(s["att"][p], _pair_blockdiag_rows(vp, vf))
            o = o + _dot_nt(s["q_in"][:, kl], st.astype(BF16))
            s["o_ref"][:, vl] = o
            upd = _dot_tn(vp, s["k_hat"][:, kl])
            s["st_ref"][p] = st * s["decay"][:, kl] + jnp.where(s["consts"]["state_mask"], upd, 0.0)

    n_levels = len(streams[0]["consts"]["levels"])
    prep = [cumulative_decay, diagonal] + [functools.partial(level, lvl) for lvl in range(n_levels)]
    prep.append(state_operands)
    tail = [functools.partial(output_and_update, s) for s in streams]
    return prep, tail


def _interleave(first, second):
    done = 0
    for idx, thunk in enumerate(first):
        thunk()
        upto = (len(second) * (idx + 1)) // len(first)
        for extra in second[done:upto]:
            extra()
        done = upto


def _gla_kernel(qf_ref, kf_ref, vf_ref, gf_ref, qb_ref, kb_ref, vb_ref, gb_ref, s0_ref,
                of_ref, ob_ref, sout_ref, st_ref, *, chunks):
    @pl.when(pl.program_id(0) == 0)
    def _():
        st_ref[...] = s0_ref[...]

    bsz = qf_ref.shape[0]
    dirs = ((0, qf_ref, kf_ref, vf_ref, gf_ref, of_ref), (1, qb_ref, kb_ref, vb_ref, gb_ref, ob_ref))
    consts = [_gla_consts(reverse=False), _gla_consts(reverse=True)]

    def chunk_stages(ci):
        streams = []
        for d, q_ref, k_ref, v_ref, g_ref, o_ref in dirs:
            rows = pl.ds(((chunks - 1 - ci) if d else ci) * GLA_CHUNK, GLA_CHUNK)
            for bi in range(bsz):
                streams.append(dict(q=q_ref[bi, rows, :], k=k_ref[bi, rows, :], v=v_ref[bi, rows, :],
                                    g=g_ref[bi, rows, :], st_ref=st_ref.at[d, bi], o_ref=o_ref.at[bi, rows],
                                    consts=consts[d], reverse=bool(d)))
        return _gla_stages(streams)

    prep, tail = chunk_stages(0)
    for thunk in prep:
        thunk()
    for ci in range(chunks):
        if ci + 1 < chunks:
            next_prep, next_tail = chunk_stages(ci + 1)
            _interleave(next_prep, tail)
            tail = next_tail
        else:
            for thunk in tail:
                thunk()

    @pl.when(pl.program_id(0) == pl.num_programs(0) - 1)
    def _():
        sout_ref[...] = st_ref[...]


def _gla_call(q, k, v, lg, state0):
    bsz, n, kw = q.shape
    vw = v.shape[2]
    chunks = min(GLA_CHUNKS_PER_STEP, n // GLA_CHUNK)
    br = chunks * GLA_CHUNK
    nstep = n // br

    def fwd(i):
        return (0, i, 0)

    def bwd(i):
        return (0, nstep - 1 - i, 0)

    return pl.pallas_call(
        functools.partial(_gla_kernel, chunks=chunks),
        grid=(nstep,),
        in_specs=[
            pl.BlockSpec((bsz, br, kw), fwd),
            pl.BlockSpec((bsz, br, kw), fwd),
            pl.BlockSpec((bsz, br, vw), fwd),
            pl.BlockSpec((bsz, br, kw), fwd),
            pl.BlockSpec((bsz, br, kw), bwd),
            pl.BlockSpec((bsz, br, kw), bwd),
            pl.BlockSpec((bsz, br, vw), bwd),
            pl.BlockSpec((bsz, br, kw), lambda i: (0, nstep - 1 - i, 1)),
            _full_spec(state0),
        ],
        out_specs=[pl.BlockSpec((bsz, br, vw), fwd), pl.BlockSpec((bsz, br, vw), bwd),
                   pl.BlockSpec(state0.shape, lambda i: (0,) * state0.ndim)],
        out_shape=[jax.ShapeDtypeStruct((bsz, n, vw), F32)] * 2
                  + [jax.ShapeDtypeStruct(state0.shape, F32)],
        scratch_shapes=[pltpu.VMEM(state0.shape, F32)],
        compiler_params=_params(("arbitrary",)),
        name="gla",
    )(q, k, v, lg, q, k, v, lg, state0)


def _pair_blockdiag(w):
    hd = w.shape[-1]
    per = LANES // hd
    lead = w.shape[:-3]
    wg = w.reshape(lead + (RG_HEADS // per, per, hd, hd))
    eye = jnp.eye(per, dtype=w.dtype)
    return jnp.einsum('...gpij,pq->...gpiqj', wg, eye).reshape(lead + (RG_HEADS // per, LANES, LANES))


def kernel(x, c, ctx, c_ctx, w_mod, b_mod, norm_g, ffn_w1, ffn_w3, ffn_w2, w_in, conv_w, conv_b,
           rg_lam, rg_wa, rg_ba, rg_wi, rg_bi, gla_wup, gla_bup, gla_norm_g, w_out, final_g):
    bsz, t, d = x.shape
    n_ctx = ctx.shape[1]
    depth = w_mod.shape[0]
    assert t % TOKEN_BLOCK == 0 and t % FFN_TOKENS == 0 and t % (GRID_W * SUBLANES) == 0
    assert n_ctx % (SUBLANES * SUBLANES) == 0 and n_ctx % GLA_CHUNK == 0 and bsz * n_ctx <= TOKEN_BLOCK

    rows = -(-(bsz + 1) // SUBLANES) * SUBLANES
    cvec = jnp.zeros((rows, d), F32).at[0].set(c_ctx).at[1:bsz + 1].set(c)
    mods = _mod_call(cvec, w_mod, b_mod).reshape(depth, rows, N_MOD, d)

    w1, w3, w2 = ffn_w1.astype(BF16), ffn_w3.astype(BF16), ffn_w2.astype(BF16)
    w_in_b, w_out_b = w_in.astype(BF16), w_out.astype(BF16)
    norm_g4 = norm_g.reshape(depth, 3, 1, d)
    zero = jnp.zeros((depth, GLA_RANK, GLA_KW), F32)
    wup_bd = jnp.concatenate([jnp.concatenate([gla_wup[:, 0], zero], axis=2),
                              jnp.concatenate([zero, gla_wup[:, 1]], axis=2)], axis=1).astype(BF16)
    bup = gla_bup.reshape(depth, 1, 2 * GLA_KW)
    wg = jnp.concatenate([_pair_blockdiag(rg_wa), _pair_blockdiag(rg_wi)], axis=-1).astype(BF16)
    conv_b3 = conv_b.reshape(depth, 1, RG_WIDTH)
    lam4 = rg_lam.reshape(depth, 2, 1, RG_WIDTH)
    ba4 = rg_ba.reshape(depth, 2, 1, RG_WIDTH)
    bi4 = rg_bi.reshape(depth, 2, 1, RG_WIDTH)
    gnorm = gla_norm_g.reshape(depth, 1, GLA_VW)
    fg = final_g.reshape(1, d)

    h_c = ctx.reshape(1, bsz * n_ctx, d)
    grid_rows = t // GRID_W
    h_l = x.reshape(bsz, grid_rows, GRID_W, d).swapaxes(1, 2).reshape(bsz, t, d)
    ctx_seg = n_ctx // SUBLANES
    for l in range(depth):
        last = l == depth - 1
        ffn = functools.partial(_ffn_call, mods=mods, norm_g4=norm_g4, w1=w1, w3=w3, w2=w2, final_g=fg, l=l)
        h_l = ffn(h_l, half=0, is_ctx=False, final=False)
        h_c = ffn(h_c, half=0, is_ctx=True, final=False)

        proj = functools.partial(_proj_call, mods=mods, norm_g4=norm_g4, w_in=w_in_b, wup_bd=wup_bd,
                                 bup=bup, l=l)
        xr_l, gr_l, q_l, k_l, v_l, og_l, lg_l = proj(h_l, is_ctx=False, seqs_per_block=0)
        xr_c, gr_c, q_c, k_c, v_c, og_c, lg_c = proj(h_c, is_ctx=True, seqs_per_block=bsz)

        rg = functools.partial(_rg_call, conv_w=conv_w, conv_b=conv_b3, lam=lam4, wg=wg, ba=ba4, bi=bi4, l=l)
        *scans_c, carry = rg(xr_c.reshape(bsz, ctx_seg, SUBLANES, RG_WIDTH),
                             jnp.zeros((2, bsz, 1, RG_WIDTH), F32))
        *scans_l, _ = rg(xr_l.reshape(bsz, GRID_W, grid_rows, RG_WIDTH), carry)
        scans_c = [a.reshape(xr_c.shape) for a in scans_c]
        scans_l = [a.reshape(xr_l.shape) for a in scans_l]

        per_seq = lambda a: a.reshape(bsz, n_ctx, a.shape[-1])
        state0 = jnp.zeros((2, bsz, GLA_PAIRS, PAIR_VW, LANES), F32)
        oc_f, oc_b, state = _gla_call(per_seq(q_c), per_seq(k_c), per_seq(v_c), per_seq(lg_c), state0)
        ol_f, ol_b, _ = _gla_call(q_l, k_l, v_l, lg_l, state)

        mixer_l = dict(gr=gr_l, hf=scans_l[0], hb=scans_l[1], o_f=ol_f, o_b=ol_b, og=og_l, gnorm=gnorm,
                       w_out=w_out_b, seqs_per_block=0)
        h_l = ffn(h_l, half=1, is_ctx=False, final=last, mixer=mixer_l)
        if not last:
            flat = lambda a: a.reshape(1, bsz * n_ctx, a.shape[-1])
            mixer_c = dict(gr=gr_c, hf=scans_c[0], hb=scans_c[1], o_f=flat(oc_f), o_b=flat(oc_b), og=og_c,
                           gnorm=gnorm, w_out=w_out_b, seqs_per_block=bsz)
            h_c = ffn(h_c, half=1, is_ctx=True, final=False, mixer=mixer_c)
    return h_l.reshape(bsz, GRID_W, grid_rows, d).swapaxes(1, 2).reshape(bsz, t, d)
```

```python
import functools
import math

import jax
import jax.numpy as jnp
from jax import lax
from jax.experimental import pallas as pl
from jax.experimental.pallas import tpu as pltpu

F32 = jnp.float32
BF16 = jnp.bfloat16

LANES = 128
SUBLANES = 8
FFN_TOKENS = 512
TOKEN_BLOCK = 1024
SUB_ROWS = 256
FF_STEP = 1024
GLA_CHUNKS_PER_STEP = 8
VMEM_LIMIT = 56 * 1024 * 1024

N_MOD = 9
EPS = 1e-6
GRID_W = 64
RG_WIDTH = 512
RG_HEADS = 8
RG_C = 8.0
GLA_HEADS = 4
GLA_DK = 64
GLA_DV = 128
GLA_KW = GLA_HEADS * GLA_DK
GLA_VW = GLA_HEADS * GLA_DV
GLA_RANK = 16
GLA_GATE_NORM = 16.0
GLA_CHUNK = 64
GLA_PAIRS = GLA_KW // LANES
PAIR_VW = GLA_VW // GLA_PAIRS
LOG2E = math.log2(math.e)


def _params(sem):
    return pltpu.CompilerParams(dimension_semantics=sem, vmem_limit_bytes=VMEM_LIMIT)


def _fixed_spec(shape, index):
    return pl.BlockSpec(shape, lambda *_: index, pipeline_mode=pl.Buffered(1))


def _full_spec(arr):
    return _fixed_spec(arr.shape, (0,) * arr.ndim)


def _sigmoid(x):
    return 1.0 / (1.0 + jnp.exp(-x))


def _softplus(x):
    return jnp.maximum(x, 0.0) + jnp.log(1.0 + jnp.exp(-jnp.abs(x)))


def _ada_norm(x, m, g, j):
    ms = jnp.mean(x * x, axis=-1, keepdims=True)
    gain = g * (1.0 + m[3 * j + 1:3 * j + 2])
    return x * lax.rsqrt(ms + EPS) * gain + m[3 * j:3 * j + 1]


def _token_block(n, target):
    return min(n, target)


def _row_blocks(n_rows):
    step = min(SUB_ROWS, n_rows)
    return [slice(r, r + step) for r in range(0, n_rows, step)]


def _staged(blocks, prologue, matmul, epilogue):
    operand = prologue(blocks[0])
    pending = None
    for i, rows in enumerate(blocks):
        acc = matmul(operand)
        if i + 1 < len(blocks):
            operand = prologue(blocks[i + 1])
        if pending is not None:
            epilogue(*pending)
        pending = (rows, acc)
    epilogue(*pending)


def _mod_kernel(c_ref, w_ref, b_ref, o_ref):
    c = c_ref[...]
    sc = (c * _sigmoid(c)).astype(BF16)
    o_ref[...] = jnp.dot(sc, w_ref[...].astype(BF16), preferred_element_type=F32) + b_ref[...]


def _mod_call(cvec, w_mod, b_mod):
    depth, d, n = w_mod.shape
    rows = cvec.shape[0]
    tn = n // 4
    return pl.pallas_call(
        _mod_kernel,
        grid=(depth, n // tn),
        in_specs=[
            pl.BlockSpec((rows, d), lambda l, i: (0, 0)),
            pl.BlockSpec((None, d, tn), lambda l, i: (l, 0, i)),
            pl.BlockSpec((None, 1, tn), lambda l, i: (l, 0, i)),
        ],
        out_specs=pl.BlockSpec((None, rows, tn), lambda l, i: (l, 0, i)),
        out_shape=jax.ShapeDtypeStruct((depth, rows, n), F32),
        compiler_params=_params(("parallel", "parallel")),
        name="mod",
    )(cvec, w_mod, b_mod.reshape(depth, 1, n))


def _mod_spec(mods, l, is_ctx):
    d = mods.shape[-1]
    if is_ctx:
        return pl.BlockSpec((None, None, N_MOD, d), lambda b, i: (l, 0, 0, 0))
    return pl.BlockSpec((None, None, N_MOD, d), lambda b, i: (l, b + 1, 0, 0))


def _tok_spec(tm, w):
    return pl.BlockSpec((None, tm, w), lambda b, i: (b, i, 0))


def _store_interleaved(ref, val):
    nb, seg_len, _ = ref.shape
    w = val.shape[1]
    for bb in range(nb):
        for s in range(SUBLANES):
            r0 = (bb * SUBLANES + s) * seg_len
            ref[bb, :, s * w:(s + 1) * w] = val[r0:r0 + seg_len]


def _load_interleaved(ref):
    nb, _, sw = ref.shape
    w = sw // SUBLANES
    return jnp.concatenate([ref[bb, :, s * w:(s + 1) * w]
                            for bb in range(nb) for s in range(SUBLANES)], axis=0)


def _interleaved_spec(nb, seg_len, w, nblk):
    return pl.BlockSpec((nb, seg_len, SUBLANES * w), lambda b, i: (b * nblk + i, 0, 0))


def _mixer_rows(rows, gr_ref, hf_ref, hb_ref, of_ref, ob_ref, og_ref, gn):
    gr = gr_ref[rows, :]
    gelu = 0.5 * gr * (1.0 + jnp.tanh(0.7978845608028654 * (gr + 0.044715 * (gr * gr * gr))))
    if len(hf_ref.shape) == 3:
        rg = gelu * (_load_interleaved(hf_ref) + _load_interleaved(hb_ref))
    else:
        rg = gelu * (hf_ref[rows, :] + hb_ref[rows, :])
    o = of_ref[rows, :] + ob_ref[rows, :]
    og = og_ref[rows, :]
    parts = []
    for hd in range(GLA_HEADS):
        sl = slice(hd * GLA_DV, (hd + 1) * GLA_DV)
        oh = o[:, sl]
        ms = jnp.mean(oh * oh, axis=-1, keepdims=True)
        parts.append(oh * lax.rsqrt(ms + EPS) * gn[:, sl])
    gla = jnp.concatenate(parts, axis=-1) * (og * _sigmoid(og))
    return jnp.concatenate([rg, gla], axis=-1).astype(BF16)


def _ffn_kernel(*refs, j, final, ff_chunks, with_mixer):
    if with_mixer:
        (x_ref, m_ref, g_ref, gr_ref, hf_ref, hb_ref, of_ref, ob_ref, og_ref, gn_ref, wo_ref,
         w1_ref, w3_ref, w2_ref, fg_ref, o_ref) = refs
        whole = len(hf_ref.shape) == 3
    else:
        x_ref, m_ref, g_ref, w1_ref, w3_ref, w2_ref, fg_ref, o_ref = refs
        whole = False
    m = m_ref[...]
    n_rows = x_ref.shape[0]

    def prologue(rows):
        x = x_ref[rows, :]
        if with_mixer:
            mix = _mixer_rows(rows, gr_ref, hf_ref, hb_ref, of_ref, ob_ref, og_ref, gn_ref[...])
            x = x + m[5:6] * jnp.dot(mix, wo_ref[...], preferred_element_type=F32)
        return x, _ada_norm(x, m, g_ref[...], j).astype(BF16)

    def matmul(operand):
        x, u = operand
        y = None
        for lo, hi in ff_chunks:
            h1 = jnp.dot(u, w1_ref[:, lo:hi], preferred_element_type=F32)
            h3 = jnp.dot(u, w3_ref[:, lo:hi], preferred_element_type=F32)
            act = (h1 * _sigmoid(h1) * h3).astype(BF16)
            part = jnp.dot(act, w2_ref[lo:hi, :], preferred_element_type=F32)
            y = part if y is None else y + part
        return x, y

    def epilogue(rows, acc):
        x, y = acc
        out = x + (0.5 * m[3 * j + 2:3 * j + 3]) * y
        if final:
            ms = jnp.mean(out * out, axis=-1, keepdims=True)
            out = out * lax.rsqrt(ms + EPS) * fg_ref[...]
        o_ref[rows, :] = out

    _staged([slice(0, n_rows)] if whole else _row_blocks(n_rows), prologue, matmul, epilogue)


def _ffn_call(h, mods, norm_g4, w1, w3, w2, final_g, *, l, half, is_ctx, final, mixer=None):
    bsz, n, d = h.shape
    dff = w1.shape[-1]
    tm = _token_block(n, FFN_TOKENS)
    nblk = n // tm
    j = 2 * half
    ff_chunks = tuple((lo, min(lo + FF_STEP, dff)) for lo in range(0, dff, FF_STEP))
    operands = [h, mods, norm_g4]
    in_specs = [_tok_spec(tm, d), _mod_spec(mods, l, is_ctx), _fixed_spec((None, None, 1, d), (l, j, 0, 0))]
    if mixer is not None:
        spb = mixer["seqs_per_block"]
        if spb:
            scan_spec = _interleaved_spec(spb, tm // (spb * SUBLANES), RG_WIDTH, nblk)
        else:
            scan_spec = _tok_spec(tm, RG_WIDTH)
        operands += [mixer[name] for name in ("gr", "hf", "hb", "o_f", "o_b", "og", "gnorm", "w_out")]
        in_specs += [_tok_spec(tm, RG_WIDTH), scan_spec, scan_spec,
                     _tok_spec(tm, GLA_VW), _tok_spec(tm, GLA_VW), _tok_spec(tm, GLA_VW),
                     _fixed_spec((None, 1, GLA_VW), (l, 0, 0)),
                     _fixed_spec((None,) + mixer["w_out"].shape[1:], (l, 0, 0))]
    operands += [w1, w3, w2, final_g]
    in_specs += [_fixed_spec((None, None, d, dff), (l, half, 0, 0)),
                 _fixed_spec((None, None, d, dff), (l, half, 0, 0)),
                 _fixed_spec((None, None, dff, d), (l, half, 0, 0)),
                 _full_spec(final_g)]
    return pl.pallas_call(
        functools.partial(_ffn_kernel, j=j, final=final, ff_chunks=ff_chunks, with_mixer=mixer is not None),
        grid=(bsz, nblk),
        in_specs=in_specs,
        out_specs=_tok_spec(tm, d),
        out_shape=jax.ShapeDtypeStruct((bsz, n, d), F32),
        compiler_params=_params(("parallel", "parallel")),
        name="mix_ffn" if mixer is not None else "ffn",
    )(*operands)


def _proj_kernel(x_ref, m_ref, g_ref, w_ref, wup_ref, bup_ref,
                 xr_ref, gr_ref, q_ref, k_ref, v_ref, og_ref, lg_ref):
    interleaved = len(xr_ref.shape) == 3
    m = m_ref[...]

    def prologue(rows):
        return _ada_norm(x_ref[rows, :], m, g_ref[...], 1).astype(BF16)

    def matmul(u):
        return jnp.dot(u, w_ref[...], preferred_element_type=F32)

    def epilogue(rows, p):
        o = 0
        if interleaved:
            _store_interleaved(xr_ref, p[:, o:o + RG_WIDTH])
        else:
            xr_ref[rows, :] = p[:, o:o + RG_WIDTH]
        o += RG_WIDTH
        gr_ref[rows, :] = p[:, o:o + RG_WIDTH]; o += RG_WIDTH
        q_ref[rows, :] = p[:, o:o + GLA_KW] * (GLA_DK ** -0.5); o += GLA_KW
        k_ref[rows, :] = p[:, o:o + GLA_KW]; o += GLA_KW
        v_ref[rows, :] = p[:, o:o + GLA_VW].astype(BF16); o += GLA_VW
        og_ref[rows, :] = p[:, o:o + GLA_VW]; o += GLA_VW
        lr = p[:, o:o + 2 * GLA_RANK].astype(BF16)
        z = jnp.dot(lr, wup_ref[...], preferred_element_type=F32) + bup_ref[...]
        lg_ref[rows, :] = _softplus(-z) * (-LOG2E / GLA_GATE_NORM)

    n_rows = x_ref.shape[0]
    _staged([slice(0, n_rows)] if interleaved else _row_blocks(n_rows), prologue, matmul, epilogue)


def _proj_call(h, mods, norm_g4, w_in, wup_bd, bup, *, l, is_ctx, seqs_per_block):
    bsz, n, d = h.shape
    tm = _token_block(n, TOKEN_BLOCK)
    nblk = n // tm
    outs = ((RG_WIDTH, F32), (GLA_KW, F32), (GLA_KW, F32), (GLA_VW, BF16), (GLA_VW, F32), (2 * GLA_KW, F32))
    if seqs_per_block:
        seg_len = tm // (seqs_per_block * SUBLANES)
        xr_shape = (bsz * nblk * seqs_per_block, seg_len, SUBLANES * RG_WIDTH)
        xr_spec = _interleaved_spec(seqs_per_block, seg_len, RG_WIDTH, nblk)
    else:
        xr_shape = (bsz, n, RG_WIDTH)
        xr_spec = _tok_spec(tm, RG_WIDTH)
    return pl.pallas_call(
        _proj_kernel,
        grid=(bsz, nblk),
        in_specs=[
            _tok_spec(tm, d),
            _mod_spec(mods, l, is_ctx),
            _fixed_spec((None, None, 1, d), (l, 1, 0, 0)),
            _fixed_spec((None,) + w_in.shape[1:], (l, 0, 0)),
            _fixed_spec((None,) + wup_bd.shape[1:], (l, 0, 0)),
            _fixed_spec((None, 1, bup.shape[-1]), (l, 0, 0)),
        ],
        out_specs=[xr_spec] + [_tok_spec(tm, w) for w, _ in outs],
        out_shape=[jax.ShapeDtypeStruct(xr_shape, F32)]
                  + [jax.ShapeDtypeStruct((bsz, n, w), dt) for w, dt in outs],
        compiler_params=_params(("parallel", "parallel")),
        name="proj",
    )(h, mods, norm_g4, w_in, wup_bd, bup)


def _rg_kernel(xf_ref, xfp_ref, xfn_ref, xb_ref, xbp_ref, xbn_ref, c0_ref, cw_ref, cb_ref, lam_ref, wg_ref,
               ba_ref, bi_ref, of_ref, ob_ref, cout_ref, carry_ref, *, nblk):
    for d, (x_ref, xp_ref, xn_ref, o_ref) in enumerate(((xf_ref, xfp_ref, xfn_ref, of_ref),
                                                         (xb_ref, xbp_ref, xbn_ref, ob_ref))):
        _rg_direction(x_ref, xp_ref, xn_ref, c0_ref.at[d], cw_ref, cb_ref, lam_ref.at[d], wg_ref.at[d],
                      ba_ref.at[d], bi_ref.at[d], o_ref, cout_ref.at[d], carry_ref.at[d],
                      reverse=bool(d), nblk=nblk)


def _rg_direction(x_ref, xp_ref, xn_ref, c0_ref, cw_ref, cb_ref, lam_ref, wg_ref, ba_ref, bi_ref,
                  o_ref, cout_ref, carry_ref, *, reverse, nblk):
    seg_len = x_ref.shape[0]
    j = pl.program_id(1)
    blk = (nblk - 1 - j) if reverse else j

    @pl.when(j == 0)
    def _():
        carry_ref[...] = c0_ref[...]

    x = x_ref[...]
    last = SUBLANES - 1
    prev2 = jnp.where(blk == 0, 0.0, xp_ref[last - 1, last:, :])
    prev1 = jnp.where(blk == 0, 0.0, xp_ref[last, last:, :])
    next1 = jnp.where(blk == nblk - 1, 0.0, xn_ref[0, 0:1, :])
    row8 = lax.broadcasted_iota(jnp.int32, (SUBLANES, 1), 0)

    def from_prev_segment(tile, halo):
        return jnp.where(row8 == 0, halo, pltpu.roll(tile, 1, axis=0))

    def from_next_segment(tile, halo):
        return jnp.where(row8 == last, halo, pltpu.roll(tile, last, axis=0))

    xm2 = from_prev_segment(x[seg_len - 2], prev2)
    xm1 = from_prev_segment(x[seg_len - 1], prev1)
    xp1 = from_next_segment(x[0], next1)
    xe = jnp.concatenate([xm2[None], xm1[None], x, xp1[None]], axis=0)
    cw = cw_ref[...]
    xc = xe[0:seg_len] * cw[0:1] + xe[1:seg_len + 1] * cw[1:2]
    xc = xc + xe[2:seg_len + 2] * cw[2:3]
    xc = xc + xe[3:seg_len + 3] * cw[3:4]
    xc = (xc + cb_ref[...]).reshape(seg_len * SUBLANES, RG_WIDTH)

    neg_c_sp = (-RG_C) * _softplus(-lam_ref[...])
    order = range(seg_len - 1, -1, -1) if reverse else range(seg_len)
    for g in range(RG_WIDTH // LANES):
        sl = slice(g * LANES, (g + 1) * LANES)
        xg = xc[:, sl]
        pre = jnp.dot(xg.astype(BF16), wg_ref[g], preferred_element_type=F32)
        r = _sigmoid(pre[:, :LANES] + ba_ref[:, sl])
        i = _sigmoid(pre[:, LANES:] + bi_ref[:, sl])
        log_a = r * neg_c_sp[:, sl]
        a = jnp.exp(log_a)
        bv = jnp.sqrt(1.0 - a * a) * (i * xg)
        hs, ps = [None] * seg_len, [None] * seg_len
        h_run = p_run = None
        for t in order:
            rows = slice(t * SUBLANES, (t + 1) * SUBLANES)
            if h_run is None:
                h_run, p_run = bv[rows], a[rows]
            else:
                h_run = a[rows] * h_run + bv[rows]
                p_run = a[rows] * p_run
            hs[t], ps[t] = h_run, p_run
        pc, hc = p_run, h_run
        s = 1
        while s < SUBLANES:
            shift = SUBLANES - s if reverse else s
            valid = (row8 < SUBLANES - s) if reverse else (row8 >= s)
            hc = jnp.where(valid, pc * pltpu.roll(hc, shift, axis=0) + hc, hc)
            pc = jnp.where(valid, pc * pltpu.roll(pc, shift, axis=0), pc)
            s *= 2
        carry = carry_ref[:, sl]
        seg_end = pc * carry + hc
        if reverse:
            seg_in = jnp.where(row8 == last, carry, pltpu.roll(seg_end, last, axis=0))
            carry_ref[:, sl] = seg_end[0:1]
        else:
            seg_in = jnp.where(row8 == 0, carry, pltpu.roll(seg_end, 1, axis=0))
            carry_ref[:, sl] = seg_end[last:]
        for t in range(seg_len):
            o_ref[t, :, sl] = hs[t] + ps[t] * seg_in

    @pl.when(j == nblk - 1)
    def _():
        cout_ref[...] = carry_ref[...]


def _rg_call(x4, carry0, conv_w, conv_b, lam, wg, ba, bi, *, l):
    bsz, seg_len, nseg, w = x4.shape
    nblk = nseg // SUBLANES
    tiles = seg_len // SUBLANES

    def block_specs(blk):
        return [
            pl.BlockSpec((None, seg_len, SUBLANES, w), lambda b, j: (b, 0, blk(j), 0)),
            pl.BlockSpec((None, SUBLANES, SUBLANES, w),
                         lambda b, j: (b, tiles - 1, jnp.maximum(blk(j) - 1, 0), 0)),
            pl.BlockSpec((None, SUBLANES, SUBLANES, w),
                         lambda b, j: (b, 0, jnp.minimum(blk(j) + 1, nblk - 1), 0)),
        ]

    fwd = block_specs(lambda j: j)
    bwd = block_specs(lambda j: nblk - 1 - j)
    both_dirs = _fixed_spec((None, 2, 1, w), (l, 0, 0, 0))
    state_spec = pl.BlockSpec((2, None, 1, w), lambda b, j: (0, b, 0, 0))
    return pl.pallas_call(
        functools.partial(_rg_kernel, nblk=nblk),
        grid=(bsz, nblk),
        in_specs=fwd + bwd + [
            state_spec,
            _fixed_spec((None,) + conv_w.shape[1:], (l, 0, 0)),
            _fixed_spec((None, 1, w), (l, 0, 0)),
            both_dirs,
            _fixed_spec((None,) + wg.shape[1:], (l, 0, 0, 0, 0)),
            both_dirs,
            both_dirs,
        ],
        out_specs=[fwd[0], bwd[0], state_spec],
        out_shape=[jax.ShapeDtypeStruct(x4.shape, F32)] * 2 + [jax.ShapeDtypeStruct(carry0.shape, F32)],
        scratch_shapes=[pltpu.VMEM((2, 1, w), F32)],
        compiler_params=_params(("arbitrary", "arbitrary")),
        name="rg",
    )(x4, x4, x4, x4, x4, x4, carry0, conv_w, conv_b, lam, wg, ba, bi)


def _level_ref(b, m, r):
    c, w = b.shape
    if m >= SUBLANES:
        b3 = b.reshape(c // m, m, w)
        return jnp.broadcast_to(b3[:, r:r + 1, :], (c // m, m, w)).reshape(c, w)
    bt = b.reshape(c // SUBLANES, SUBLANES, w)
    sub = lax.broadcasted_iota(jnp.int32, (1, SUBLANES, 1), 1)
    out = None
    for blk in range(SUBLANES // m):
        src = blk * m + r
        piece = jnp.broadcast_to(bt[:, src:src + 1, :], bt.shape)
        out = piece if out is None else jnp.where(sub >= blk * m, piece, out)
    return out.reshape(c, w)


def _halves_select(m, q_rows, q, k, reverse):
    half = m // 2
    if half % SUBLANES:
        return jnp.where(q_rows, q, k)
    first, second = (q, k) if reverse else (k, q)
    pieces = []
    for lo in range(0, q.shape[0], m):
        pieces += [first[lo:lo + half], second[lo + half:lo + m]]
    return jnp.concatenate(pieces, axis=0)


def _dot_nt(a, b):
    return lax.dot_general(a, b, (((1,), (1,)), ((), ())), preferred_element_type=F32)


def _dot_tn(a, b):
    return lax.dot_general(a, b, (((0,), (0,)), ((), ())), preferred_element_type=F32)


def _gla_consts(reverse):
    c = GLA_CHUNK
    row = lax.broadcasted_iota(jnp.int32, (c, 1), 0)
    r2 = lax.broadcasted_iota(jnp.int32, (c, c), 0)
    c2 = lax.broadcasted_iota(jnp.int32, (c, c), 1)
    tri = jnp.where((c2 >= r2) if reverse else (c2 <= r2), 1.0, 0.0).astype(BF16)
    klane = lax.broadcasted_iota(jnp.int32, (1, LANES), 1)
    vlane = lax.broadcasted_iota(jnp.int32, (1, PAIR_VW), 1)
    s = lax.broadcasted_iota(jnp.int32, (2 * c, c), 0) % c
    t = lax.broadcasted_iota(jnp.int32, (2 * c, c), 1)
    levels = []
    m = c
    while m >= 2:
        half = m // 2
        late = (row % m) >= half
        q_rows = ~late if reverse else late
        same = (t // m) == (s // m)
        t_late = (t % m) >= half
        s_late = (s % m) >= half
        mask = same & (~t_late) & s_late if reverse else same & t_late & (~s_late)
        levels.append((m, q_rows, mask))
        m = half
    key_first = klane < GLA_DK
    srow = lax.broadcasted_iota(jnp.int32, (PAIR_VW, 1), 0)
    state_mask = (srow < GLA_DV) == key_first
    return dict(tri=tri, key_first=key_first, val_first=vlane < GLA_DV, state_mask=state_mask,
                levels=levels, diag=(t == s))


def _pair_blockdiag_rows(x, first_mask):
    zero = jnp.zeros_like(x)
    return jnp.concatenate([jnp.where(first_mask, x, zero), jnp.where(first_mask, zero, x)], axis=0)


def _gla_stages(streams):
    c = GLA_CHUNK
    pair_k = [slice(p * LANES, (p + 1) * LANES) for p in range(GLA_PAIRS)]
    pair_v = [slice(p * PAIR_VW, (p + 1) * PAIR_VW) for p in range(GLA_PAIRS)]

    def cumulative_decay():
        for s in streams:
            g, tri = s["g"], s["consts"]["tri"]
            g1 = g.astype(BF16)
            g2 = (g - g1.astype(F32)).astype(BF16)
            g3 = (g - g1.astype(F32) - g2.astype(F32)).astype(BF16)
            s["b"] = (jnp.dot(tri, g1, preferred_element_type=F32)
                      + jnp.dot(tri, g2, preferred_element_type=F32)
                      + jnp.dot(tri, g3, preferred_element_type=F32))

    def diagonal():
        for s in streams:
            kf = s["consts"]["key_first"]
            qb, kb = s["q"].astype(BF16), s["k"].astype(BF16)
            s["att"] = [jnp.where(s["consts"]["diag"],
                                  _dot_nt(_pair_blockdiag_rows(kb[:, kl], kf), qb[:, kl]), 0.0) for kl in pair_k]

    def level(lvl):
        for s in streams:
            m, q_rows, mask = s["consts"]["levels"][lvl]
            half = m // 2
            if m == 2:
                x = jnp.where(q_rows, s["q"] * jnp.exp2(s["g"]), s["k"])
            else:
                ref = _level_ref(s["b"], m, half if s["reverse"] else half - 1)
                x = jnp.exp2(-jnp.abs(s["b"] - ref)) * _halves_select(m, q_rows, s["q"], s["k"], s["reverse"])
            x = x.astype(BF16)
            kf = s["consts"]["key_first"]
            s["att"] = [jnp.where(mask, _dot_nt(_pair_blockdiag_rows(x[:, kl], kf), x[:, kl]), att)
                        for kl, att in zip(pair_k, s["att"])]

    def state_operands():
        for s in streams:
            b = s["b"]
            b_last = b[0:1] if s["reverse"] else b[c - 1:c]
            s["q_in"] = (s["q"] * jnp.exp2(b)).astype(BF16)
            s["k_hat"] = (s["k"] * jnp.exp2(b_last - b)).astype(BF16)
            s["decay"] = jnp.exp2(b_last)
            s["att"] = [att.astype(BF16) for att in s["att"]]

    def output_and_update(s):
        vf = s["consts"]["val_first"]
        for p, (kl, vl) in enumerate(zip(pair_k, pair_v)):
            vp = s["v"][:, vl]
            st = s["st_ref"][p]
            o = _dot_tn(s["att"][p], _pair_blockdiag_rows(vp, vf))
            o = o + _dot_nt(s["q_in"][:, kl], st.astype(BF16))
            s["o_ref"][:, vl] = o
            upd = _dot_tn(vp, s["k_hat"][:, kl])
            s["st_ref"][p] = st * s["decay"][:, kl] + jnp.where(s["consts"]["state_mask"], upd, 0.0)

    n_levels = len(streams[0]["consts"]["levels"])
    prep = [cumulative_decay, diagonal] + [functools.partial(level, lvl) for lvl in range(n_levels)]
    prep.append(state_operands)
    tail = [functools.partial(output_and_update, s) for s in streams]
    return prep, tail


def _interleave(first, second):
    done = 0
    for idx, thunk in enumerate(first):
        thunk()
        upto = (len(second) * (idx + 1)) // len(first)
        for extra in second[done:upto]:
            extra()
        done = upto


def _gla_kernel(qf_ref, kf_ref, vf_ref, gf_ref, qb_ref, kb_ref, vb_ref, gb_ref, s0_ref,
                of_ref, ob_ref, sout_ref, st_ref, *, chunks):
    @pl.when(pl.program_id(0) == 0)
    def _():
        st_ref[...] = s0_ref[...]

    bsz = qf_ref.shape[0]
    dirs = ((0, qf_ref, kf_ref, vf_ref, gf_ref, of_ref), (1, qb_ref, kb_ref, vb_ref, gb_ref, ob_ref))
    consts = [_gla_consts(reverse=False), _gla_consts(reverse=True)]

    def chunk_stages(ci):
        streams = []
        for d, q_ref, k_ref, v_ref, g_ref, o_ref in dirs:
            rows = pl.ds(((chunks - 1 - ci) if d else ci) * GLA_CHUNK, GLA_CHUNK)
            for bi in range(bsz):
                streams.append(dict(q=q_ref[bi, rows, :], k=k_ref[bi, rows, :], v=v_ref[bi, rows, :],
                                    g=g_ref[bi, rows, :], st_ref=st_ref.at[d, bi], o_ref=o_ref.at[bi, rows],
                                    consts=consts[d], reverse=bool(d)))
        return _gla_stages(streams)

    prep, tail = chunk_stages(0)
    for thunk in prep:
        thunk()
    for ci in range(chunks):
        if ci + 1 < chunks:
            next_prep, next_tail = chunk_stages(ci + 1)
            _interleave(next_prep, tail)
            tail = next_tail
        else:
            for thunk in tail:
                thunk()

    @pl.when(pl.program_id(0) == pl.num_programs(0) - 1)
    def _():
        sout_ref[...] = st_ref[...]


def _gla_call(q, k, v, lg, state0):
    bsz, n, kw = q.shape
    vw = v.shape[2]
    chunks = min(GLA_CHUNKS_PER_STEP, n // GLA_CHUNK)
    br = chunks * GLA_CHUNK
    nstep = n // br

    def fwd(i):
        return (0, i, 0)

    def bwd(i):
        return (0, nstep - 1 - i, 0)

    return pl.pallas_call(
        functools.partial(_gla_kernel, chunks=chunks),
        grid=(nstep,),
        in_specs=[
            pl.BlockSpec((bsz, br, kw), fwd),
            pl.BlockSpec((bsz, br, kw), fwd),
            pl.BlockSpec((bsz, br, vw), fwd),
            pl.BlockSpec((bsz, br, kw), fwd),
            pl.BlockSpec((bsz, br, kw), bwd),
            pl.BlockSpec((bsz, br, kw), bwd),
            pl.BlockSpec((bsz, br, vw), bwd),
            pl.BlockSpec((bsz, br, kw), lambda i: (0, nstep - 1 - i, 1)),
            _full_spec(state0),
        ],
        out_specs=[pl.BlockSpec((bsz, br, vw), fwd), pl.BlockSpec((bsz, br, vw), bwd),
                   pl.BlockSpec(state0.shape, lambda i: (0,) * state0.ndim)],
        out_shape=[jax.ShapeDtypeStruct((bsz, n, vw), F32)] * 2
                  + [jax.ShapeDtypeStruct(state0.shape, F32)],
        scratch_shapes=[pltpu.VMEM(state0.shape, F32)],
        compiler_params=_params(("arbitrary",)),
        name="gla",
    )(q, k, v, lg, q, k, v, lg, state0)


def _pair_blockdiag(w):
    hd = w.shape[-1]
    per = LANES // hd
    lead = w.shape[:-3]
    wg = w.reshape(lead + (RG_HEADS // per, per, hd, hd))
    eye = jnp.eye(per, dtype=w.dtype)
    return jnp.einsum('...gpij,pq->...gpiqj', wg, eye).reshape(lead + (RG_HEADS // per, LANES, LANES))


def kernel(x, c, ctx, c_ctx, w_mod, b_mod, norm_g, ffn_w1, ffn_w3, ffn_w2, w_in, conv_w, conv_b,
           rg_lam, rg_wa, rg_ba, rg_wi, rg_bi, gla_wup, gla_bup, gla_norm_g, w_out, final_g):
    bsz, t, d = x.shape
    n_ctx = ctx.shape[1]
    depth = w_mod.shape[0]
    assert t % TOKEN_BLOCK == 0 and t % FFN_TOKENS == 0 and t % (GRID_W * SUBLANES) == 0
    assert n_ctx % (SUBLANES * SUBLANES) == 0 and n_ctx % GLA_CHUNK == 0 and bsz * n_ctx <= TOKEN_BLOCK

    rows = -(-(bsz + 1) // SUBLANES) * SUBLANES
    cvec = jnp.zeros((rows, d), F32).at[0].set(c_ctx).at[1:bsz + 1].set(c)
    mods = _mod_call(cvec, w_mod, b_mod).reshape(depth, rows, N_MOD, d)

    w1, w3, w2 = ffn_w1.astype(BF16), ffn_w3.astype(BF16), ffn_w2.astype(BF16)
    w_in_b, w_out_b = w_in.astype(BF16), w_out.astype(BF16)
    norm_g4 = norm_g.reshape(depth, 3, 1, d)
    zero = jnp.zeros((depth, GLA_RANK, GLA_KW), F32)
    wup_bd = jnp.concatenate([jnp.concatenate([gla_wup[:, 0], zero], axis=2),
                              jnp.concatenate([zero, gla_wup[:, 1]], axis=2)], axis=1).astype(BF16)
    bup = gla_bup.reshape(depth, 1, 2 * GLA_KW)
    wg = jnp.concatenate([_pair_blockdiag(rg_wa), _pair_blockdiag(rg_wi)], axis=-1).astype(BF16)
    conv_b3 = conv_b.reshape(depth, 1, RG_WIDTH)
    lam4 = rg_lam.reshape(depth, 2, 1, RG_WIDTH)
    ba4 = rg_ba.reshape(depth, 2, 1, RG_WIDTH)
    bi4 = rg_bi.reshape(depth, 2, 1, RG_WIDTH)
    gnorm = gla_norm_g.reshape(depth, 1, GLA_VW)
    fg = final_g.reshape(1, d)

    h_c = ctx.reshape(1, bsz * n_ctx, d)
    grid_rows = t // GRID_W
    h_l = x.reshape(bsz, grid_rows, GRID_W, d).swapaxes(1, 2).reshape(bsz, t, d)
    ctx_seg = n_ctx // SUBLANES
    for l in range(depth):
        last = l == depth - 1
        ffn = functools.partial(_ffn_call, mods=mods, norm_g4=norm_g4, w1=w1, w3=w3, w2=w2, final_g=fg, l=l)
        h_l = ffn(h_l, half=0, is_ctx=False, final=False)
        h_c = ffn(h_c, half=0, is_ctx=True, final=False)

        proj = functools.partial(_proj_call, mods=mods, norm_g4=norm_g4, w_in=w_in_b, wup_bd=wup_bd,
                                 bup=bup, l=l)
        xr_l, gr_l, q_l, k_l, v_l, og_l, lg_l = proj(h_l, is_ctx=False, seqs_per_block=0)
        xr_c, gr_c, q_c, k_c, v_c, og_c, lg_c = proj(h_c, is_ctx=True, seqs_per_block=bsz)

        rg = functools.partial(_rg_call, conv_w=conv_w, conv_b=conv_b3, lam=lam4, wg=wg, ba=ba4, bi=bi4, l=l)
        *scans_c, carry = rg(xr_c.reshape(bsz, ctx_seg, SUBLANES, RG_WIDTH),
                             jnp.zeros((2, bsz, 1, RG_WIDTH), F32))
        *scans_l, _ = rg(xr_l.reshape(bsz, GRID_W, grid_rows, RG_WIDTH), carry)
        scans_c = [a.reshape(xr_c.shape) for a in scans_c]
        scans_l = [a.reshape(xr_l.shape) for a in scans_l]

        per_seq = lambda a: a.reshape(bsz, n_ctx, a.shape[-1])
        state0 = jnp.zeros((2, bsz, GLA_PAIRS, PAIR_VW, LANES), F32)
        oc_f, oc_b, state = _gla_call(per_seq(q_c), per_seq(k_c), per_seq(v_c), per_seq(lg_c), state0)
        ol_f, ol_b, _ = _gla_call(q_l, k_l, v_l, lg_l, state)

        mixer_l = dict(gr=gr_l, hf=scans_l[0], hb=scans_l[1], o_f=ol_f, o_b=ol_b, og=og_l, gnorm=gnorm,
                       w_out=w_out_b, seqs_per_block=0)
        h_l = ffn(h_l, half=1, is_ctx=False, final=last, mixer=mixer_l)
        if not last:
            flat = lambda a: a.reshape(1, bsz * n_ctx, a.shape[-1])
            mixer_c = dict(gr=gr_c, hf=scans_c[0], hb=scans_c[1], o_f=flat(oc_f), o_b=flat(oc_b), og=og_c,
                           gnorm=gnorm, w_out=w_out_b, seqs_per_block=bsz)
            h_c = ffn(h_c, half=1, is_ctx=True, final=False, mixer=mixer_c)
    return h_l.reshape(bsz, GRID_W, grid_rows, d).swapaxes(1, 2).reshape(bsz, t, d)
```

```python
import functools
import math

import jax
import jax.numpy as jnp
from jax import lax
from jax.experimental import pallas as pl
from jax.experimental.pallas import tpu as pltpu

F32 = jnp.float32
BF16 = jnp.bfloat16

LANES = 128
SUBLANES = 8
FFN_TOKENS = 512
TOKEN_BLOCK = 1024
SUB_ROWS = 256
FF_STEP = 1024
GLA_CHUNKS_PER_STEP = 16
VMEM_LIMIT = 56 * 1024 * 1024

N_MOD = 9
EPS = 1e-6
GRID_W = 64
RG_WIDTH = 512
RG_HEADS = 8
RG_C = 8.0
GLA_HEADS = 4
GLA_DK = 64
GLA_DV = 128
GLA_KW = GLA_HEADS * GLA_DK
GLA_VW = GLA_HEADS * GLA_DV
GLA_RANK = 16
GLA_GATE_NORM = 16.0
GLA_CHUNK = 64
GLA_PAIRS = GLA_KW // LANES
PAIR_VW = GLA_VW // GLA_PAIRS
LOG2E = math.log2(math.e)


def _params(sem):
    return pltpu.CompilerParams(dimension_semantics=sem, vmem_limit_bytes=VMEM_LIMIT)


def _fixed_spec(shape, index):
    return pl.BlockSpec(shape, lambda *_: index, pipeline_mode=pl.Buffered(1))


def _full_spec(arr):
    return _fixed_spec(arr.shape, (0,) * arr.ndim)


def _sigmoid(x):
    return 1.0 / (1.0 + jnp.exp(-x))


def _softplus(x):
    return jnp.maximum(x, 0.0) + jnp.log(1.0 + jnp.exp(-jnp.abs(x)))


def _ada_norm(x, m, g, j):
    ms = jnp.mean(x * x, axis=-1, keepdims=True)
    gain = g * (1.0 + m[3 * j + 1:3 * j + 2])
    return x * lax.rsqrt(ms + EPS) * gain + m[3 * j:3 * j + 1]


def _token_block(n, target):
    return min(n, target)


def _row_blocks(n_rows):
    step = min(SUB_ROWS, n_rows)
    return [slice(r, r + step) for r in range(0, n_rows, step)]


def _staged(blocks, prologue, matmul, epilogue):
    operand = prologue(blocks[0])
    pending = None
    for i, rows in enumerate(blocks):
        acc = matmul(operand)
        if i + 1 < len(blocks):
            operand = prologue(blocks[i + 1])
        if pending is not None:
            epilogue(*pending)
        pending = (rows, acc)
    epilogue(*pending)


def _mod_kernel(c_ref, w_ref, b_ref, o_ref):
    c = c_ref[...]
    sc = (c * _sigmoid(c)).astype(BF16)
    o_ref[...] = jnp.dot(sc, w_ref[...].astype(BF16), preferred_element_type=F32) + b_ref[...]


def _mod_call(cvec, w_mod, b_mod):
    depth, d, n = w_mod.shape
    rows = cvec.shape[0]
    tn = n // 4
    return pl.pallas_call(
        _mod_kernel,
        grid=(depth, n // tn),
        in_specs=[
            pl.BlockSpec((rows, d), lambda l, i: (0, 0)),
            pl.BlockSpec((None, d, tn), lambda l, i: (l, 0, i)),
            pl.BlockSpec((None, 1, tn), lambda l, i: (l, 0, i)),
        ],
        out_specs=pl.BlockSpec((None, rows, tn), lambda l, i: (l, 0, i)),
        out_shape=jax.ShapeDtypeStruct((depth, rows, n), F32),
        compiler_params=_params(("parallel", "parallel")),
        name="mod",
    )(cvec, w_mod, b_mod.reshape(depth, 1, n))


def _mod_spec(mods, l, is_ctx):
    d = mods.shape[-1]
    if is_ctx:
        return pl.BlockSpec((None, None, N_MOD, d), lambda b, i: (l, 0, 0, 0))
    return pl.BlockSpec((None, None, N_MOD, d), lambda b, i: (l, b + 1, 0, 0))


def _tok_spec(tm, w):
    return pl.BlockSpec((None, tm, w), lambda b, i: (b, i, 0))


def _store_interleaved(ref, val):
    nb, seg_len, _ = ref.shape
    w = val.shape[1]
    for bb in range(nb):
        for s in range(SUBLANES):
            r0 = (bb * SUBLANES + s) * seg_len
            ref[bb, :, s * w:(s + 1) * w] = val[r0:r0 + seg_len]


def _load_interleaved(ref):
    nb, _, sw = ref.shape
    w = sw // SUBLANES
    return jnp.concatenate([ref[bb, :, s * w:(s + 1) * w]
                            for bb in range(nb) for s in range(SUBLANES)], axis=0)


def _interleaved_spec(nb, seg_len, w, nblk):
    return pl.BlockSpec((nb, seg_len, SUBLANES * w), lambda b, i: (b * nblk + i, 0, 0))


def _mixer_rows(rows, gr_ref, hf_ref, hb_ref, of_ref, ob_ref, og_ref, gn):
    gr = gr_ref[rows, :]
    gelu = 0.5 * gr * (1.0 + jnp.tanh(0.7978845608028654 * (gr + 0.044715 * (gr * gr * gr))))
    if len(hf_ref.shape) == 3:
        rg = gelu * (_load_interleaved(hf_ref) + _load_interleaved(hb_ref))
    else:
        rg = gelu * (hf_ref[rows, :] + hb_ref[rows, :])
    o = of_ref[rows, :] + ob_ref[rows, :]
    og = og_ref[rows, :]
    parts = []
    for hd in range(GLA_HEADS):
        sl = slice(hd * GLA_DV, (hd + 1) * GLA_DV)
        oh = o[:, sl]
        ms = jnp.mean(oh * oh, axis=-1, keepdims=True)
        parts.append(oh * lax.rsqrt(ms + EPS) * gn[:, sl])
    gla = jnp.concatenate(parts, axis=-1) * (og * _sigmoid(og))
    return jnp.concatenate([rg, gla], axis=-1).astype(BF16)


def _ffn_kernel(*refs, j, final, ff_chunks, with_mixer):
    if with_mixer:
        (x_ref, m_ref, g_ref, gr_ref, hf_ref, hb_ref, of_ref, ob_ref, og_ref, gn_ref, wo_ref,
         w1_ref, w3_ref, w2_ref, fg_ref, o_ref) = refs
        whole = len(hf_ref.shape) == 3
    else:
        x_ref, m_ref, g_ref, w1_ref, w3_ref, w2_ref, fg_ref, o_ref = refs
        whole = False
    m = m_ref[...]
    n_rows = x_ref.shape[0]

    def prologue(rows):
        x = x_ref[rows, :]
        if with_mixer:
            mix = _mixer_rows(rows, gr_ref, hf_ref, hb_ref, of_ref, ob_ref, og_ref, gn_ref[...])
            x = x + m[5:6] * jnp.dot(mix, wo_ref[...], preferred_element_type=F32)
        return x, _ada_norm(x, m, g_ref[...], j).astype(BF16)

    def matmul(operand):
        x, u = operand
        y = None
        for lo, hi in ff_chunks:
            h1 = jnp.dot(u, w1_ref[:, lo:hi], preferred_element_type=F32)
            h3 = jnp.dot(u, w3_ref[:, lo:hi], preferred_element_type=F32)
            act = (h1 * _sigmoid(h1) * h3).astype(BF16)
            part = jnp.dot(act, w2_ref[lo:hi, :], preferred_element_type=F32)
            y = part if y is None else y + part
        return x, y

    def epilogue(rows, acc):
        x, y = acc
        out = x + (0.5 * m[3 * j + 2:3 * j + 3]) * y
        if final:
            ms = jnp.mean(out * out, axis=-1, keepdims=True)
            out = out * lax.rsqrt(ms + EPS) * fg_ref[...]
        o_ref[rows, :] = out

    _staged([slice(0, n_rows)] if whole else _row_blocks(n_rows), prologue, matmul, epilogue)


def _ffn_call(h, mods, norm_g4, w1, w3, w2, final_g, *, l, half, is_ctx, final, mixer=None):
    bsz, n, d = h.shape
    dff = w1.shape[-1]
    tm = _token_block(n, FFN_TOKENS)
    nblk = n // tm
    j = 2 * half
    ff_chunks = tuple((lo, min(lo + FF_STEP, dff)) for lo in range(0, dff, FF_STEP))
    operands = [h, mods, norm_g4]
    in_specs = [_tok_spec(tm, d), _mod_spec(mods, l, is_ctx), _fixed_spec((None, None, 1, d), (l, j, 0, 0))]
    if mixer is not None:
        spb = mixer["seqs_per_block"]
        if spb:
            scan_spec = _interleaved_spec(spb, tm // (spb * SUBLANES), RG_WIDTH, nblk)
        else:
            scan_spec = _tok_spec(tm, RG_WIDTH)
        operands += [mixer[name] for name in ("gr", "hf", "hb", "o_f", "o_b", "og", "gnorm", "w_out")]
        in_specs += [_tok_spec(tm, RG_WIDTH), scan_spec, scan_spec,
                     _tok_spec(tm, GLA_VW), _tok_spec(tm, GLA_VW), _tok_spec(tm, GLA_VW),
                     _fixed_spec((None, 1, GLA_VW), (l, 0, 0)),
                     _fixed_spec((None,) + mixer["w_out"].shape[1:], (l, 0, 0))]
    operands += [w1, w3, w2, final_g]
    in_specs += [_fixed_spec((None, None, d, dff), (l, half, 0, 0)),
                 _fixed_spec((None, None, d, dff), (l, half, 0, 0)),
                 _fixed_spec((None, None, dff, d), (l, half, 0, 0)),
                 _full_spec(final_g)]
    return pl.pallas_call(
        functools.partial(_ffn_kernel, j=j, final=final, ff_chunks=ff_chunks, with_mixer=mixer is not None),
        grid=(bsz, nblk),
        in_specs=in_specs,
        out_specs=_tok_spec(tm, d),
        out_shape=jax.ShapeDtypeStruct((bsz, n, d), F32),
        compiler_params=_params(("parallel", "parallel")),
        name="mix_ffn" if mixer is not None else "ffn",
    )(*operands)


def _proj_kernel(x_ref, m_ref, g_ref, w_ref, wup_ref, bup_ref,
                 xr_ref, gr_ref, q_ref, k_ref, v_ref, og_ref, lg_ref):
    interleaved = len(xr_ref.shape) == 3
    m = m_ref[...]

    def prologue(rows):
        return _ada_norm(x_ref[rows, :], m, g_ref[...], 1).astype(BF16)

    def matmul(u):
        return jnp.dot(u, w_ref[...], preferred_element_type=F32)

    def epilogue(rows, p):
        o = 0
        if interleaved:
            _store_interleaved(xr_ref, p[:, o:o + RG_WIDTH])
        else:
            xr_ref[rows, :] = p[:, o:o + RG_WIDTH]
        o += RG_WIDTH
        gr_ref[rows, :] = p[:, o:o + RG_WIDTH]; o += RG_WIDTH
        q_ref[rows, :] = p[:, o:o + GLA_KW] * (GLA_DK ** -0.5); o += GLA_KW
        k_ref[rows, :] = p[:, o:o + GLA_KW]; o += GLA_KW
        v_ref[rows, :] = p[:, o:o + GLA_VW].astype(BF16); o += GLA_VW
        og_ref[rows, :] = p[:, o:o + GLA_VW]; o += GLA_VW
        lr = p[:, o:o + 2 * GLA_RANK].astype(BF16)
        z = jnp.dot(lr, wup_ref[...], preferred_element_type=F32) + bup_ref[...]
        lg_ref[rows, :] = _softplus(-z) * (-LOG2E / GLA_GATE_NORM)

    n_rows = x_ref.shape[0]
    _staged([slice(0, n_rows)] if interleaved else _row_blocks(n_rows), prologue, matmul, epilogue)


def _proj_call(h, mods, norm_g4, w_in, wup_bd, bup, *, l, is_ctx, seqs_per_block):
    bsz, n, d = h.shape
    tm = _token_block(n, TOKEN_BLOCK)
    nblk = n // tm
    outs = ((RG_WIDTH, F32), (GLA_KW, F32), (GLA_KW, F32), (GLA_VW, BF16), (GLA_VW, F32), (2 * GLA_KW, F32))
    if seqs_per_block:
        seg_len = tm // (seqs_per_block * SUBLANES)
        xr_shape = (bsz * nblk * seqs_per_block, seg_len, SUBLANES * RG_WIDTH)
        xr_spec = _interleaved_spec(seqs_per_block, seg_len, RG_WIDTH, nblk)
    else:
        xr_shape = (bsz, n, RG_WIDTH)
        xr_spec = _tok_spec(tm, RG_WIDTH)
    return pl.pallas_call(
        _proj_kernel,
        grid=(bsz, nblk),
        in_specs=[
            _tok_spec(tm, d),
            _mod_spec(mods, l, is_ctx),
            _fixed_spec((None, None, 1, d), (l, 1, 0, 0)),
            _fixed_spec((None,) + w_in.shape[1:], (l, 0, 0)),
            _fixed_spec((None,) + wup_bd.shape[1:], (l, 0, 0)),
            _fixed_spec((None, 1, bup.shape[-1]), (l, 0, 0)),
        ],
        out_specs=[xr_spec] + [_tok_spec(tm, w) for w, _ in outs],
        out_shape=[jax.ShapeDtypeStruct(xr_shape, F32)]
                  + [jax.ShapeDtypeStruct((bsz, n, w), dt) for w, dt in outs],
        compiler_params=_params(("parallel", "parallel")),
        name="proj",
    )(h, mods, norm_g4, w_in, wup_bd, bup)


def _rg_kernel(xf_ref, xfp_ref, xfn_ref, xb_ref, xbp_ref, xbn_ref, c0_ref, cw_ref, cb_ref, lam_ref, wg_ref,
               ba_ref, bi_ref, of_ref, ob_ref, cout_ref, carry_ref, *, nblk):
    for d, (x_ref, xp_ref, xn_ref, o_ref) in enumerate(((xf_ref, xfp_ref, xfn_ref, of_ref),
                                                         (xb_ref, xbp_ref, xbn_ref, ob_ref))):
        _rg_direction(x_ref, xp_ref, xn_ref, c0_ref.at[d], cw_ref, cb_ref, lam_ref.at[d], wg_ref.at[d],
                      ba_ref.at[d], bi_ref.at[d], o_ref, cout_ref.at[d], carry_ref.at[d],
                      reverse=bool(d), nblk=nblk)


def _rg_direction(x_ref, xp_ref, xn_ref, c0_ref, cw_ref, cb_ref, lam_ref, wg_ref, ba_ref, bi_ref,
                  o_ref, cout_ref, carry_ref, *, reverse, nblk):
    seg_len = x_ref.shape[0]
    j = pl.program_id(1)
    blk = (nblk - 1 - j) if reverse else j

    @pl.when(j == 0)
    def _():
        carry_ref[...] = c0_ref[...]

    x = x_ref[...]
    last = SUBLANES - 1
    prev2 = jnp.where(blk == 0, 0.0, xp_ref[last - 1, last:, :])
    prev1 = jnp.where(blk == 0, 0.0, xp_ref[last, last:, :])
    next1 = jnp.where(blk == nblk - 1, 0.0, xn_ref[0, 0:1, :])
    row8 = lax.broadcasted_iota(jnp.int32, (SUBLANES, 1), 0)

    def from_prev_segment(tile, halo):
        return jnp.where(row8 == 0, halo, pltpu.roll(tile, 1, axis=0))

    def from_next_segment(tile, halo):
        return jnp.where(row8 == last, halo, pltpu.roll(tile, last, axis=0))

    xm2 = from_prev_segment(x[seg_len - 2], prev2)
    xm1 = from_prev_segment(x[seg_len - 1], prev1)
    xp1 = from_next_segment(x[0], next1)
    xe = jnp.concatenate([xm2[None], xm1[None], x, xp1[None]], axis=0)
    cw = cw_ref[...]
    xc = xe[0:seg_len] * cw[0:1] + xe[1:seg_len + 1] * cw[1:2]
    xc = xc + xe[2:seg_len + 2] * cw[2:3]
    xc = xc + xe[3:seg_len + 3] * cw[3:4]
    xc = (xc + cb_ref[...]).reshape(seg_len * SUBLANES, RG_WIDTH)

    neg_c_sp = (-RG_C) * _softplus(-lam_ref[...])
    order = range(seg_len - 1, -1, -1) if reverse else range(seg_len)
    for g in range(RG_WIDTH // LANES):
        sl = slice(g * LANES, (g + 1) * LANES)
        xg = xc[:, sl]
        pre = jnp.dot(xg.astype(BF16), wg_ref[g], preferred_element_type=F32)
        r = _sigmoid(pre[:, :LANES] + ba_ref[:, sl])
        i = _sigmoid(pre[:, LANES:] + bi_ref[:, sl])
        log_a = r * neg_c_sp[:, sl]
        a = jnp.exp(log_a)
        bv = jnp.sqrt(1.0 - a * a) * (i * xg)
        hs, ps = [None] * seg_len, [None] * seg_len
        h_run = p_run = None
        for t in order:
            rows = slice(t * SUBLANES, (t + 1) * SUBLANES)
            if h_run is None:
                h_run, p_run = bv[rows], a[rows]
            else:
                h_run = a[rows] * h_run + bv[rows]
                p_run = a[rows] * p_run
            hs[t], ps[t] = h_run, p_run
        pc, hc = p_run, h_run
        s = 1
        while s < SUBLANES:
            shift = SUBLANES - s if reverse else s
            valid = (row8 < SUBLANES - s) if reverse else (row8 >= s)
            hc = jnp.where(valid, pc * pltpu.roll(hc, shift, axis=0) + hc, hc)
            pc = jnp.where(valid, pc * pltpu.roll(pc, shift, axis=0), pc)
            s *= 2
        carry = carry_ref[:, sl]
        seg_end = pc * carry + hc
        if reverse:
            seg_in = jnp.where(row8 == last, carry, pltpu.roll(seg_end, last, axis=0))
            carry_ref[:, sl] = seg_end[0:1]
        else:
            seg_in = jnp.where(row8 == 0, carry, pltpu.roll(seg_end, 1, axis=0))
            carry_ref[:, sl] = seg_end[last:]
        for t in range(seg_len):
            o_ref[t, :, sl] = hs[t] + ps[t] * seg_in

    @pl.when(j == nblk - 1)
    def _():
        cout_ref[...] = carry_ref[...]


def _rg_call(x4, carry0, conv_w, conv_b, lam, wg, ba, bi, *, l):
    bsz, seg_len, nseg, w = x4.shape
    nblk = nseg // SUBLANES
    tiles = seg_len // SUBLANES

    def block_specs(blk):
        return [
            pl.BlockSpec((None, seg_len, SUBLANES, w), lambda b, j: (b, 0, blk(j), 0)),
            pl.BlockSpec((None, SUBLANES, SUBLANES, w),
                         lambda b, j: (b, tiles - 1, jnp.maximum(blk(j) - 1, 0), 0)),
            pl.BlockSpec((None, SUBLANES, SUBLANES, w),
                         lambda b, j: (b, 0, jnp.minimum(blk(j) + 1, nblk - 1), 0)),
        ]

    fwd = block_specs(lambda j: j)
    bwd = block_specs(lambda j: nblk - 1 - j)
    both_dirs = _fixed_spec((None, 2, 1, w), (l, 0, 0, 0))
    state_spec = pl.BlockSpec((2, None, 1, w), lambda b, j: (0, b, 0, 0))
    return pl.pallas_call(
        functools.partial(_rg_kernel, nblk=nblk),
        grid=(bsz, nblk),
        in_specs=fwd + bwd + [
            state_spec,
            _fixed_spec((None,) + conv_w.shape[1:], (l, 0, 0)),
            _fixed_spec((None, 1, w), (l, 0, 0)),
            both_dirs,
            _fixed_spec((None,) + wg.shape[1:], (l, 0, 0, 0, 0)),
            both_dirs,
            both_dirs,
        ],
        out_specs=[fwd[0], bwd[0], state_spec],
        out_shape=[jax.ShapeDtypeStruct(x4.shape, F32)] * 2 + [jax.ShapeDtypeStruct(carry0.shape, F32)],
        scratch_shapes=[pltpu.VMEM((2, 1, w), F32)],
        compiler_params=_params(("arbitrary", "arbitrary")),
        name="rg",
    )(x4, x4, x4, x4, x4, x4, carry0, conv_w, conv_b, lam, wg, ba, bi)


def _level_ref(b, m, r):
    c, w = b.shape
    if m >= SUBLANES:
        b3 = b.reshape(c // m, m, w)
        return jnp.broadcast_to(b3[:, r:r + 1, :], (c // m, m, w)).reshape(c, w)
    bt = b.reshape(c // SUBLANES, SUBLANES, w)
    sub = lax.broadcasted_iota(jnp.int32, (1, SUBLANES, 1), 1)
    out = None
    for blk in range(SUBLANES // m):
        src = blk * m + r
        piece = jnp.broadcast_to(bt[:, src:src + 1, :], bt.shape)
        out = piece if out is None else jnp.where(sub >= blk * m, piece, out)
    return out.reshape(c, w)


def _halves_select(m, q_rows, q, k, reverse):
    half = m // 2
    if half % SUBLANES:
        return jnp.where(q_rows, q, k)
    first, second = (q, k) if reverse else (k, q)
    pieces = []
    for lo in range(0, q.shape[0], m):
        pieces += [first[lo:lo + half], second[lo + half:lo + m]]
    return jnp.concatenate(pieces, axis=0)


def _dot_nt(a, b):
    return lax.dot_general(a, b, (((1,), (1,)), ((), ())), preferred_element_type=F32)


def _dot_tn(a, b):
    return lax.dot_general(a, b, (((0,), (0,)), ((), ())), preferred_element_type=F32)


def _gla_consts(reverse):
    c = GLA_CHUNK
    row = lax.broadcasted_iota(jnp.int32, (c, 1), 0)
    r2 = lax.broadcasted_iota(jnp.int32, (c, c), 0)
    c2 = lax.broadcasted_iota(jnp.int32, (c, c), 1)
    tri = jnp.where((c2 >= r2) if reverse else (c2 <= r2), 1.0, 0.0).astype(BF16)
    klane = lax.broadcasted_iota(jnp.int32, (1, LANES), 1)
    vlane = lax.broadcasted_iota(jnp.int32, (1, PAIR_VW), 1)
    s = lax.broadcasted_iota(jnp.int32, (2 * c, c), 0) % c
    t = lax.broadcasted_iota(jnp.int32, (2 * c, c), 1)
    levels = []
    m = c
    while m >= 2:
        half = m // 2
        late = (row % m) >= half
        q_rows = ~late if reverse else late
        same = (t // m) == (s // m)
        t_late = (t % m) >= half
        s_late = (s % m) >= half
        mask = same & (~t_late) & s_late if reverse else same & t_late & (~s_late)
        levels.append((m, q_rows, mask))
        m = half
    key_first = klane < GLA_DK
    srow = lax.broadcasted_iota(jnp.int32, (PAIR_VW, 1), 0)
    state_mask = (srow < GLA_DV) == key_first
    return dict(tri=tri, key_first=key_first, val_first=vlane < GLA_DV, state_mask=state_mask,
                levels=levels, diag=(t == s))


def _pair_blockdiag_rows(x, first_mask):
    zero = jnp.zeros_like(x)
    return jnp.concatenate([jnp.where(first_mask, x, zero), jnp.where(first_mask, zero, x)], axis=0)


def _gla_stages(streams):
    c = GLA_CHUNK
    pair_k = [slice(p * LANES, (p + 1) * LANES) for p in range(GLA_PAIRS)]
    pair_v = [slice(p * PAIR_VW, (p + 1) * PAIR_VW) for p in range(GLA_PAIRS)]

    def cumulative_decay():
        for s in streams:
            g, tri = s["g"], s["consts"]["tri"]
            g1 = g.astype(BF16)
            g2 = (g - g1.astype(F32)).astype(BF16)
            g3 = (g - g1.astype(F32) - g2.astype(F32)).astype(BF16)
            s["b"] = (jnp.dot(tri, g1, preferred_element_type=F32)
                      + jnp.dot(tri, g2, preferred_element_type=F32)
                      + jnp.dot(tri, g3, preferred_element_type=F32))

    def diagonal():
        for s in streams:
            kf = s["consts"]["key_first"]
            qb, kb = s["q"].astype(BF16), s["k"].astype(BF16)
            s["att"] = [jnp.where(s["consts"]["diag"],
                                  _dot_nt(_pair_blockdiag_rows(kb[:, kl], kf), qb[:, kl]), 0.0) for kl in pair_k]

    def level(lvl):
        for s in streams:
            m, q_rows, mask = s["consts"]["levels"][lvl]
            half = m // 2
            if m == 2:
                x = jnp.where(q_rows, s["q"] * jnp.exp2(s["g"]), s["k"])
            else:
                ref = _level_ref(s["b"], m, half if s["reverse"] else half - 1)
                x = jnp.exp2(-jnp.abs(s["b"] - ref)) * _halves_select(m, q_rows, s["q"], s["k"], s["reverse"])
            x = x.astype(BF16)
            kf = s["consts"]["key_first"]
            s["att"] = [jnp.where(mask, _dot_nt(_pair_blockdiag_rows(x[:, kl], kf), x[:, kl]), att)
                        for kl, att in zip(pair_k, s["att"])]

    def state_operands():
        for s in streams:
            b = s["b"]
            b_last = b[0:1] if s["reverse"] else b[c - 1:c]
            s["q_in"] = (s["q"] * jnp.exp2(b)).astype(BF16)
            s["k_hat"] = (s["k"] * jnp.exp2(b_last - b)).astype(BF16)
            s["decay"] = jnp.exp2(b_last)
            s["att"] = [att.astype(BF16) for att in s["att"]]

    def output_and_update(s):
        vf = s["consts"]["val_first"]
        for p, (kl, vl) in enumerate(zip(pair_k, pair_v)):
            vp = s["v"][:, vl]
            st = s["st_ref"][p]
            o = _dot_tn(s["att"][p], _pair_blockdiag_rows(vp, vf))
            o = o + _dot_nt(s["q_in"][:, kl], st.astype(BF16))
            s["o_ref"][:, vl] = o
            upd = _dot_tn(vp, s["k_hat"][:, kl])
            s["st_ref"][p] = st * s["decay"][:, kl] + jnp.where(s["consts"]["state_mask"], upd, 0.0)

    n_levels = len(streams[0]["consts"]["levels"])
    prep = [cumulative_decay, diagonal] + [functools.partial(level, lvl) for lvl in range(n_levels)]
    prep.append(state_operands)
    tail = [functools.partial(output_and_update, s) for s in streams]
    return prep, tail


def _interleave(first, second):
    done = 0
    for idx, thunk in enumerate(first):
        thunk()
        upto = (len(second) * (idx + 1)) // len(first)
        for extra in second[done:upto]:
            extra()
        done = upto


def _gla_kernel(qf_ref, kf_ref, vf_ref, gf_ref, qb_ref, kb_ref, vb_ref, gb_ref, s0_ref,
                of_ref, ob_ref, sout_ref, st_ref, *, chunks):
    @pl.when(pl.program_id(0) == 0)
    def _():
        st_ref[...] = s0_ref[...]

    bsz = qf_ref.shape[0]
    dirs = ((0, qf_ref, kf_ref, vf_ref, gf_ref, of_ref), (1, qb_ref, kb_ref, vb_ref, gb_ref, ob_ref))
    consts = [_gla_consts(reverse=False), _gla_consts(reverse=True)]

    def chunk_stages(ci):
        streams = []
        for d, q_ref, k_ref, v_ref, g_ref, o_ref in dirs:
            rows = pl.ds(((chunks - 1 - ci) if d else ci) * GLA_CHUNK, GLA_CHUNK)
            for bi in range(bsz):
                streams.append(dict(q=q_ref[bi, rows, :], k=k_ref[bi, rows, :], v=v_ref[bi, rows, :],
                                    g=g_ref[bi, rows, :], st_ref=st_ref.at[d, bi], o_ref=o_ref.at[bi, rows],
                                    consts=consts[d], reverse=bool(d)))
        return _gla_stages(streams)

    prep, tail = chunk_stages(0)
    for thunk in prep:
        thunk()
    for ci in range(chunks):
        if ci + 1 < chunks:
            next_prep, next_tail = chunk_stages(ci + 1)
            _interleave(next_prep, tail)
            tail = next_tail
        else:
            for thunk in tail:
                thunk()

    @pl.when(pl.program_id(0) == pl.num_programs(0) - 1)
    def _():
        sout_ref[...] = st_ref[...]


def _gla_call(q, k, v, lg, state0):
    bsz, n, kw = q.shape
    vw = v.shape[2]
    chunks = min(GLA_CHUNKS_PER_STEP, n // GLA_CHUNK)
    br = chunks * GLA_CHUNK
    nstep = n // br

    def fwd(i):
        return (0, i, 0)

    def bwd(i):
        return (0, nstep - 1 - i, 0)

    return pl.pallas_call(
        functools.partial(_gla_kernel, chunks=chunks),
        grid=(nstep,),
        in_specs=[
            pl.BlockSpec((bsz, br, kw), fwd),
            pl.BlockSpec((bsz, br, kw), fwd),
            pl.BlockSpec((bsz, br, vw), fwd),
            pl.BlockSpec((bsz, br, kw), fwd),
            pl.BlockSpec((bsz, br, kw), bwd),
            pl.BlockSpec((bsz, br, kw), bwd),
            pl.BlockSpec((bsz, br, vw), bwd),
            pl.BlockSpec((bsz, br, kw), lambda i: (0, nstep - 1 - i, 1)),
            _full_spec(state0),
        ],
        out_specs=[pl.BlockSpec((bsz, br, vw), fwd), pl.BlockSpec((bsz, br, vw), bwd),
                   pl.BlockSpec(state0.shape, lambda i: (0,) * state0.ndim)],
        out_shape=[jax.ShapeDtypeStruct((bsz, n, vw), F32)] * 2
                  + [jax.ShapeDtypeStruct(state0.shape, F32)],
        scratch_shapes=[pltpu.VMEM(state0.shape, F32)],
        compiler_params=_params(("arbitrary",)),
        name="gla",
    )(q, k, v, lg, q, k, v, lg, state0)


def _pair_blockdiag(w):
    hd = w.shape[-1]
    per = LANES // hd
    lead = w.shape[:-3]
    wg = w.reshape(lead + (RG_HEADS // per, per, hd, hd))
    eye = jnp.eye(per, dtype=w.dtype)
    return jnp.einsum('...gpij,pq->...gpiqj', wg, eye).reshape(lead + (RG_HEADS // per, LANES, LANES))


def kernel(x, c, ctx, c_ctx, w_mod, b_mod, norm_g, ffn_w1, ffn_w3, ffn_w2, w_in, conv_w, conv_b,
           rg_lam, rg_wa, rg_ba, rg_wi, rg_bi, gla_wup, gla_bup, gla_norm_g, w_out, final_g):
    bsz, t, d = x.shape
    n_ctx = ctx.shape[1]
    depth = w_mod.shape[0]
    assert t % TOKEN_BLOCK == 0 and t % FFN_TOKENS == 0 and t % (GRID_W * SUBLANES) == 0
    assert n_ctx % (SUBLANES * SUBLANES) == 0 and n_ctx % GLA_CHUNK == 0 and bsz * n_ctx <= TOKEN_BLOCK

    rows = -(-(bsz + 1) // SUBLANES) * SUBLANES
    cvec = jnp.zeros((rows, d), F32).at[0].set(c_ctx).at[1:bsz + 1].set(c)
    mods = _mod_call(cvec, w_mod, b_mod).reshape(depth, rows, N_MOD, d)

    w1, w3, w2 = ffn_w1.astype(BF16), ffn_w3.astype(BF16), ffn_w2.astype(BF16)
    w_in_b, w_out_b = w_in.astype(BF16), w_out.astype(BF16)
    norm_g4 = norm_g.reshape(depth, 3, 1, d)
    zero = jnp.zeros((depth, GLA_RANK, GLA_KW), F32)
    wup_bd = jnp.concatenate([jnp.concatenate([gla_wup[:, 0], zero], axis=2),
                              jnp.concatenate([zero, gla_wup[:, 1]], axis=2)], axis=1).astype(BF16)
    bup = gla_bup.reshape(depth, 1, 2 * GLA_KW)
    wg = jnp.concatenate([_pair_blockdiag(rg_wa), _pair_blockdiag(rg_wi)], axis=-1).astype(BF16)
    conv_b3 = conv_b.reshape(depth, 1, RG_WIDTH)
    lam4 = rg_lam.reshape(depth, 2, 1, RG_WIDTH)
    ba4 = rg_ba.reshape(depth, 2, 1, RG_WIDTH)
    bi4 = rg_bi.reshape(depth, 2, 1, RG_WIDTH)
    gnorm = gla_norm_g.reshape(depth, 1, GLA_VW)
    fg = final_g.reshape(1, d)

    h_c = ctx.reshape(1, bsz * n_ctx, d)
    grid_rows = t // GRID_W
    h_l = x.reshape(bsz, grid_rows, GRID_W, d).swapaxes(1, 2).reshape(bsz, t, d)
    ctx_seg = n_ctx // SUBLANES
    for l in range(depth):
        last = l == depth - 1
        ffn = functools.partial(_ffn_call, mods=mods, norm_g4=norm_g4, w1=w1, w3=w3, w2=w2, final_g=fg, l=l)
        h_l = ffn(h_l, half=0, is_ctx=False, final=False)
        h_c = ffn(h_c, half=0, is_ctx=True, final=False)

        proj = functools.partial(_proj_call, mods=mods, norm_g4=norm_g4, w_in=w_in_b, wup_bd=wup_bd,
                                 bup=bup, l=l)
        xr_l, gr_l, q_l, k_l, v_l, og_l, lg_l = proj(h_l, is_ctx=False, seqs_per_block=0)
        xr_c, gr_c, q_c, k_c, v_c, og_c, lg_c = proj(h_c, is_ctx=True, seqs_per_block=bsz)

        rg = functools.partial(_rg_call, conv_w=conv_w, conv_b=conv_b3, lam=lam4, wg=wg, ba=ba4, bi=bi4, l=l)
        *scans_c, carry = rg(xr_c.reshape(bsz, ctx_seg, SUBLANES, RG_WIDTH),
                             jnp.zeros((2, bsz, 1, RG_WIDTH), F32))
        *scans_l, _ = rg(xr_l.reshape(bsz, GRID_W, grid_rows, RG_WIDTH), carry)
        scans_c = [a.reshape(xr_c.shape) for a in scans_c]
        scans_l = [a.reshape(xr_l.shape) for a in scans_l]

        per_seq = lambda a: a.reshape(bsz, n_ctx, a.shape[-1])
        state0 = jnp.zeros((2, bsz, GLA_PAIRS, PAIR_VW, LANES), F32)
        oc_f, oc_b, state = _gla_call(per_seq(q_c), per_seq(k_c), per_seq(v_c), per_seq(lg_c), state0)
        ol_f, ol_b, _ = _gla_call(q_l, k_l, v_l, lg_l, state)

        mixer_l = dict(gr=gr_l, hf=scans_l[0], hb=scans_l[1], o_f=ol_f, o_b=ol_b, og=og_l, gnorm=gnorm,
                       w_out=w_out_b, seqs_per_block=0)
        h_l = ffn(h_l, half=1, is_ctx=False, final=last, mixer=mixer_l)
        if not last:
            flat = lambda a: a.reshape(1, bsz * n_ctx, a.shape[-1])
            mixer_c = dict(gr=gr_c, hf=scans_c[0], hb=scans_c[1], o_f=flat(oc_f), o_b=flat(oc_b), og=og_c,
                           gnorm=gnorm, w_out=w_out_b, seqs_per_block=bsz)
            h_c = ffn(h_c, half=1, is_ctx=True, final=False, mixer=mixer_c)
    return h_l.reshape(bsz, GRID_W, grid_rows, d).swapaxes(1, 2).reshape(bsz, t, d)
```
